```python
import jax, jax.numpy as jnp
from jax import lax
import numpy as np

D_MODEL = 1024
BATCH = 4
SEQ = 8192
DEPTH = 2

CHUNK = 64
Q_BLOCK = 128
HEAD_DIM = 64
D_MIX = D_MODEL
A_HEADS = 6
A_LEFT_CHUNKS = 8
A_REL_MAX = 128
A_NUM_REL = CHUNK + A_REL_MAX
B_HEADS = 5
IDX_HEADS = 8
IDX_DIM = 32
TOPK_MAX = 256
C_HEADS = 5
C_Q_RANK = 384
C_KV_RANK = 256
C_NOPE = 64
C_ROPE = 32
C_V = 64
D_FF = 2816
ROPE_THETA = 10000.0
EPS = 1e-6
NEG = -1e30

A_W = A_HEADS * HEAD_DIM
B_W = B_HEADS * HEAD_DIM
C_W = C_HEADS * C_V
IN_SIZES = (A_W, A_W, A_W, B_W, B_W, B_W, IDX_HEADS * IDX_DIM, IDX_DIM, IDX_HEADS,
            C_Q_RANK, C_KV_RANK, C_ROPE)
D_IN = 3 * A_W + 3 * B_W + IDX_HEADS * IDX_DIM + IDX_DIM + IDX_HEADS + C_Q_RANK + C_KV_RANK + C_ROPE

kernel_name = "hybrid_streaming_chunkattn_dsa_mla_macaron"


def rmsnorm(x, g):
    x32 = x.astype(jnp.float32)
    y = x32 * lax.rsqrt(jnp.mean(x32 * x32, axis=-1, keepdims=True) + EPS)
    return (y * g.astype(jnp.float32)).astype(x.dtype)


def rope(x, pos):
    d = x.shape[-1]
    inv = ROPE_THETA ** (-jnp.arange(0, d, 2, dtype=jnp.float32) / d)
    ang = pos.astype(jnp.float32)[:, None] * inv[None, :]
    cos = jnp.cos(ang)[:, None, :]
    sin = jnp.sin(ang)[:, None, :]
    x32 = x.astype(jnp.float32)
    x1, x2 = x32[..., : d // 2], x32[..., d // 2:]
    return jnp.concatenate([x1 * cos - x2 * sin, x1 * sin + x2 * cos], axis=-1).astype(x.dtype)


def swiglu(x, w_gate, w_up, w_down):
    return (jax.nn.silu(x @ w_gate) * (x @ w_up)) @ w_down


def sweep_query_blocks(fn, bsz, seq):
    out = lax.map(fn, jnp.arange(seq // Q_BLOCK) * Q_BLOCK)
    return jnp.moveaxis(out, 0, 1).reshape(bsz, seq, -1)


def chunk_relpos_attention(q, k, v, rel_bias):
    bsz, seq, h, d = q.shape
    nc = seq // CHUNK
    nb = A_LEFT_CHUNKS + 1
    pad = A_LEFT_CHUNKS * CHUNK
    qc = q.reshape(bsz, nc, CHUNK, h, d)
    kp = jnp.pad(k, ((0, 0), (pad, 0), (0, 0), (0, 0))).reshape(bsz, nc + A_LEFT_CHUNKS, CHUNK, h, d)
    vp = jnp.pad(v, ((0, 0), (pad, 0), (0, 0), (0, 0))).reshape(bsz, nc + A_LEFT_CHUNKS, CHUNK, h, d)
    kb = jnp.stack([kp[:, j:j + nc] for j in range(nb)], axis=2).reshape(bsz, nc, nb * CHUNK, h, d)
    vb = jnp.stack([vp[:, j:j + nc] for j in range(nb)], axis=2).reshape(bsz, nc, nb * CHUNK, h, d)
    s = jnp.einsum('bcqhd,bckhd->bhcqk', qc, kb).astype(jnp.float32) * (d ** -0.5)
    qi = jnp.arange(CHUNK)[:, None]
    ki = jnp.arange(nb * CHUNK)[None, :]
    rel = pad + qi - ki
    ridx = jnp.clip(rel, -(CHUNK - 1), A_REL_MAX) + (CHUNK - 1)
    bias = rel_bias.astype(jnp.float32)[:, ridx]
    valid = (jnp.arange(nc)[:, None] + ki // CHUNK - A_LEFT_CHUNKS) >= 0
    s = jnp.where(valid[None, None, :, None, :], s + bias[None, :, None], NEG)
    p = jax.nn.softmax(s, axis=-1).astype(v.dtype)
    o = jnp.einsum('bhcqk,bckhd->bcqhd', p, vb)
    return o.reshape(bsz, seq, h * d)


def dsa_attention(q, k, v, q_idx, k_idx, w_idx):
    bsz, seq, h, d = q.shape
    topk = min(TOPK_MAX, seq // 4)
    kchunk = jnp.arange(seq) // CHUNK
    gather = jax.vmap(lambda kk, ii: kk[ii])

    def block(q0):
        tc = (q0 + jnp.arange(Q_BLOCK)) // CHUNK
        qi = lax.dynamic_slice_in_dim(q_idx, q0, Q_BLOCK, axis=1)
        wi = lax.dynamic_slice_in_dim(w_idx, q0, Q_BLOCK, axis=1).astype(jnp.float32) * (IDX_HEADS ** -0.5)
        qb = lax.dynamic_slice_in_dim(q, q0, Q_BLOCK, axis=1)
        dots = jnp.einsum('bqhd,bsd->bqhs', qi, k_idx).astype(jnp.float32) * (IDX_DIM ** -0.5)
        score = jnp.einsum('bqh,bqhs->bqs', wi, jax.nn.relu(dots))
        adm = kchunk[None, :] <= tc[:, None]
        score = jnp.where(adm[None], score, NEG)
        _, idx = lax.top_k(score, topk)
        ksel = gather(k, idx)
        vsel = gather(v, idx)
        valid = (idx // CHUNK) <= tc[None, :, None]
        s = jnp.einsum('bqhd,bqkhd->bhqk', qb, ksel).astype(jnp.float32) * (d ** -0.5)
        s = jnp.where(valid[:, None], s, NEG)
        p = jax.nn.softmax(s, axis=-1).astype(v.dtype)
        return jnp.einsum('bhqk,bqkhd->bqhd', p, vsel)

    return sweep_query_blocks(block, bsz, seq)


def mla_attention(c_q, c_kv, k_rope, q_norm, kv_norm, w_uq, w_ukv, pos):
    bsz, seq, _ = c_q.shape
    q = (rmsnorm(c_q, q_norm) @ w_uq).reshape(bsz, seq, C_HEADS, C_NOPE + C_ROPE)
    qn, qr = q[..., :C_NOPE], rope(q[..., C_NOPE:], pos)
    kv = (rmsnorm(c_kv, kv_norm) @ w_ukv).reshape(bsz, seq, C_HEADS, C_NOPE + C_V)
    kn, v = kv[..., :C_NOPE], kv[..., C_NOPE:]
    kr = rope(k_rope[:, :, None, :], pos)[:, :, 0, :]
    kchunk = pos // CHUNK
    scale = (C_NOPE + C_ROPE) ** -0.5

    def block(q0):
        tc = (q0 + jnp.arange(Q_BLOCK)) // CHUNK
        qnb = lax.dynamic_slice_in_dim(qn, q0, Q_BLOCK, axis=1)
        qrb = lax.dynamic_slice_in_dim(qr, q0, Q_BLOCK, axis=1)
        s = (jnp.einsum('bqhd,bshd->bhqs', qnb, kn) + jnp.einsum('bqhr,bsr->bhqs', qrb, kr)).astype(jnp.float32) * scale
        s = jnp.where((kchunk[None, :] <= tc[:, None])[None, None], s, NEG)
        p = jax.nn.softmax(s, axis=-1).astype(v.dtype)
        return jnp.einsum('bhqs,bshd->bqhd', p, v)

    return sweep_query_blocks(block, bsz, seq)


def split_columns(h):
    parts, off = [], 0
    for size in IN_SIZES:
        parts.append(h[..., off:off + size])
        off += size
    return parts


def hybrid_layer(x, pos, ffn1_norm, ffn1_w_gate, ffn1_w_up, ffn1_w_down, mix_norm, w_in, a_rel_bias,
                 c_q_norm, c_kv_norm, c_w_uq, c_w_ukv, w_out, ffn2_norm, ffn2_w_gate, ffn2_w_up, ffn2_w_down):
    bsz, seq, _ = x.shape
    x = x + 0.5 * swiglu(rmsnorm(x, ffn1_norm), ffn1_w_gate, ffn1_w_up, ffn1_w_down)
    h = rmsnorm(x, mix_norm) @ w_in
    (a_q, a_k, a_v, b_q, b_k, b_v, i_q, i_k, i_w, c_q, c_kv, c_kr) = split_columns(h)
    hd4 = lambda t, nh: t.reshape(bsz, seq, nh, -1)
    o_a = chunk_relpos_attention(hd4(a_q, A_HEADS), hd4(a_k, A_HEADS), hd4(a_v, A_HEADS), a_rel_bias)
    o_b = dsa_attention(rope(hd4(b_q, B_HEADS), pos), rope(hd4(b_k, B_HEADS), pos), hd4(b_v, B_HEADS),
                        rope(hd4(i_q, IDX_HEADS), pos), rope(i_k[:, :, None, :], pos)[:, :, 0, :], i_w)
    o_c = mla_attention(c_q, c_kv, c_kr, c_q_norm, c_kv_norm, c_w_uq, c_w_ukv, pos)
    x = x + jnp.concatenate([o_a, o_b, o_c], axis=-1) @ w_out
    x = x + 0.5 * swiglu(rmsnorm(x, ffn2_norm), ffn2_w_gate, ffn2_w_up, ffn2_w_down)
    return x


def setup_inputs(seed: int = 0) -> dict:
    key = jax.random.key(seed)
    ks = jax.random.split(key, 20)
    f32 = jnp.float32

    def nrm(k, shape, scale):
        return jax.random.normal(k, shape, f32) * scale

    def gain(k, n):
        return 1.0 + 0.02 * jax.random.normal(k, (DEPTH, n), f32)

    return {
        "x": jax.random.normal(ks[0], (BATCH, SEQ, D_MODEL), f32),
        "ffn1_norm": gain(ks[1], D_MODEL),
        "ffn1_w_gate": nrm(ks[2], (DEPTH, D_MODEL, D_FF), D_MODEL ** -0.5),
        "ffn1_w_up": nrm(ks[3], (DEPTH, D_MODEL, D_FF), D_MODEL ** -0.5),
        "ffn1_w_down": nrm(ks[4], (DEPTH, D_FF, D_MODEL), D_FF ** -0.5),
        "mix_norm": gain(ks[5], D_MODEL),
        "w_in": nrm(ks[6], (DEPTH, D_MODEL, D_IN), D_MODEL ** -0.5),
        "a_rel_bias": nrm(ks[7], (DEPTH, A_HEADS, A_NUM_REL), 0.1),
        "c_q_norm": gain(ks[8], C_Q_RANK),
        "c_kv_norm": gain(ks[9], C_KV_RANK),
        "c_w_uq": nrm(ks[10], (DEPTH, C_Q_RANK, C_HEADS * (C_NOPE + C_ROPE)), C_Q_RANK ** -0.5),
        "c_w_ukv": nrm(ks[11], (DEPTH, C_KV_RANK, C_HEADS * (C_NOPE + C_V)), C_KV_RANK ** -0.5),
        "w_out": nrm(ks[12], (DEPTH, D_MIX, D_MODEL), D_MIX ** -0.5),
        "ffn2_norm": gain(ks[13], D_MODEL),
        "ffn2_w_gate": nrm(ks[14], (DEPTH, D_MODEL, D_FF), D_MODEL ** -0.5),
        "ffn2_w_up": nrm(ks[15], (DEPTH, D_MODEL, D_FF), D_MODEL ** -0.5),
        "ffn2_w_down": nrm(ks[16], (DEPTH, D_FF, D_MODEL), D_FF ** -0.5),
        "final_norm": 1.0 + 0.02 * jax.random.normal(ks[17], (D_MODEL,), f32),
    }


def reference(x, ffn1_norm, ffn1_w_gate, ffn1_w_up, ffn1_w_down, mix_norm, w_in, a_rel_bias,
              c_q_norm, c_kv_norm, c_w_uq, c_w_ukv, w_out, ffn2_norm, ffn2_w_gate, ffn2_w_up,
              ffn2_w_down, final_norm):
    pos = jnp.arange(x.shape[1], dtype=jnp.int32)
    for l in range(DEPTH):
        x = hybrid_layer(x, pos, ffn1_norm[l], ffn1_w_gate[l], ffn1_w_up[l], ffn1_w_down[l], mix_norm[l],
                         w_in[l], a_rel_bias[l], c_q_norm[l], c_kv_norm[l], c_w_uq[l], c_w_ukv[l], w_out[l],
                         ffn2_norm[l], ffn2_w_gate[l], ffn2_w_up[l], ffn2_w_down[l])
    return rmsnorm(x, final_norm)
```

```python
import functools

import numpy as np
import jax
import jax.numpy as jnp
from jax import lax
from jax.experimental import pallas as pl
from jax.experimental.pallas import tpu as pltpu

F32 = jnp.float32
BF16 = jnp.bfloat16

D_MODEL = 1024
CHUNK = 64
HEAD_DIM = 64
A_HEADS = 6
A_LEFT_CHUNKS = 8
A_REL_MAX = 128
B_HEADS = 5
IDX_HEADS = 8
IDX_DIM = 32
TOPK_MAX = 256
C_HEADS = 5
C_Q_RANK = 384
C_KV_RANK = 256
C_NOPE = 64
C_ROPE = 32
C_V = 64
D_FF = 2816
ROPE_THETA = 10000.0
EPS = 1e-6
NEG = -1e30

LANES = 128
PAIR_W = 2 * HEAD_DIM
A_W = A_HEADS * HEAD_DIM
B_W = B_HEADS * HEAD_DIM
B_WP = 384
C_QW = C_HEADS * LANES
IQ_W = IDX_HEADS * IDX_DIM
VMEM_LIMIT = 56 * 1024 * 1024

OFF_AQ, OFF_AK, OFF_AV, OFF_BV, OFF_CQ, OFF_CKV, OFF_IW = 0, 384, 768, 1152, 1536, 1920, 2176
OFF_ROPE = 2304
ROPE_W = 384 + 384 + 256 + 256 + 128
R_BQ, R_BK, R_IQ, R_IK, R_KR = 0, 384, 768, 1024, 1280
W_CAT = OFF_ROPE + 2 * ROPE_W


def _sortable_key_of(x):
    b = np.float32(x).view(np.int32)
    return int(b ^ ((b >> 31) & np.int32(0x7FFFFFFF)))


NEG_KEY = _sortable_key_of(NEG)
INT_MIN = -(2 ** 31)


def _dot(a, b):
    return jnp.dot(a, b, preferred_element_type=F32)


def _dot_t(a, b):
    return lax.dot_general(a, b, (((1,), (1,)), ((), ())), preferred_element_type=F32)


def _rms(x, g):
    ms = jnp.mean(x * x, axis=-1, keepdims=True)
    return x * lax.rsqrt(ms + EPS) * g


def _ffn_kernel(*refs, has_attn, has_final, tf):
    it = iter(refs)
    x_ref = next(it)
    if has_attn:
        oa_ref, ob_ref, oc_ref, wo_ref = next(it), next(it), next(it), next(it)
    g_ref, wg_ref, wu_ref, wd_ref = next(it), next(it), next(it), next(it)
    fg_ref = next(it) if has_final else None
    o_ref = next(it)
    acc_ref = next(it)

    x = x_ref[...]
    if has_attn:
        x = x + _dot(oa_ref[0], wo_ref[0:384, :])
        x = x + _dot(ob_ref[0], wo_ref[384:768, :])
        x = x + _dot(oc_ref[0], wo_ref[768:1152, :])
    xn = _rms(x, g_ref[...]).astype(BF16)
    for j in range(D_FF // tf):
        g = _dot(xn, wg_ref[:, j * tf:(j + 1) * tf])
        u = _dot(xn, wu_ref[:, j * tf:(j + 1) * tf])
        h = (g * jax.nn.sigmoid(g) * u).astype(BF16)
        c = _dot(h, wd_ref[j * tf:(j + 1) * tf, :])
        if j == 0:
            acc_ref[...] = c
        else:
            acc_ref[...] += c
    y = x + 0.5 * acc_ref[...]
    if has_final:
        y = _rms(y, fg_ref[...])
    o_ref[...] = y


def _ffn(x2, attn, wo, g, wg, wu, wd, final_g, *, tm=512, tf=256):
    n = x2.shape[0]
    has_attn = attn is not None
    has_final = final_g is not None
    const = lambda i: (0, 0)
    row = lambda i: (i, 0)
    in_specs = [pl.BlockSpec((tm, D_MODEL), row)]
    args = [x2]
    if has_attn:
        s = attn[0].shape[1]
        nt = s // tm
        amap = lambda i: (i // nt, i % nt, 0)
        for a in attn:
            in_specs.append(pl.BlockSpec((1, tm, 384), amap))
            args.append(a)
        in_specs.append(pl.BlockSpec((1152, D_MODEL), const))
        args.append(wo)
    in_specs += [pl.BlockSpec((1, D_MODEL), const),
                 pl.BlockSpec((D_MODEL, D_FF), const),
                 pl.BlockSpec((D_MODEL, D_FF), const),
                 pl.BlockSpec((D_FF, D_MODEL), const)]
    args += [g, wg, wu, wd]
    if has_final:
        in_specs.append(pl.BlockSpec((1, D_MODEL), const))
        args.append(final_g)
    return pl.pallas_call(
        functools.partial(_ffn_kernel, has_attn=has_attn, has_final=has_final, tf=tf),
        grid=(n // tm,),
        in_specs=in_specs,
        out_specs=pl.BlockSpec((tm, D_MODEL), row),
        out_shape=jax.ShapeDtypeStruct((n, D_MODEL), F32),
        scratch_shapes=[pltpu.VMEM((tm, D_MODEL), F32)],
        compiler_params=pltpu.CompilerParams(
            dimension_semantics=("arbitrary",), vmem_limit_bytes=VMEM_LIMIT),
        name="ffn",
    )(*args)


def _proj_kernel(x_ref, g_ref, w_ref, tab_ref, qn_ref, kvn_ref, wuq_ref, wukn_ref, wuv_ref,
                 aq_ref, ak_ref, av_ref, bq_ref, bk_ref, bv_ref, iq_ref, ik_ref, iw_ref,
                 cq_ref, ck_ref, cv_ref):
    xn = _rms(x_ref[...], g_ref[...]).astype(BF16)

    def mm(off, n):
        return _dot(xn, w_ref[:, off:off + n])

    aq_ref[0] = mm(OFF_AQ, 384).astype(BF16)
    ak_ref[0] = mm(OFF_AK, 384).astype(BF16)
    av_ref[0] = mm(OFF_AV, 384).astype(BF16)
    bv_ref[0] = mm(OFF_BV, 384).astype(BF16)
    iw_ref[0] = mm(OFF_IW, 128)[:, :IDX_HEADS]

    cos64, sin64 = tab_ref[:, 0:128], tab_ref[:, 128:256]
    cos32, sin32 = tab_ref[:, 256:384], tab_ref[:, 384:512]
    cosc, sinc = tab_ref[:, 512:640], tab_ref[:, 640:768]

    def rope_group(r_off, n, cos, sin):
        m = mm(OFF_ROPE + r_off, n)
        s = mm(OFF_ROPE + ROPE_W + r_off, n)
        return [m[:, c * 128:(c + 1) * 128] * cos + s[:, c * 128:(c + 1) * 128] * sin
                for c in range(n // 128)]

    for c, v in enumerate(rope_group(R_BQ, 384, cos64, sin64)):
        bq_ref[0, :, c * 128:(c + 1) * 128] = v.astype(BF16)
    for c, v in enumerate(rope_group(R_BK, 384, cos64, sin64)):
        bk_ref[0, :, c * 128:(c + 1) * 128] = v.astype(BF16)
    for c, v in enumerate(rope_group(R_IQ, 256, cos32, sin32)):
        iq_ref[0, :, c * 128:(c + 1) * 128] = v.astype(BF16)
    for c, v in enumerate(rope_group(R_IK, 256, cos32, sin32)):
        ik_ref[0, :, c * 128:(c + 1) * 128] = v.astype(BF16)
    krg = rope_group(R_KR, 128, cosc, sinc)[0]

    cqn = _rms(mm(OFF_CQ, C_Q_RANK), qn_ref[...]).astype(BF16)
    qm = _dot(cqn, wuq_ref[:, 0:C_QW])
    qs = _dot(cqn, wuq_ref[:, C_QW:2 * C_QW])
    ckn = _rms(mm(OFF_CKV, C_KV_RANK), kvn_ref[...]).astype(BF16)
    kn = _dot(ckn, wukn_ref[...])
    for h in range(C_HEADS):
        sl = slice(h * 128, (h + 1) * 128)
        cq_ref[0, :, sl] = (qm[:, sl] * cosc + qs[:, sl] * sinc).astype(BF16)
        ck_ref[0, :, sl] = (kn[:, sl] + krg).astype(BF16)
    cv_ref[0] = _dot(ckn, wuv_ref[...]).astype(BF16)


def _proj(x2, bsz, seq, g, wcat, tab, qn, kvn, wuq, wukn, wuv, *, tm=256):
    nt = seq // tm
    const = lambda i: (0, 0)
    omap = lambda i: (i // nt, i % nt, 0)
    widths = [384, 384, 384, 384, 384, 384, IQ_W, IQ_W, IDX_HEADS, C_QW, C_QW, 384]
    dtypes = [BF16] * 8 + [F32] + [BF16] * 3
    return pl.pallas_call(
        _proj_kernel,
        grid=(bsz * nt,),
        in_specs=[pl.BlockSpec((tm, D_MODEL), lambda i: (i, 0)),
                  pl.BlockSpec((1, D_MODEL), const),
                  pl.BlockSpec((D_MODEL, W_CAT), const),
                  pl.BlockSpec((tm, 768), lambda i: (i % nt, 0)),
                  pl.BlockSpec((1, C_Q_RANK), const),
                  pl.BlockSpec((1, C_KV_RANK), const),
                  pl.BlockSpec((C_Q_RANK, 2 * C_QW), const),
                  pl.BlockSpec((C_KV_RANK, C_QW), const),
                  pl.BlockSpec((C_KV_RANK, 384), const)],
        out_specs=[pl.BlockSpec((1, tm, w), omap) for w in widths],
        out_shape=[jax.ShapeDtypeStruct((bsz, seq, w), d) for w, d in zip(widths, dtypes)],
        compiler_params=pltpu.CompilerParams(
            dimension_semantics=("arbitrary",), vmem_limit_bytes=VMEM_LIMIT),
        name="proj",
    )(x2, g, wcat, tab, qn, kvn, wuq, wukn, wuv)


A_QB = 2 * CHUNK
A_WIN = (A_LEFT_CHUNKS + 2) * CHUNK
A_PREV = A_LEFT_CHUNKS * CHUNK


def _chunk_attn_kernel(q_ref, kp_ref, kc_ref, vp_ref, vc_ref, bias_ref, o_ref, kcat_ref, vcat_ref,
                       *, tq):
    i = pl.program_id(1)
    kcat_ref[0:A_PREV, :] = kp_ref[0]
    kcat_ref[A_PREV:A_PREV + tq, :] = kc_ref[0]
    vcat_ref[0:A_PREV, :] = vp_ref[0]
    vcat_ref[A_PREV:A_PREV + tq, :] = vc_ref[0]
    lane = lax.broadcasted_iota(jnp.int32, (A_QB, LANES), 1)
    col_chunk = lax.broadcasted_iota(jnp.int32, (2 * A_QB, A_WIN), 1) // CHUNK
    for qb in range(tq // A_QB):
        first_chunk = i * (tq // CHUNK) + 2 * qb - A_LEFT_CHUNKS
        valid = (col_chunk + first_chunk) >= 0
        for p in range(A_HEADS // 2):
            ls = slice(p * LANES, (p + 1) * LANES)
            q = q_ref[0, qb * A_QB:(qb + 1) * A_QB, ls]
            qs = jnp.concatenate([jnp.where(lane < HEAD_DIM, q, jnp.zeros_like(q)),
                                  jnp.where(lane >= HEAD_DIM, q, jnp.zeros_like(q))], axis=0)
            kw = kcat_ref[qb * A_QB:qb * A_QB + A_WIN, ls]
            vw = vcat_ref[qb * A_QB:qb * A_QB + A_WIN, ls]
            s = _dot_t(qs, kw) + bias_ref[p]
            s = jnp.where(valid, s, NEG)
            m = jnp.max(s, axis=1, keepdims=True)
            e = jnp.exp(s - m)
            l = jnp.sum(e, axis=1, keepdims=True)
            o = _dot(e.astype(BF16), vw) / l
            o_ref[0, qb * A_QB:(qb + 1) * A_QB, ls] = jnp.where(
                lane < HEAD_DIM, o[0:A_QB], o[A_QB:2 * A_QB]).astype(BF16)


def _chunk_attn(q, k, v, bias, *, tq=512):
    bsz, seq, _ = q.shape
    assert tq == A_PREV and seq % tq == 0
    cur = lambda b, i: (b, i, 0)
    prev = lambda b, i: (b, jnp.maximum(i - 1, 0), 0)
    blk = (1, tq, A_W)
    return pl.pallas_call(
        functools.partial(_chunk_attn_kernel, tq=tq),
        grid=(bsz, seq // tq),
        in_specs=[pl.BlockSpec(blk, cur), pl.BlockSpec(blk, prev), pl.BlockSpec(blk, cur),
                  pl.BlockSpec(blk, prev), pl.BlockSpec(blk, cur),
                  pl.BlockSpec((A_HEADS // 2, 2 * A_QB, A_WIN), lambda b, i: (0, 0, 0))],
        out_specs=pl.BlockSpec(blk, cur),
        out_shape=jax.ShapeDtypeStruct((bsz, seq, A_W), BF16),
        scratch_shapes=[pltpu.VMEM((A_PREV + tq, A_W), BF16), pltpu.VMEM((A_PREV + tq, A_W), BF16)],
        compiler_params=pltpu.CompilerParams(
            dimension_semantics=("arbitrary", "arbitrary"), vmem_limit_bytes=VMEM_LIMIT),
        name="chunk_attn",
    )(q, k, k, v, v, bias)


def _pair_mask(shape, e):
    lane = lax.broadcasted_iota(jnp.int32, shape, 1)
    return (lane < HEAD_DIM) if e == 0 else (lane >= HEAD_DIM)


def _mla_kernel(q_ref, k_ref, v_ref, o_ref, m_ref, l_ref, acc_ref, *, tq, tk):
    i = pl.program_id(1)
    q0 = i * tq
    nkb = (q0 + tq) // tk
    m_ref[...] = jnp.full(m_ref.shape, NEG, F32)
    l_ref[...] = jnp.zeros(l_ref.shape, F32)
    acc_ref[...] = jnp.zeros(acc_ref.shape, F32)
    row_chunk = (q0 + lax.broadcasted_iota(jnp.int32, (tq, tk), 0)) // CHUNK
    col_chunk0 = lax.broadcasted_iota(jnp.int32, (tq, tk), 1) // CHUNK

    def body(kb, carry):
        k0 = pl.multiple_of(kb * tk, tk)
        mask = (col_chunk0 + kb * (tk // CHUNK)) <= row_chunk
        for h in range(C_HEADS):
            hs = slice(h * LANES, (h + 1) * LANES)
            ps = slice((h // 2) * LANES, (h // 2 + 1) * LANES)
            s = _dot_t(q_ref[0, :, hs], k_ref[0, pl.ds(k0, tk), hs])
            s = jnp.where(mask, s, NEG)
            m_old = m_ref[h]
            m_new = jnp.maximum(m_old, jnp.max(s, axis=1, keepdims=True))
            e = jnp.exp(s - m_new)
            alpha = jnp.exp(m_old - m_new)
            l_ref[h] = alpha * l_ref[h] + jnp.sum(e, axis=1, keepdims=True)
            acc_ref[h] = alpha * acc_ref[h] + _dot(e.astype(BF16), v_ref[0, pl.ds(k0, tk), ps])
            m_ref[h] = m_new
        return carry

    lax.fori_loop(0, nkb, body, 0)
    for p in range(3):
        ls = slice(p * LANES, (p + 1) * LANES)
        o0 = acc_ref[2 * p] / l_ref[2 * p]
        if 2 * p + 1 < C_HEADS:
            o1 = acc_ref[2 * p + 1] / l_ref[2 * p + 1]
        else:
            o1 = jnp.zeros_like(o0)
        o_ref[0, :, ls] = jnp.where(_pair_mask(o0.shape, 0), o0, o1).astype(BF16)


def _resident(shape, index_map):
    return pl.BlockSpec(shape, index_map, pipeline_mode=pl.Buffered(1))


def _mla(q, k, v, *, tq=256, tk=256):
    bsz, seq, _ = q.shape
    assert seq % tq == 0 and tq % tk == 0
    return pl.pallas_call(
        functools.partial(_mla_kernel, tq=tq, tk=tk),
        grid=(bsz, seq // tq),
        in_specs=[pl.BlockSpec((1, tq, C_QW), lambda b, i: (b, i, 0)),
                  _resident((1, seq, C_QW), lambda b, i: (b, 0, 0)),
                  _resident((1, seq, 384), lambda b, i: (b, 0, 0))],
        out_specs=pl.BlockSpec((1, tq, 384), lambda b, i: (b, i, 0)),
        out_shape=jax.ShapeDtypeStruct((bsz, seq, 384), BF16),
        scratch_shapes=[pltpu.VMEM((C_HEADS, tq, 1), F32), pltpu.VMEM((C_HEADS, tq, 1), F32),
                        pltpu.VMEM((C_HEADS, tq, LANES), F32)],
        compiler_params=pltpu.CompilerParams(
            dimension_semantics=("arbitrary", "arbitrary"), vmem_limit_bytes=VMEM_LIMIT),
        name="mla",
    )(q, k, v)


def _dsa_kernel(iq_ref, iw_ref, ik_ref, q_ref, k_ref, v_ref, tri_ref, o_ref,
                key_ref, qm_ref, m_ref, l_ref, acc_ref, *, tq, tk, topk):
    i = pl.program_id(1)
    q0 = i * tq
    nkb = (q0 + tq) // tk
    row_chunk = (q0 + lax.broadcasted_iota(jnp.int32, (tq, tk), 0)) // CHUNK
    col_chunk0 = lax.broadcasted_iota(jnp.int32, (tq, tk), 1) // CHUNK

    iq = iq_ref[0]
    head_of_lane = lax.broadcasted_iota(jnp.int32, iq.shape, 1) // IDX_DIM
    for h in range(IDX_HEADS):
        qm_ref[h * tq:(h + 1) * tq, :] = jnp.where(head_of_lane == h, iq, jnp.zeros_like(iq))
    w = iw_ref[0]

    def score_body(kb, carry):
        k0 = pl.multiple_of(kb * tk, tk)
        d = _dot_t(qm_ref[...], ik_ref[0, pl.ds(k0, tk), :])
        sc = jnp.zeros((tq, tk), F32)
        for h in range(IDX_HEADS):
            sc = sc + jnp.maximum(d[h * tq:(h + 1) * tq], 0.0) * w[:, h:h + 1]
        adm = (col_chunk0 + kb * (tk // CHUNK)) <= row_chunk
        sc = jnp.where(adm, sc, NEG)
        bits = pltpu.bitcast(sc, jnp.int32)
        key_ref[kb] = bits ^ ((bits >> 31) & jnp.int32(0x7FFFFFFF))
        return carry

    lax.fori_loop(0, nkb, score_body, 0)

    def count(pred_fn):
        def body(kb, accv):
            blk = key_ref[kb]
            for c in range(tk // LANES):
                accv = accv + jnp.where(pred_fn(blk[:, c * LANES:(c + 1) * LANES]), 1, 0)
            return accv
        accv = lax.fori_loop(0, nkb, body, jnp.zeros((tq, LANES), jnp.int32))
        return jnp.sum(accv, axis=1, keepdims=True)

    def count_ge(cand):
        candb = jnp.broadcast_to(cand, (tq, LANES))
        return count(lambda blk: blk >= candb)

    zero = jnp.zeros((tq, 1), jnp.int32)
    t0 = jnp.where(count_ge(zero) >= topk, zero, jnp.full((tq, 1), INT_MIN, jnp.int32))

    def bit_body(j, t):
        cand = t + lax.shift_left(jnp.int32(1), jnp.int32(30) - j)
        return jnp.where(count_ge(cand) >= topk, cand, t)

    thr = lax.fori_loop(0, 31, bit_body, t0)
    thrb = jnp.broadcast_to(thr, (tq, LANES))
    need = (topk - count(lambda blk: blk > thrb)).astype(F32)

    m_ref[...] = jnp.full(m_ref.shape, NEG, F32)
    l_ref[...] = jnp.zeros(l_ref.shape, F32)
    acc_ref[...] = jnp.zeros(acc_ref.shape, F32)
    q = q_ref[0]
    qlane = lax.broadcasted_iota(jnp.int32, (tq, LANES), 1)

    def attn_body(kb, eq_before):
        k0 = pl.multiple_of(kb * tk, tk)
        keyb = key_ref[kb]
        eq = keyb == thr
        eqf = jnp.where(eq, 1.0, 0.0)
        rank = eq_before + _dot(eqf.astype(BF16), tri_ref[...])
        adm = (col_chunk0 + kb * (tk // CHUNK)) <= row_chunk
        tie_ok = jnp.where(rank < need, 1, 0)
        sel = jnp.where(adm, jnp.where(keyb > thr, 1, jnp.where(eq, tie_ok, 0)), 0) > 0
        for h in range(B_HEADS):
            ps = slice((h // 2) * LANES, (h // 2 + 1) * LANES)
            qp = q[:, ps]
            qh = jnp.where((qlane < HEAD_DIM) if h % 2 == 0 else (qlane >= HEAD_DIM),
                           qp, jnp.zeros_like(qp))
            s = _dot_t(qh, k_ref[0, pl.ds(k0, tk), ps])
            s = jnp.where(sel, s, NEG)
            m_old = m_ref[h]
            m_new = jnp.maximum(m_old, jnp.max(s, axis=1, keepdims=True))
            e = jnp.where(sel, jnp.exp(s - m_new), 0.0)
            alpha = jnp.exp(m_old - m_new)
            l_ref[h] = alpha * l_ref[h] + jnp.sum(e, axis=1, keepdims=True)
            acc_ref[h] = alpha * acc_ref[h] + _dot(e.astype(BF16), v_ref[0, pl.ds(k0, tk), ps])
            m_ref[h] = m_new
        return eq_before + jnp.sum(eqf, axis=1, keepdims=True)

    lax.fori_loop(0, nkb, attn_body, jnp.zeros((tq, 1), F32))
    for p in range(3):
        ls = slice(p * LANES, (p + 1) * LANES)
        o0 = acc_ref[2 * p] / l_ref[2 * p]
        if 2 * p + 1 < B_HEADS:
            o1 = acc_ref[2 * p + 1] / l_ref[2 * p + 1]
        else:
            o1 = jnp.zeros_like(o0)
        o_ref[0, :, ls] = jnp.where(qlane < HEAD_DIM, o0, o1).astype(BF16)


def _dsa(iq, iw, ik, q, k, v, *, tq=256, tk=256):
    bsz, seq, _ = q.shape
    assert seq % tq == 0 and tq % tk == 0
    topk = min(TOPK_MAX, seq // 4)
    tri = (np.arange(tk)[:, None] < np.arange(tk)[None, :]).astype(np.float32)
    tri = jnp.asarray(tri, BF16)
    qmap = lambda b, i: (b, i, 0)
    full = lambda b, i: (b, 0, 0)
    return pl.pallas_call(
        functools.partial(_dsa_kernel, tq=tq, tk=tk, topk=topk),
        grid=(bsz, seq // tq),
        in_specs=[pl.BlockSpec((1, tq, IQ_W), qmap),
                  pl.BlockSpec((1, tq, IDX_HEADS), qmap),
                  _resident((1, seq, IQ_W), full),
                  pl.BlockSpec((1, tq, B_WP), qmap),
                  _resident((1, seq, B_WP), full),
                  _resident((1, seq, B_WP), full),
                  pl.BlockSpec((tk, tk), lambda b, i: (0, 0))],
        out_specs=pl.BlockSpec((1, tq, B_WP), qmap),
        out_shape=jax.ShapeDtypeStruct((bsz, seq, B_WP), BF16),
        scratch_shapes=[pltpu.VMEM((seq // tk, tq, tk), jnp.int32),
                        pltpu.VMEM((IDX_HEADS * tq, IQ_W), BF16),
                        pltpu.VMEM((B_HEADS, tq, 1), F32), pltpu.VMEM((B_HEADS, tq, 1), F32),
                        pltpu.VMEM((B_HEADS, tq, LANES), F32)],
        compiler_params=pltpu.CompilerParams(
            dimension_semantics=("arbitrary", "arbitrary"), vmem_limit_bytes=VMEM_LIMIT),
        name="dsa",
    )(iq, iw, ik, q, k, v, tri)


def _rot_cols(w, d):
    k, n = w.shape
    w3 = w.reshape(k, n // d, d)
    return jnp.concatenate([-w3[..., d // 2:], w3[..., :d // 2]], axis=-1).reshape(k, n)


def _pad_cols(w, n):
    return jnp.pad(w, ((0, 0), (0, n - w.shape[1])))


def _prep_proj_weights(w_in, c_w_uq, c_w_ukv):
    offs = np.cumsum([0, A_W, A_W, A_W, B_W, B_W, B_W, IQ_W, IDX_DIM, IDX_HEADS,
                      C_Q_RANK, C_KV_RANK, C_ROPE])
    (aq, ak, av, bq, bk, bv, iq, ik, iw, cq, ckv, ckr) = [
        w_in[:, offs[j]:offs[j + 1]] for j in range(12)]
    qscale = HEAD_DIM ** -0.5
    zeros64 = jnp.zeros((D_MODEL, 64), F32)
    zeros32 = jnp.zeros((D_MODEL, 32), F32)
    kr_group = jnp.concatenate([zeros64, ckr, zeros32], axis=1)
    kr_group_rot = jnp.concatenate([zeros64, _rot_cols(ckr, C_ROPE), zeros32], axis=1)
    plain = [aq * qscale, ak, av, _pad_cols(bv, B_WP), cq, ckv,
             _pad_cols(iw * (IDX_HEADS * IDX_DIM) ** -0.5, 128)]
    rope_main = [_pad_cols(bq * qscale, B_WP), _pad_cols(bk, B_WP), iq, jnp.tile(ik, (1, IDX_HEADS)),
                 kr_group]
    rope_rot = [_pad_cols(_rot_cols(bq, HEAD_DIM) * qscale, B_WP), _pad_cols(_rot_cols(bk, HEAD_DIM), B_WP),
                _rot_cols(iq, IDX_DIM), jnp.tile(_rot_cols(ik, IDX_DIM), (1, IDX_HEADS)), kr_group_rot]
    wcat = jnp.concatenate(plain + rope_main + rope_rot, axis=1).astype(BF16)

    cscale = (C_NOPE + C_ROPE) ** -0.5
    uq = c_w_uq.reshape(C_Q_RANK, C_HEADS, C_NOPE + C_ROPE) * cscale
    zq = jnp.zeros((C_Q_RANK, C_HEADS, 32), F32)
    uq_main = jnp.concatenate([uq, zq], axis=-1).reshape(C_Q_RANK, C_QW)
    uq_rot_r = jnp.concatenate([-uq[..., C_NOPE + C_ROPE // 2:], uq[..., C_NOPE:C_NOPE + C_ROPE // 2]],
                               axis=-1)
    uq_rot = jnp.concatenate([jnp.zeros((C_Q_RANK, C_HEADS, C_NOPE), F32), uq_rot_r, zq],
                             axis=-1).reshape(C_Q_RANK, C_QW)
    wuq = jnp.concatenate([uq_main, uq_rot], axis=1).astype(BF16)
    ukv = c_w_ukv.reshape(C_KV_RANK, C_HEADS, C_NOPE + C_V)
    wukn = jnp.concatenate([ukv[..., :C_NOPE], jnp.zeros((C_KV_RANK, C_HEADS, 64), F32)],
                           axis=-1).reshape(C_KV_RANK, C_QW).astype(BF16)
    wuv = _pad_cols(ukv[..., C_NOPE:].reshape(C_KV_RANK, C_HEADS * C_V), 384).astype(BF16)
    return wcat, wuq, wukn, wuv


def _rope_tables(seq):
    pos = jnp.arange(seq, dtype=F32)[:, None]

    def cs(d):
        inv = ROPE_THETA ** (-jnp.arange(0, d, 2, dtype=F32) / d)
        ang = pos * inv[None, :]
        c, s = jnp.cos(ang), jnp.sin(ang)
        return (jnp.tile(jnp.concatenate([c, c], axis=1), (1, LANES // d)),
                jnp.tile(jnp.concatenate([s, s], axis=1), (1, LANES // d)))

    c64, s64 = cs(HEAD_DIM)
    c32, s32 = cs(C_ROPE)
    lane = jnp.arange(LANES)[None, :]
    roped = (lane >= C_NOPE) & (lane < C_NOPE + C_ROPE)
    cc = jnp.where(roped, c32, 1.0)
    sc = jnp.where(roped, s32, 0.0)
    return jnp.concatenate([c64, s64, c32, s32, cc, sc], axis=1)


def _chunk_bias(rel_bias):
    r = np.arange(A_QB)
    j = np.arange(A_WIN)
    ki = j[None, :] - CHUNK * (r // CHUNK)[:, None]
    inwin = (ki >= 0) & (ki < (A_LEFT_CHUNKS + 1) * CHUNK)
    rel = A_LEFT_CHUNKS * CHUNK + (r % CHUNK)[:, None] - ki
    ridx = np.clip(rel, -(CHUNK - 1), A_REL_MAX) + (CHUNK - 1)
    b = jnp.where(jnp.asarray(inwin)[None], rel_bias.astype(F32)[:, ridx], NEG)
    return b.reshape(A_HEADS // 2, 2 * A_QB, A_WIN)


def _prep_wo(w_out):
    pad64 = jnp.zeros((64, D_MODEL), F32)
    return jnp.concatenate([w_out[:A_W], w_out[A_W:A_W + B_W], pad64,
                            w_out[A_W + B_W:], pad64], axis=0).astype(BF16)


def kernel(x, ffn1_norm, ffn1_w_gate, ffn1_w_up, ffn1_w_down, mix_norm, w_in, a_rel_bias, c_q_norm,
           c_kv_norm, c_w_uq, c_w_ukv, w_out, ffn2_norm, ffn2_w_gate, ffn2_w_up, ffn2_w_down,
           final_norm):
    bsz, seq, _ = x.shape
    depth = w_in.shape[0]
    tab = _rope_tables(seq)
    x2 = x.reshape(bsz * seq, D_MODEL)
    for l in range(depth):
        x2 = _ffn(x2, None, None, ffn1_norm[l][None], ffn1_w_gate[l].astype(BF16),
                  ffn1_w_up[l].astype(BF16), ffn1_w_down[l].astype(BF16), None)
        wcat, wuq, wukn, wuv = _prep_proj_weights(w_in[l], c_w_uq[l], c_w_ukv[l])
        (aq, ak, av, bq, bk, bv, iq, ik, iw, cq, ck, cv) = _proj(
            x2, bsz, seq, mix_norm[l][None], wcat, tab, c_q_norm[l][None], c_kv_norm[l][None],
            wuq, wukn, wuv)
        o_a = _chunk_attn(aq, ak, av, _chunk_bias(a_rel_bias[l]))
        o_b = _dsa(iq, iw, ik, bq, bk, bv)
        o_c = _mla(cq, ck, cv)
        x2 = _ffn(x2, (o_a, o_b, o_c), _prep_wo(w_out[l]), ffn2_norm[l][None],
                  ffn2_w_gate[l].astype(BF16), ffn2_w_up[l].astype(BF16),
                  ffn2_w_down[l].astype(BF16), final_norm[None] if l == depth - 1 else None)
    return x2.reshape(bsz, seq, D_MODEL)
```

```python
import functools
import math

import numpy as np
import jax
import jax.numpy as jnp
from jax import lax
from jax.experimental import pallas as pl
from jax.experimental.pallas import tpu as pltpu

F32 = jnp.float32
BF16 = jnp.bfloat16

D_MODEL = 1024
CHUNK = 64
HEAD_DIM = 64
A_HEADS = 6
A_LEFT_CHUNKS = 8
A_REL_MAX = 128
B_HEADS = 5
IDX_HEADS = 8
IDX_DIM = 32
TOPK_MAX = 256
C_HEADS = 5
C_Q_RANK = 384
C_KV_RANK = 256
C_NOPE = 64
C_ROPE = 32
C_V = 64
D_FF = 2816
ROPE_THETA = 10000.0
EPS = 1e-6
NEG = -1e30
LOG2E = math.log2(math.e)

LANES = 128
A_W = A_HEADS * HEAD_DIM
B_W = B_HEADS * HEAD_DIM
B_WP = 384
C_QW = C_HEADS * LANES
IQ_W = IDX_HEADS * IDX_DIM
VMEM_LIMIT = 56 * 1024 * 1024

OFF_AQ, OFF_AK, OFF_AV, OFF_BV, OFF_CQ, OFF_CKV, OFF_IW = 0, 384, 768, 1152, 1536, 1920, 2176
OFF_ROPE = 2304
ROPE_W = 384 + 384 + 256 + 256 + 128
R_BQ, R_BK, R_IQ, R_IK, R_KR = 0, 384, 768, 1024, 1280
W_CAT = OFF_ROPE + 2 * ROPE_W


def _sortable_key_of(x):
    b = np.float32(x).view(np.int32)
    return int(b ^ ((b >> 31) & np.int32(0x7FFFFFFF)))


NEG_KEY = _sortable_key_of(NEG)
INT_MIN = -(2 ** 31)


def _dot(a, b):
    return jnp.dot(a, b, preferred_element_type=F32)


def _dot_t(a, b):
    return lax.dot_general(a, b, (((1,), (1,)), ((), ())), preferred_element_type=F32)


def _rms(x, g):
    ms = jnp.mean(x * x, axis=-1, keepdims=True)
    return x * lax.rsqrt(ms + EPS) * g


def _ffn_kernel(*refs, has_attn, has_final, tf):
    it = iter(refs)
    x_ref = next(it)
    if has_attn:
        oa_ref, ob_ref, oc_ref, wo_ref = next(it), next(it), next(it), next(it)
    g_ref, wg_ref, wu_ref, wd_ref = next(it), next(it), next(it), next(it)
    fg_ref = next(it) if has_final else None
    o_ref = next(it)
    acc_ref = next(it)

    x = x_ref[...]
    if has_attn:
        x = x + _dot(oa_ref[0], wo_ref[0:384, :])
        x = x + _dot(ob_ref[0], wo_ref[384:768, :])
        x = x + _dot(oc_ref[0], wo_ref[768:1152, :])
    xn = _rms(x, g_ref[...]).astype(BF16)
    for j in range(D_FF // tf):
        g = _dot(xn, wg_ref[:, j * tf:(j + 1) * tf])
        u = _dot(xn, wu_ref[:, j * tf:(j + 1) * tf])
        h = (g * jax.nn.sigmoid(g) * u).astype(BF16)
        c = _dot(h, wd_ref[j * tf:(j + 1) * tf, :])
        if j == 0:
            acc_ref[...] = c
        else:
            acc_ref[...] += c
    y = x + 0.5 * acc_ref[...]
    if has_final:
        y = _rms(y, fg_ref[...])
    o_ref[...] = y


def _ffn(x2, attn, wo, g, wg, wu, wd, final_g, *, tm=512, tf=256):
    n = x2.shape[0]
    has_attn = attn is not None
    has_final = final_g is not None
    const = lambda i: (0, 0)
    row = lambda i: (i, 0)
    in_specs = [pl.BlockSpec((tm, D_MODEL), row)]
    args = [x2]
    if has_attn:
        s = attn[0].shape[1]
        nt = s // tm
        amap = lambda i: (i // nt, i % nt, 0)
        for a in attn:
            in_specs.append(pl.BlockSpec((1, tm, 384), amap))
            args.append(a)
        in_specs.append(pl.BlockSpec((1152, D_MODEL), const))
        args.append(wo)
    in_specs += [pl.BlockSpec((1, D_MODEL), const),
                 pl.BlockSpec((D_MODEL, D_FF), const),
                 pl.BlockSpec((D_MODEL, D_FF), const),
                 pl.BlockSpec((D_FF, D_MODEL), const)]
    args += [g, wg, wu, wd]
    if has_final:
        in_specs.append(pl.BlockSpec((1, D_MODEL), const))
        args.append(final_g)
    return pl.pallas_call(
        functools.partial(_ffn_kernel, has_attn=has_attn, has_final=has_final, tf=tf),
        grid=(n // tm,),
        in_specs=in_specs,
        out_specs=pl.BlockSpec((tm, D_MODEL), row),
        out_shape=jax.ShapeDtypeStruct((n, D_MODEL), F32),
        scratch_shapes=[pltpu.VMEM((tm, D_MODEL), F32)],
        compiler_params=pltpu.CompilerParams(
            dimension_semantics=("arbitrary",), vmem_limit_bytes=VMEM_LIMIT),
        name="ffn",
    )(*args)


def _proj_kernel(x_ref, g_ref, w_ref, tab_ref, qn_ref, kvn_ref, wuq_ref, wukn_ref, wuv_ref,
                 aq_ref, ak_ref, av_ref, bq_ref, bk_ref, bv_ref, iq_ref, ik_ref, iw_ref,
                 cq_ref, ck_ref, cv_ref):
    xn = _rms(x_ref[...], g_ref[...]).astype(BF16)

    def mm(off, n):
        return _dot(xn, w_ref[:, off:off + n])

    aq_ref[0] = mm(OFF_AQ, 384).astype(BF16)
    ak_ref[0] = mm(OFF_AK, 384).astype(BF16)
    av_ref[0] = mm(OFF_AV, 384).astype(BF16)
    bv_ref[0] = mm(OFF_BV, 384).astype(BF16)
    iw_ref[0] = mm(OFF_IW, 128)[:, :IDX_HEADS]

    cos64, sin64 = tab_ref[:, 0:128], tab_ref[:, 128:256]
    cos32, sin32 = tab_ref[:, 256:384], tab_ref[:, 384:512]
    cosc, sinc = tab_ref[:, 512:640], tab_ref[:, 640:768]

    def rope_group(r_off, n, cos, sin):
        m = mm(OFF_ROPE + r_off, n)
        s = mm(OFF_ROPE + ROPE_W + r_off, n)
        return [m[:, c * 128:(c + 1) * 128] * cos + s[:, c * 128:(c + 1) * 128] * sin
                for c in range(n // 128)]

    for c, v in enumerate(rope_group(R_BQ, 384, cos64, sin64)):
        bq_ref[0, :, c * 128:(c + 1) * 128] = v.astype(BF16)
    for c, v in enumerate(rope_group(R_BK, 384, cos64, sin64)):
        bk_ref[0, :, c * 128:(c + 1) * 128] = v.astype(BF16)
    for c, v in enumerate(rope_group(R_IQ, 256, cos32, sin32)):
        iq_ref[0, :, c * 128:(c + 1) * 128] = v.astype(BF16)
    for c, v in enumerate(rope_group(R_IK, 256, cos32, sin32)):
        ik_ref[0, :, c * 128:(c + 1) * 128] = v.astype(BF16)
    krg = rope_group(R_KR, 128, cosc, sinc)[0]

    cqn = _rms(mm(OFF_CQ, C_Q_RANK), qn_ref[...]).astype(BF16)
    qm = _dot(cqn, wuq_ref[:, 0:C_QW])
    qs = _dot(cqn, wuq_ref[:, C_QW:2 * C_QW])
    ckn = _rms(mm(OFF_CKV, C_KV_RANK), kvn_ref[...]).astype(BF16)
    kn = _dot(ckn, wukn_ref[...])
    for h in range(C_HEADS):
        sl = slice(h * 128, (h + 1) * 128)
        cq_ref[0, :, sl] = (qm[:, sl] * cosc + qs[:, sl] * sinc).astype(BF16)
        ck_ref[0, :, sl] = (kn[:, sl] + krg).astype(BF16)
    cv_ref[0] = _dot(ckn, wuv_ref[...]).astype(BF16)


def _proj(x2, bsz, seq, g, wcat, tab, qn, kvn, wuq, wukn, wuv, *, tm=256):
    nt = seq // tm
    const = lambda i: (0, 0)
    omap = lambda i: (i // nt, i % nt, 0)
    widths = [384, 384, 384, 384, 384, 384, IQ_W, IQ_W, IDX_HEADS, C_QW, C_QW, 384]
    dtypes = [BF16] * 8 + [F32] + [BF16] * 3
    return pl.pallas_call(
        _proj_kernel,
        grid=(bsz * nt,),
        in_specs=[pl.BlockSpec((tm, D_MODEL), lambda i: (i, 0)),
                  pl.BlockSpec((1, D_MODEL), const),
                  pl.BlockSpec((D_MODEL, W_CAT), const),
                  pl.BlockSpec((tm, 768), lambda i: (i % nt, 0)),
                  pl.BlockSpec((1, C_Q_RANK), const),
                  pl.BlockSpec((1, C_KV_RANK), const),
                  pl.BlockSpec((C_Q_RANK, 2 * C_QW), const),
                  pl.BlockSpec((C_KV_RANK, C_QW), const),
                  pl.BlockSpec((C_KV_RANK, 384), const)],
        out_specs=[pl.BlockSpec((1, tm, w), omap) for w in widths],
        out_shape=[jax.ShapeDtypeStruct((bsz, seq, w), d) for w, d in zip(widths, dtypes)],
        compiler_params=pltpu.CompilerParams(
            dimension_semantics=("arbitrary",), vmem_limit_bytes=VMEM_LIMIT),
        name="proj",
    )(x2, g, wcat, tab, qn, kvn, wuq, wukn, wuv)


A_QB = 2 * CHUNK
A_WIN = (A_LEFT_CHUNKS + 2) * CHUNK
A_PREV = A_LEFT_CHUNKS * CHUNK


def _chunk_attn_kernel(q_ref, kp_ref, kc_ref, vp_ref, vc_ref, bias_ref, o_ref, kcat_ref, vcat_ref,
                       *, tq):
    i = pl.program_id(1)
    kcat_ref[0:A_PREV, :] = kp_ref[0]
    kcat_ref[A_PREV:A_PREV + tq, :] = kc_ref[0]
    vcat_ref[0:A_PREV, :] = vp_ref[0]
    vcat_ref[A_PREV:A_PREV + tq, :] = vc_ref[0]
    lane = lax.broadcasted_iota(jnp.int32, (A_QB, LANES), 1)
    col_chunk = lax.broadcasted_iota(jnp.int32, (2 * A_QB, A_WIN), 1) // CHUNK
    for qb in range(tq // A_QB):
        first_chunk = i * (tq // CHUNK) + 2 * qb - A_LEFT_CHUNKS
        valid = (col_chunk + first_chunk) >= 0
        for p in range(A_HEADS // 2):
            ls = slice(p * LANES, (p + 1) * LANES)
            q = q_ref[0, qb * A_QB:(qb + 1) * A_QB, ls]
            qs = jnp.concatenate([jnp.where(lane < HEAD_DIM, q, jnp.zeros_like(q)),
                                  jnp.where(lane >= HEAD_DIM, q, jnp.zeros_like(q))], axis=0)
            kw = kcat_ref[qb * A_QB:qb * A_QB + A_WIN, ls]
            vw = vcat_ref[qb * A_QB:qb * A_QB + A_WIN, ls]
            s = _dot_t(qs, kw) + bias_ref[p]
            s = jnp.where(valid, s, NEG)
            m = jnp.max(s, axis=1, keepdims=True)
            e = jnp.exp(s - m)
            l = jnp.sum(e, axis=1, keepdims=True)
            o = _dot(e.astype(BF16), vw) / l
            o_ref[0, qb * A_QB:(qb + 1) * A_QB, ls] = jnp.where(
                lane < HEAD_DIM, o[0:A_QB], o[A_QB:2 * A_QB]).astype(BF16)


def _chunk_attn(q, k, v, bias, *, tq=512):
    bsz, seq, _ = q.shape
    assert tq == A_PREV and seq % tq == 0
    cur = lambda b, i: (b, i, 0)
    prev = lambda b, i: (b, jnp.maximum(i - 1, 0), 0)
    blk = (1, tq, A_W)
    return pl.pallas_call(
        functools.partial(_chunk_attn_kernel, tq=tq),
        grid=(bsz, seq // tq),
        in_specs=[pl.BlockSpec(blk, cur), pl.BlockSpec(blk, prev), pl.BlockSpec(blk, cur),
                  pl.BlockSpec(blk, prev), pl.BlockSpec(blk, cur),
                  pl.BlockSpec((A_HEADS // 2, 2 * A_QB, A_WIN), lambda b, i: (0, 0, 0))],
        out_specs=pl.BlockSpec(blk, cur),
        out_shape=jax.ShapeDtypeStruct((bsz, seq, A_W), BF16),
        scratch_shapes=[pltpu.VMEM((A_PREV + tq, A_W), BF16), pltpu.VMEM((A_PREV + tq, A_W), BF16)],
        compiler_params=pltpu.CompilerParams(
            dimension_semantics=("arbitrary", "arbitrary"), vmem_limit_bytes=VMEM_LIMIT),
        name="chunk_attn",
    )(q, k, k, v, v, bias)


def _fold(op, acc, s):
    for c in range(s.shape[1] // LANES):
        acc = op(acc, s[:, c * LANES:(c + 1) * LANES])
    return acc


def _row_all_lanes(op, x):
    return jnp.broadcast_to(op(x, axis=1, keepdims=True), x.shape)


def _pair_select(even_head, odd_head):
    lane = lax.broadcasted_iota(jnp.int32, even_head.shape, 1)
    return jnp.where(lane < HEAD_DIM, even_head, odd_head)


def _write_heads(o_ref, acc_ref, l_ref, n_heads):
    for p in range((n_heads + 1) // 2):
        o0 = acc_ref[2 * p] / _row_all_lanes(jnp.sum, l_ref[2 * p])
        if 2 * p + 1 < n_heads:
            o1 = acc_ref[2 * p + 1] / _row_all_lanes(jnp.sum, l_ref[2 * p + 1])
        else:
            o1 = jnp.zeros_like(o0)
        o_ref[0, :, p * LANES:(p + 1) * LANES] = _pair_select(o0, o1).astype(BF16)


def _resident(shape, index_map):
    return pl.BlockSpec(shape, index_map, pipeline_mode=pl.Buffered(1))


def _mla_kernel(q_ref, k_ref, v_ref, o_ref, m_ref, l_ref, acc_ref, *, tq):
    q0 = pl.multiple_of(pl.program_id(1) * tq, tq)
    nfull = q0 // tq
    diag_mask = (lax.broadcasted_iota(jnp.int32, (tq, tq), 1) // CHUNK
                 <= lax.broadcasted_iota(jnp.int32, (tq, tq), 0) // CHUNK)

    def scores(h, k0, masked):
        hs = slice(h * LANES, (h + 1) * LANES)
        s = _dot_t(q_ref[0, :, hs], k_ref[0, pl.ds(k0, tq), hs])
        return jnp.where(diag_mask, s, NEG) if masked else s

    def max_step(k0, masked):
        for h in range(C_HEADS):
            m_ref[h] = _fold(jnp.maximum, m_ref[h], scores(h, k0, masked))

    def acc_step(k0, masked):
        for h in range(C_HEADS):
            ps = slice((h // 2) * LANES, (h // 2 + 1) * LANES)
            p = jnp.exp2(scores(h, k0, masked) - pltpu.repeat(m_ref[h], tq // LANES, axis=1))
            l_ref[h] = _fold(jnp.add, l_ref[h], p)
            acc_ref[h] += _dot(p.astype(BF16), v_ref[0, pl.ds(k0, tq), ps])

    def sweep(step):
        def body(kb, carry):
            step(pl.multiple_of(kb * tq, tq), False)
            return carry
        lax.fori_loop(0, nfull, body, 0)
        step(q0, True)

    m_ref[...] = jnp.full(m_ref.shape, NEG, F32)
    sweep(max_step)
    for h in range(C_HEADS):
        m_ref[h] = _row_all_lanes(jnp.max, m_ref[h])
    l_ref[...] = jnp.zeros(l_ref.shape, F32)
    acc_ref[...] = jnp.zeros(acc_ref.shape, F32)
    sweep(acc_step)
    _write_heads(o_ref, acc_ref, l_ref, C_HEADS)


def _mla(q, k, v, *, tq=512):
    bsz, seq, _ = q.shape
    assert seq % tq == 0
    stat = pltpu.VMEM((C_HEADS, tq, LANES), F32)
    return pl.pallas_call(
        functools.partial(_mla_kernel, tq=tq),
        grid=(bsz, seq // tq),
        in_specs=[pl.BlockSpec((1, tq, C_QW), lambda b, i: (b, i, 0)),
                  _resident((1, seq, C_QW), lambda b, i: (b, 0, 0)),
                  _resident((1, seq, 384), lambda b, i: (b, 0, 0))],
        out_specs=pl.BlockSpec((1, tq, 384), lambda b, i: (b, i, 0)),
        out_shape=jax.ShapeDtypeStruct((bsz, seq, 384), BF16),
        scratch_shapes=[stat, stat, stat],
        compiler_params=pltpu.CompilerParams(
            dimension_semantics=("arbitrary", "arbitrary"), vmem_limit_bytes=VMEM_LIMIT),
        name="mla",
    )(q, k, v)


def _dsa_kernel(iq_ref, iw_ref, ik_ref, q_ref, k_ref, v_ref, tri_ref, o_ref,
                key_ref, qm_ref, m_ref, l_ref, acc_ref, *, tq, tk, topk):
    q0 = pl.program_id(1) * tq
    nkb = (q0 + tq + tk - 1) // tk
    row_chunk = (q0 + lax.broadcasted_iota(jnp.int32, (tq, tk), 0)) // CHUNK
    col_chunk0 = lax.broadcasted_iota(jnp.int32, (tq, tk), 1) // CHUNK

    def admissible(kb):
        return (col_chunk0 + kb * (tk // CHUNK)) <= row_chunk

    iq = iq_ref[0]
    head_of_lane = lax.broadcasted_iota(jnp.int32, iq.shape, 1) // IDX_DIM
    for h in range(IDX_HEADS):
        qm_ref[h * tq:(h + 1) * tq, :] = jnp.where(head_of_lane == h, iq, jnp.zeros_like(iq))
    w = iw_ref[0]

    sub = 256

    def score_body(kb, carry):
        adm = admissible(kb)
        for c in range(tk // sub):
            k0 = pl.multiple_of(kb * tk + c * sub, sub)
            d = _dot_t(qm_ref[...], ik_ref[0, pl.ds(k0, sub), :])
            sc = jnp.zeros((tq, sub), F32)
            for h in range(IDX_HEADS):
                sc = sc + jnp.maximum(d[h * tq:(h + 1) * tq], 0.0) * w[:, h:h + 1]
            sc = jnp.where(adm[:, c * sub:(c + 1) * sub], sc, NEG)
            bits = pltpu.bitcast(sc, jnp.int32)
            key_ref[kb, :, c * sub:(c + 1) * sub] = bits ^ ((bits >> 31) & jnp.int32(0x7FFFFFFF))
        return carry

    lax.fori_loop(0, nkb, score_body, 0)

    def count(pred_fn):
        def body(kb, accv):
            return _fold(lambda a, blk: a + jnp.where(pred_fn(blk), 1, 0), accv, key_ref[kb])
        accv = lax.fori_loop(0, nkb, body, jnp.zeros((tq, LANES), jnp.int32))
        return jnp.sum(accv, axis=1, keepdims=True)

    def count_ge(cand):
        candb = jnp.broadcast_to(cand, (tq, LANES))
        return count(lambda blk: blk >= candb)

    zero = jnp.zeros((tq, 1), jnp.int32)
    c0 = count_ge(zero)
    t0 = jnp.where(c0 >= topk, zero, jnp.full((tq, 1), INT_MIN, jnp.int32))
    ct0 = jnp.where(c0 >= topk, c0, nkb * tk)

    def bit_body(j, carry):
        t, ct = carry
        cand = t + lax.shift_left(jnp.int32(1), jnp.int32(30) - j)
        cc = count_ge(cand)
        ok = cc >= topk
        return jnp.where(ok, cand, t), jnp.where(ok, cc, ct)

    thr, cnt_thr = lax.fori_loop(0, 31, bit_body, (t0, ct0))
    has_tie = jnp.max(cnt_thr) > topk

    q = q_ref[0]
    qlane = lax.broadcasted_iota(jnp.int32, (tq, LANES), 1)

    def scores(h, kb, bias):
        ps = slice((h // 2) * LANES, (h // 2 + 1) * LANES)
        qp = q[:, ps]
        qh = jnp.where((qlane < HEAD_DIM) if h % 2 == 0 else (qlane >= HEAD_DIM),
                       qp, jnp.zeros_like(qp))
        k0 = pl.multiple_of(kb * tk, tk)
        return _dot_t(qh, k_ref[0, pl.ds(k0, tk), ps]) + bias

    def max_step(kb, bias):
        key_ref[kb] = pltpu.bitcast(bias, jnp.int32)
        for h in range(B_HEADS):
            m_ref[h] = _fold(jnp.maximum, m_ref[h], scores(h, kb, bias))

    m_ref[...] = jnp.full(m_ref.shape, NEG, F32)

    def plain_select():
        def body(kb, carry):
            adm_bias = jnp.where(admissible(kb), 0.0, NEG)
            max_step(kb, jnp.where(key_ref[kb] >= thr, adm_bias, NEG))
            return carry
        lax.fori_loop(0, nkb, body, 0)

    def tie_select():
        thrb = jnp.broadcast_to(thr, (tq, LANES))
        need = (topk - count(lambda blk: blk > thrb)).astype(F32)

        def body(kb, eq_before):
            keyb = key_ref[kb]
            eq = keyb == thr
            eqf = jnp.where(eq, 1.0, 0.0)
            rank = eq_before + _dot(eqf.astype(BF16), tri_ref[...])
            take = jnp.where(keyb > thr, 1, jnp.where(eq, jnp.where(rank < need, 1, 0), 0))
            adm_bias = jnp.where(admissible(kb), 0.0, NEG)
            max_step(kb, jnp.where(take > 0, adm_bias, NEG))
            return eq_before + jnp.sum(eqf, axis=1, keepdims=True)
        lax.fori_loop(0, nkb, body, jnp.zeros((tq, 1), F32))

    lax.cond(has_tie, tie_select, plain_select)

    for h in range(B_HEADS):
        m_ref[h] = _row_all_lanes(jnp.max, m_ref[h])
    l_ref[...] = jnp.zeros(l_ref.shape, F32)
    acc_ref[...] = jnp.zeros(acc_ref.shape, F32)

    def acc_body(kb, carry):
        bias = pltpu.bitcast(key_ref[kb], F32)
        k0 = pl.multiple_of(kb * tk, tk)
        for h in range(B_HEADS):
            ps = slice((h // 2) * LANES, (h // 2 + 1) * LANES)
            p = jnp.exp2(scores(h, kb, bias) - pltpu.repeat(m_ref[h], tk // LANES, axis=1))
            l_ref[h] = _fold(jnp.add, l_ref[h], p)
            acc_ref[h] += _dot(p.astype(BF16), v_ref[0, pl.ds(k0, tk), ps])
        return carry

    lax.fori_loop(0, nkb, acc_body, 0)
    _write_heads(o_ref, acc_ref, l_ref, B_HEADS)


def _dsa(iq, iw, ik, q, k, v, *, tq=256, tk=512):
    bsz, seq, _ = q.shape
    assert seq % tq == 0 and seq % tk == 0
    topk = min(TOPK_MAX, seq // 4)
    tri = (np.arange(tk)[:, None] < np.arange(tk)[None, :]).astype(np.float32)
    tri = jnp.asarray(tri, BF16)
    qmap = lambda b, i: (b, i, 0)
    full = lambda b, i: (b, 0, 0)
    stat = pltpu.VMEM((B_HEADS, tq, LANES), F32)
    return pl.pallas_call(
        functools.partial(_dsa_kernel, tq=tq, tk=tk, topk=topk),
        grid=(bsz, seq // tq),
        in_specs=[pl.BlockSpec((1, tq, IQ_W), qmap),
                  pl.BlockSpec((1, tq, IDX_HEADS), qmap),
                  _resident((1, seq, IQ_W), full),
                  pl.BlockSpec((1, tq, B_WP), qmap),
                  _resident((1, seq, B_WP), full),
                  _resident((1, seq, B_WP), full),
                  pl.BlockSpec((tk, tk), lambda b, i: (0, 0))],
        out_specs=pl.BlockSpec((1, tq, B_WP), qmap),
        out_shape=jax.ShapeDtypeStruct((bsz, seq, B_WP), BF16),
        scratch_shapes=[pltpu.VMEM((seq // tk, tq, tk), jnp.int32),
                        pltpu.VMEM((IDX_HEADS * tq, IQ_W), BF16),
                        stat, stat, stat],
        compiler_params=pltpu.CompilerParams(
            dimension_semantics=("arbitrary", "arbitrary"), vmem_limit_bytes=VMEM_LIMIT),
        name="dsa",
    )(iq, iw, ik, q, k, v, tri)


def _rot_cols(w, d):
    k, n = w.shape
    w3 = w.reshape(k, n // d, d)
    return jnp.concatenate([-w3[..., d // 2:], w3[..., :d // 2]], axis=-1).reshape(k, n)


def _pad_cols(w, n):
    return jnp.pad(w, ((0, 0), (0, n - w.shape[1])))


def _prep_proj_weights(w_in, c_w_uq, c_w_ukv):
    offs = np.cumsum([0, A_W, A_W, A_W, B_W, B_W, B_W, IQ_W, IDX_DIM, IDX_HEADS,
                      C_Q_RANK, C_KV_RANK, C_ROPE])
    (aq, ak, av, bq, bk, bv, iq, ik, iw, cq, ckv, ckr) = [
        w_in[:, offs[j]:offs[j + 1]] for j in range(12)]
    ascale = HEAD_DIM ** -0.5
    bscale = HEAD_DIM ** -0.5 * LOG2E
    zeros64 = jnp.zeros((D_MODEL, 64), F32)
    zeros32 = jnp.zeros((D_MODEL, 32), F32)
    kr_group = jnp.concatenate([zeros64, ckr, zeros32], axis=1)
    kr_group_rot = jnp.concatenate([zeros64, _rot_cols(ckr, C_ROPE), zeros32], axis=1)
    plain = [aq * ascale, ak, av, _pad_cols(bv, B_WP), cq, ckv,
             _pad_cols(iw * (IDX_HEADS * IDX_DIM) ** -0.5, 128)]
    rope_main = [_pad_cols(bq * bscale, B_WP), _pad_cols(bk, B_WP), iq, jnp.tile(ik, (1, IDX_HEADS)),
                 kr_group]
    rope_rot = [_pad_cols(_rot_cols(bq, HEAD_DIM) * bscale, B_WP), _pad_cols(_rot_cols(bk, HEAD_DIM), B_WP),
                _rot_cols(iq, IDX_DIM), jnp.tile(_rot_cols(ik, IDX_DIM), (1, IDX_HEADS)), kr_group_rot]
    wcat = jnp.concatenate(plain + rope_main + rope_rot, axis=1).astype(BF16)

    cscale = (C_NOPE + C_ROPE) ** -0.5 * LOG2E
    uq = c_w_uq.reshape(C_Q_RANK, C_HEADS, C_NOPE + C_ROPE) * cscale
    zq = jnp.zeros((C_Q_RANK, C_HEADS, 32), F32)
    uq_main = jnp.concatenate([uq, zq], axis=-1).reshape(C_Q_RANK, C_QW)
    uq_rot_r = jnp.concatenate([-uq[..., C_NOPE + C_ROPE // 2:], uq[..., C_NOPE:C_NOPE + C_ROPE // 2]],
                               axis=-1)
    uq_rot = jnp.concatenate([jnp.zeros((C_Q_RANK, C_HEADS, C_NOPE), F32), uq_rot_r, zq],
                             axis=-1).reshape(C_Q_RANK, C_QW)
    wuq = jnp.concatenate([uq_main, uq_rot], axis=1).astype(BF16)
    ukv = c_w_ukv.reshape(C_KV_RANK, C_HEADS, C_NOPE + C_V)
    wukn = jnp.concatenate([ukv[..., :C_NOPE], jnp.zeros((C_KV_RANK, C_HEADS, 64), F32)],
                           axis=-1).reshape(C_KV_RANK, C_QW).astype(BF16)
    wuv = _pad_cols(ukv[..., C_NOPE:].reshape(C_KV_RANK, C_HEADS * C_V), 384).astype(BF16)
    return wcat, wuq, wukn, wuv


def _rope_tables(seq):
    pos = jnp.arange(seq, dtype=F32)[:, None]

    def cs(d):
        inv = ROPE_THETA ** (-jnp.arange(0, d, 2, dtype=F32) / d)
        ang = pos * inv[None, :]
        c, s = jnp.cos(ang), jnp.sin(ang)
        return (jnp.tile(jnp.concatenate([c, c], axis=1), (1, LANES // d)),
                jnp.tile(jnp.concatenate([s, s], axis=1), (1, LANES // d)))

    c64, s64 = cs(HEAD_DIM)
    c32, s32 = cs(C_ROPE)
    lane = jnp.arange(LANES)[None, :]
    roped = (lane >= C_NOPE) & (lane < C_NOPE + C_ROPE)
    cc = jnp.where(roped, c32, 1.0)
    sc = jnp.where(roped, s32, 0.0)
    return jnp.concatenate([c64, s64, c32, s32, cc, sc], axis=1)


def _chunk_bias(rel_bias):
    rb = rel_bias.astype(F32)
    n_rel = A_QB + A_WIN - 1
    below = jnp.broadcast_to(rb[:, :1], (A_HEADS, A_QB - 1 - (CHUNK - 1)))
    above = jnp.broadcast_to(rb[:, -1:], (A_HEADS, A_PREV + A_QB - 1 - A_REL_MAX))
    by_rel_desc = jnp.concatenate([below, rb, above], axis=1)[:, ::-1]
    assert by_rel_desc.shape[1] == n_rel
    rows = [by_rel_desc[:, A_QB - 1 - r:A_QB - 1 - r + A_WIN] for r in range(A_QB)]
    toeplitz = jnp.stack(rows, axis=1)
    r = np.arange(A_QB)
    ki = np.arange(A_WIN)[None, :] - CHUNK * (r // CHUNK)[:, None]
    inwin = (ki >= 0) & (ki < (A_LEFT_CHUNKS + 1) * CHUNK)
    b = jnp.where(jnp.asarray(inwin)[None], toeplitz, NEG)
    return b.reshape(A_HEADS // 2, 2 * A_QB, A_WIN)


def _prep_wo(w_out):
    pad64 = jnp.zeros((64, D_MODEL), F32)
    return jnp.concatenate([w_out[:A_W], w_out[A_W:A_W + B_W], pad64,
                            w_out[A_W + B_W:], pad64], axis=0).astype(BF16)


def kernel(x, ffn1_norm, ffn1_w_gate, ffn1_w_up, ffn1_w_down, mix_norm, w_in, a_rel_bias, c_q_norm,
           c_kv_norm, c_w_uq, c_w_ukv, w_out, ffn2_norm, ffn2_w_gate, ffn2_w_up, ffn2_w_down,
           final_norm):
    bsz, seq, _ = x.shape
    depth = w_in.shape[0]
    tab = _rope_tables(seq)
    x2 = x.reshape(bsz * seq, D_MODEL)
    for l in range(depth):
        x2 = _ffn(x2, None, None, ffn1_norm[l][None], ffn1_w_gate[l].astype(BF16),
                  ffn1_w_up[l].astype(BF16), ffn1_w_down[l].astype(BF16), None)
        wcat, wuq, wukn, wuv = _prep_proj_weights(w_in[l], c_w_uq[l], c_w_ukv[l])
        (aq, ak, av, bq, bk, bv, iq, ik, iw, cq, ck, cv) = _proj(
            x2, bsz, seq, mix_norm[l][None], wcat, tab, c_q_norm[l][None], c_kv_norm[l][None],
            wuq, wukn, wuv)
        o_a = _chunk_attn(aq, ak, av, _chunk_bias(a_rel_bias[l]))
        o_b = _dsa(iq, iw, ik, bq, bk, bv)
        o_c = _mla(cq, ck, cv)
        x2 = _ffn(x2, (o_a, o_b, o_c), _prep_wo(w_out[l]), ffn2_norm[l][None],
                  ffn2_w_gate[l].astype(BF16), ffn2_w_up[l].astype(BF16),
                  ffn2_w_down[l].astype(BF16), final_norm[None] if l == depth - 1 else None)
    return x2.reshape(bsz, seq, D_MODEL)
```

```python
import functools
import math

import numpy as np
import jax
import jax.numpy as jnp
from jax import lax
from jax.experimental import pallas as pl
from jax.experimental.pallas import tpu as pltpu

F32 = jnp.float32
BF16 = jnp.bfloat16

D_MODEL = 1024
CHUNK = 64
HEAD_DIM = 64
A_HEADS = 6
A_LEFT_CHUNKS = 8
A_REL_MAX = 128
B_HEADS = 5
IDX_HEADS = 8
IDX_DIM = 32
TOPK_MAX = 256
C_HEADS = 5
C_Q_RANK = 384
C_KV_RANK = 256
C_NOPE = 64
C_ROPE = 32
C_V = 64
D_FF = 2816
ROPE_THETA = 10000.0
EPS = 1e-6
NEG = -1e30
LOG2E = math.log2(math.e)

LANES = 128
COUNT_ROWS = 64
EXIT_CHECK_BIT = 10
A_W = A_HEADS * HEAD_DIM
B_W = B_HEADS * HEAD_DIM
B_WP = 384
C_QW = C_HEADS * LANES
IQ_W = IDX_HEADS * IDX_DIM
VMEM_LIMIT = 56 * 1024 * 1024

OFF_AQ, OFF_AK, OFF_AV, OFF_BV, OFF_CQ, OFF_CKV, OFF_IW = 0, 384, 768, 1152, 1536, 1920, 2176
OFF_ROPE = 2304
ROPE_W = 384 + 384 + 256 + 256 + 128
R_BQ, R_BK, R_IQ, R_IK, R_KR = 0, 384, 768, 1024, 1280
W_CAT = OFF_ROPE + 2 * ROPE_W


def _sortable_key_of(x):
    b = np.float32(x).view(np.int32)
    return int(b ^ ((b >> 31) & np.int32(0x7FFFFFFF)))


NEG_KEY = _sortable_key_of(NEG)
INT_MIN = -(2 ** 31)


def _dot(a, b):
    return jnp.dot(a, b, preferred_element_type=F32)


def _dot_t(a, b):
    return lax.dot_general(a, b, (((1,), (1,)), ((), ())), preferred_element_type=F32)


def _rms(x, g):
    ms = jnp.mean(x * x, axis=-1, keepdims=True)
    return x * lax.rsqrt(ms + EPS) * g


def _ffn_kernel(*refs, has_attn, has_final, tf):
    it = iter(refs)
    x_ref = next(it)
    if has_attn:
        oa_ref, ob_ref, oc_ref, wo_ref = next(it), next(it), next(it), next(it)
    g_ref, wg_ref, wu_ref, wd_ref = next(it), next(it), next(it), next(it)
    fg_ref = next(it) if has_final else None
    o_ref = next(it)
    acc_ref = next(it)

    x = x_ref[...]
    if has_attn:
        x = x + _dot(oa_ref[0], wo_ref[0:384, :])
        x = x + _dot(ob_ref[0], wo_ref[384:768, :])
        x = x + _dot(oc_ref[0], wo_ref[768:1152, :])
    xn = _rms(x, g_ref[...]).astype(BF16)
    for j in range(D_FF // tf):
        g = _dot(xn, wg_ref[:, j * tf:(j + 1) * tf])
        u = _dot(xn, wu_ref[:, j * tf:(j + 1) * tf])
        h = (g * jax.nn.sigmoid(g) * u).astype(BF16)
        c = _dot(h, wd_ref[j * tf:(j + 1) * tf, :])
        if j == 0:
            acc_ref[...] = c
        else:
            acc_ref[...] += c
    y = x + 0.5 * acc_ref[...]
    if has_final:
        y = _rms(y, fg_ref[...])
    o_ref[...] = y


def _ffn(x2, attn, wo, g, wg, wu, wd, final_g, *, tm=512, tf=256):
    n = x2.shape[0]
    has_attn = attn is not None
    has_final = final_g is not None
    const = lambda i: (0, 0)
    row = lambda i: (i, 0)
    in_specs = [pl.BlockSpec((tm, D_MODEL), row)]
    args = [x2]
    if has_attn:
        s = attn[0].shape[1]
        nt = s // tm
        amap = lambda i: (i // nt, i % nt, 0)
        for a in attn:
            in_specs.append(pl.BlockSpec((1, tm, 384), amap))
            args.append(a)
        in_specs.append(pl.BlockSpec((1152, D_MODEL), const))
        args.append(wo)
    in_specs += [pl.BlockSpec((1, D_MODEL), const),
                 pl.BlockSpec((D_MODEL, D_FF), const),
                 pl.BlockSpec((D_MODEL, D_FF), const),
                 pl.BlockSpec((D_FF, D_MODEL), const)]
    args += [g, wg, wu, wd]
    if has_final:
        in_specs.append(pl.BlockSpec((1, D_MODEL), const))
        args.append(final_g)
    return pl.pallas_call(
        functools.partial(_ffn_kernel, has_attn=has_attn, has_final=has_final, tf=tf),
        grid=(n // tm,),
        in_specs=in_specs,
        out_specs=pl.BlockSpec((tm, D_MODEL), row),
        out_shape=jax.ShapeDtypeStruct((n, D_MODEL), F32),
        scratch_shapes=[pltpu.VMEM((tm, D_MODEL), F32)],
        compiler_params=pltpu.CompilerParams(
            dimension_semantics=("arbitrary",), vmem_limit_bytes=VMEM_LIMIT),
        name="ffn",
    )(*args)


def _proj_kernel(x_ref, g_ref, w_ref, tab_ref, qn_ref, kvn_ref, wuq_ref, wukn_ref, wuv_ref,
                 aq_ref, ak_ref, av_ref, bq_ref, bk_ref, bv_ref, iq_ref, ik_ref, iw_ref,
                 cq_ref, ck_ref, cv_ref):
    xn = _rms(x_ref[...], g_ref[...]).astype(BF16)

    def mm(off, n):
        return _dot(xn, w_ref[:, off:off + n])

    aq_ref[0] = mm(OFF_AQ, 384).astype(BF16)
    ak_ref[0] = mm(OFF_AK, 384).astype(BF16)
    av_ref[0] = mm(OFF_AV, 384).astype(BF16)
    bv_ref[0] = mm(OFF_BV, 384).astype(BF16)
    iw_ref[0] = mm(OFF_IW, 128)[:, :IDX_HEADS]

    cos64, sin64 = tab_ref[:, 0:128], tab_ref[:, 128:256]
    cos32, sin32 = tab_ref[:, 256:384], tab_ref[:, 384:512]
    cosc, sinc = tab_ref[:, 512:640], tab_ref[:, 640:768]

    def rope_group(r_off, n, cos, sin):
        m = mm(OFF_ROPE + r_off, n)
        s = mm(OFF_ROPE + ROPE_W + r_off, n)
        return [m[:, c * 128:(c + 1) * 128] * cos + s[:, c * 128:(c + 1) * 128] * sin
                for c in range(n // 128)]

    for c, v in enumerate(rope_group(R_BQ, 384, cos64, sin64)):
        bq_ref[0, :, c * 128:(c + 1) * 128] = v.astype(BF16)
    for c, v in enumerate(rope_group(R_BK, 384, cos64, sin64)):
        bk_ref[0, :, c * 128:(c + 1) * 128] = v.astype(BF16)
    for c, v in enumerate(rope_group(R_IQ, 256, cos32, sin32)):
        iq_ref[0, :, c * 128:(c + 1) * 128] = v.astype(BF16)
    for c, v in enumerate(rope_group(R_IK, 256, cos32, sin32)):
        ik_ref[0, :, c * 128:(c + 1) * 128] = v.astype(BF16)
    krg = rope_group(R_KR, 128, cosc, sinc)[0]

    cqn = _rms(mm(OFF_CQ, C_Q_RANK), qn_ref[...]).astype(BF16)
    qm = _dot(cqn, wuq_ref[:, 0:C_QW])
    qs = _dot(cqn, wuq_ref[:, C_QW:2 * C_QW])
    ckn = _rms(mm(OFF_CKV, C_KV_RANK), kvn_ref[...]).astype(BF16)
    kn = _dot(ckn, wukn_ref[...])
    for h in range(C_HEADS):
        sl = slice(h * 128, (h + 1) * 128)
        cq_ref[0, :, sl] = (qm[:, sl] * cosc + qs[:, sl] * sinc).astype(BF16)
        ck_ref[0, :, sl] = (kn[:, sl] + krg).astype(BF16)
    cv_ref[0] = _dot(ckn, wuv_ref[...]).astype(BF16)


def _proj(x2, bsz, seq, g, wcat, tab, qn, kvn, wuq, wukn, wuv, *, tm=256):
    nt = seq // tm
    const = lambda i: (0, 0)
    omap = lambda i: (i // nt, i % nt, 0)
    widths = [384, 384, 384, 384, 384, 384, IQ_W, IQ_W, IDX_HEADS, C_QW, C_QW, 384]
    dtypes = [BF16] * 8 + [F32] + [BF16] * 3
    return pl.pallas_call(
        _proj_kernel,
        grid=(bsz * nt,),
        in_specs=[pl.BlockSpec((tm, D_MODEL), lambda i: (i, 0)),
                  pl.BlockSpec((1, D_MODEL), const),
                  pl.BlockSpec((D_MODEL, W_CAT), const),
                  pl.BlockSpec((tm, 768), lambda i: (i % nt, 0)),
                  pl.BlockSpec((1, C_Q_RANK), const),
                  pl.BlockSpec((1, C_KV_RANK), const),
                  pl.BlockSpec((C_Q_RANK, 2 * C_QW), const),
                  pl.BlockSpec((C_KV_RANK, C_QW), const),
                  pl.BlockSpec((C_KV_RANK, 384), const)],
        out_specs=[pl.BlockSpec((1, tm, w), omap) for w in widths],
        out_shape=[jax.ShapeDtypeStruct((bsz, seq, w), d) for w, d in zip(widths, dtypes)],
        compiler_params=pltpu.CompilerParams(
            dimension_semantics=("arbitrary",), vmem_limit_bytes=VMEM_LIMIT),
        name="proj",
    )(x2, g, wcat, tab, qn, kvn, wuq, wukn, wuv)


A_QB = 2 * CHUNK
A_WIN = (A_LEFT_CHUNKS + 2) * CHUNK
A_PREV = A_LEFT_CHUNKS * CHUNK


def _chunk_attn_kernel(q_ref, kp_ref, kc_ref, vp_ref, vc_ref, bias_ref, o_ref, kcat_ref, vcat_ref,
                       *, tq):
    i = pl.program_id(1)
    kcat_ref[0:A_PREV, :] = kp_ref[0]
    kcat_ref[A_PREV:A_PREV + tq, :] = kc_ref[0]
    vcat_ref[0:A_PREV, :] = vp_ref[0]
    vcat_ref[A_PREV:A_PREV + tq, :] = vc_ref[0]
    lane = lax.broadcasted_iota(jnp.int32, (A_QB, LANES), 1)
    col_chunk = lax.broadcasted_iota(jnp.int32, (2 * A_QB, A_WIN), 1) // CHUNK
    for qb in range(tq // A_QB):
        first_chunk = i * (tq // CHUNK) + 2 * qb - A_LEFT_CHUNKS
        valid = (col_chunk + first_chunk) >= 0
        for p in range(A_HEADS // 2):
            ls = slice(p * LANES, (p + 1) * LANES)
            q = q_ref[0, qb * A_QB:(qb + 1) * A_QB, ls]
            qs = jnp.concatenate([jnp.where(lane < HEAD_DIM, q, jnp.zeros_like(q)),
                                  jnp.where(lane >= HEAD_DIM, q, jnp.zeros_like(q))], axis=0)
            kw = kcat_ref[qb * A_QB:qb * A_QB + A_WIN, ls]
            vw = vcat_ref[qb * A_QB:qb * A_QB + A_WIN, ls]
            s = _dot_t(qs, kw) + bias_ref[p]
            s = jnp.where(valid, s, NEG)
            m = jnp.max(s, axis=1, keepdims=True)
            e = jnp.exp(s - m)
            l = jnp.sum(e, axis=1, keepdims=True)
            o = _dot(e.astype(BF16), vw) / l
            o_ref[0, qb * A_QB:(qb + 1) * A_QB, ls] = jnp.where(
                lane < HEAD_DIM, o[0:A_QB], o[A_QB:2 * A_QB]).astype(BF16)


def _chunk_attn(q, k, v, bias, *, tq=512):
    bsz, seq, _ = q.shape
    assert tq == A_PREV and seq % tq == 0
    cur = lambda b, i: (b, i, 0)
    prev = lambda b, i: (b, jnp.maximum(i - 1, 0), 0)
    blk = (1, tq, A_W)
    return pl.pallas_call(
        functools.partial(_chunk_attn_kernel, tq=tq),
        grid=(bsz, seq // tq),
        in_specs=[pl.BlockSpec(blk, cur), pl.BlockSpec(blk, prev), pl.BlockSpec(blk, cur),
                  pl.BlockSpec(blk, prev), pl.BlockSpec(blk, cur),
                  pl.BlockSpec((A_HEADS // 2, 2 * A_QB, A_WIN), lambda b, i: (0, 0, 0))],
        out_specs=pl.BlockSpec(blk, cur),
        out_shape=jax.ShapeDtypeStruct((bsz, seq, A_W), BF16),
        scratch_shapes=[pltpu.VMEM((A_PREV + tq, A_W), BF16), pltpu.VMEM((A_PREV + tq, A_W), BF16)],
        compiler_params=pltpu.CompilerParams(
            dimension_semantics=("arbitrary", "arbitrary"), vmem_limit_bytes=VMEM_LIMIT),
        name="chunk_attn",
    )(q, k, k, v, v, bias)


def _fold(op, acc, s):
    for c in range(s.shape[1] // LANES):
        acc = op(acc, s[:, c * LANES:(c + 1) * LANES])
    return acc


def _row_all_lanes(op, x):
    return jnp.broadcast_to(op(x, axis=1, keepdims=True), x.shape)


def _pair_select(even_head, odd_head):
    lane = lax.broadcasted_iota(jnp.int32, even_head.shape, 1)
    return jnp.where(lane < HEAD_DIM, even_head, odd_head)


def _write_heads(o_ref, acc_ref, l_ref, n_heads):
    for p in range((n_heads + 1) // 2):
        o0 = acc_ref[2 * p] / _row_all_lanes(jnp.sum, l_ref[2 * p])
        if 2 * p + 1 < n_heads:
            o1 = acc_ref[2 * p + 1] / _row_all_lanes(jnp.sum, l_ref[2 * p + 1])
        else:
            o1 = jnp.zeros_like(o0)
        o_ref[0, :, p * LANES:(p + 1) * LANES] = _pair_select(o0, o1).astype(BF16)


def _resident(shape, index_map):
    return pl.BlockSpec(shape, index_map, pipeline_mode=pl.Buffered(1))


def _mla_kernel(q_ref, k_ref, v_ref, o_ref, m_ref, l_ref, acc_ref, *, tq):
    q0 = pl.multiple_of(pl.program_id(1) * tq, tq)
    nfull = q0 // tq
    diag_mask = (lax.broadcasted_iota(jnp.int32, (tq, tq), 1) // CHUNK
                 <= lax.broadcasted_iota(jnp.int32, (tq, tq), 0) // CHUNK)

    def scores(h, k0, masked):
        hs = slice(h * LANES, (h + 1) * LANES)
        s = _dot_t(q_ref[0, :, hs], k_ref[0, pl.ds(k0, tq), hs])
        return jnp.where(diag_mask, s, NEG) if masked else s

    def max_step(k0, masked):
        for h in range(C_HEADS):
            m_ref[h] = _fold(jnp.maximum, m_ref[h], scores(h, k0, masked))

    def acc_step(k0, masked):
        for h in range(C_HEADS):
            ps = slice((h // 2) * LANES, (h // 2 + 1) * LANES)
            p = jnp.exp2(scores(h, k0, masked) - jnp.tile(m_ref[h], (1, tq // LANES)))
            l_ref[h] = _fold(jnp.add, l_ref[h], p)
            acc_ref[h] += _dot(p.astype(BF16), v_ref[0, pl.ds(k0, tq), ps])

    def sweep(step):
        def body(kb, carry):
            step(pl.multiple_of(kb * tq, tq), False)
            return carry
        lax.fori_loop(0, nfull, body, 0)
        step(q0, True)

    m_ref[...] = jnp.full(m_ref.shape, NEG, F32)
    sweep(max_step)
    for h in range(C_HEADS):
        m_ref[h] = _row_all_lanes(jnp.max, m_ref[h])
    l_ref[...] = jnp.zeros(l_ref.shape, F32)
    acc_ref[...] = jnp.zeros(acc_ref.shape, F32)
    sweep(acc_step)
    _write_heads(o_ref, acc_ref, l_ref, C_HEADS)


def _mla(q, k, v, *, tq=512):
    bsz, seq, _ = q.shape
    assert seq % tq == 0
    stat = pltpu.VMEM((C_HEADS, tq, LANES), F32)
    return pl.pallas_call(
        functools.partial(_mla_kernel, tq=tq),
        grid=(bsz, seq // tq),
        in_specs=[pl.BlockSpec((1, tq, C_QW), lambda b, i: (b, i, 0)),
                  _resident((1, seq, C_QW), lambda b, i: (b, 0, 0)),
                  _resident((1, seq, 384), lambda b, i: (b, 0, 0))],
        out_specs=pl.BlockSpec((1, tq, 384), lambda b, i: (b, i, 0)),
        out_shape=jax.ShapeDtypeStruct((bsz, seq, 384), BF16),
        scratch_shapes=[stat, stat, stat],
        compiler_params=pltpu.CompilerParams(
            dimension_semantics=("arbitrary", "arbitrary"), vmem_limit_bytes=VMEM_LIMIT),
        name="mla",
    )(q, k, v)


def _dsa_kernel(iq_ref, iw_ref, ik_ref, q_ref, k_ref, v_ref, tri_ref, o_ref,
                key_ref, qm_ref, cand_ref, m_ref, l_ref, acc_ref, *, tq, tk, topk):
    q0 = pl.program_id(1) * tq
    nkb = (q0 + tq + tk - 1) // tk
    row_chunk = (q0 + lax.broadcasted_iota(jnp.int32, (tq, tk), 0)) // CHUNK
    col_chunk0 = lax.broadcasted_iota(jnp.int32, (tq, tk), 1) // CHUNK

    def admissible(kb):
        return (col_chunk0 + kb * (tk // CHUNK)) <= row_chunk

    iq = iq_ref[0]
    head_of_lane = lax.broadcasted_iota(jnp.int32, iq.shape, 1) // IDX_DIM
    for h in range(IDX_HEADS):
        qm_ref[h * tq:(h + 1) * tq, :] = jnp.where(head_of_lane == h, iq, jnp.zeros_like(iq))
    w = iw_ref[0]

    sub = 256

    def score_body(kb, carry):
        adm = admissible(kb)
        for c in range(tk // sub):
            k0 = pl.multiple_of(kb * tk + c * sub, sub)
            d = _dot_t(qm_ref[...], ik_ref[0, pl.ds(k0, sub), :])
            sc = jnp.zeros((tq, sub), F32)
            for h in range(IDX_HEADS):
                sc = sc + jnp.maximum(d[h * tq:(h + 1) * tq], 0.0) * w[:, h:h + 1]
            sc = jnp.where(adm[:, c * sub:(c + 1) * sub], sc, NEG)
            bits = pltpu.bitcast(sc, jnp.int32)
            key_ref[kb, :, c * sub:(c + 1) * sub] = bits ^ ((bits >> 31) & jnp.int32(0x7FFFFFFF))
        return carry

    lax.fori_loop(0, nkb, score_body, 0)

    def count(pred_fn, cand):
        cand_ref[...] = cand

        def body(kb, accv):
            parts = []
            for r0 in range(0, tq, COUNT_ROWS):
                rows = slice(r0, r0 + COUNT_ROWS)
                c = cand_ref[rows, :]
                parts.append(_fold(lambda a, x: jnp.where(pred_fn(x, c), a + 1.0, a),
                                   accv[rows], key_ref[kb, rows, :]))
            return jnp.concatenate(parts, axis=0)
        accv = lax.fori_loop(0, nkb, body, jnp.zeros((tq, LANES), F32))
        return _row_all_lanes(jnp.sum, accv)

    def count_ge(cand):
        return count(lambda x, c: x >= c, cand)

    kf = float(topk)
    zero = jnp.zeros((tq, LANES), jnp.int32)
    c_nonneg = count_ge(zero)
    c_pos = count_ge(zero + 1)
    lo0 = jnp.where(c_nonneg >= kf, 0, INT_MIN)
    clo0 = jnp.where(c_nonneg >= kf, c_nonneg, (nkb * tk).astype(F32))
    settled = jnp.where(c_nonneg >= kf, jnp.where(c_pos < kf, 1, 0), 0)

    def sweep(b, lo, clo):
        cand = lo + lax.shift_left(jnp.int32(1), b)
        cc = count_ge(cand)
        ok = cc >= kf
        return jnp.where(ok, cand, lo), jnp.where(ok, cc, clo)

    lo1, clo1 = lax.fori_loop(0, 30 - EXIT_CHECK_BIT, lambda j, c: sweep(30 - j, *c), (lo0, clo0))

    def search_cond(carry):
        b, open_rows, _, _ = carry
        return jnp.logical_and(b >= 0, open_rows > 0)

    def search_body(carry):
        b, _, lo, clo = carry
        lo, clo = sweep(b, lo, clo)
        open_rows = jnp.max(jnp.where(clo == kf, 0, jnp.where(settled > 0, 0, 1)))
        return b - 1, open_rows, lo, clo

    _, _, thr, cnt_thr = lax.while_loop(search_cond, search_body,
                                        (jnp.int32(EXIT_CHECK_BIT), jnp.int32(1), lo1, clo1))
    has_tie = jnp.max(cnt_thr) > kf
    thr_t = jnp.tile(thr, (1, tk // LANES))

    q = q_ref[0]
    qlane = lax.broadcasted_iota(jnp.int32, (tq, LANES), 1)

    def scores(h, kb, bias):
        ps = slice((h // 2) * LANES, (h // 2 + 1) * LANES)
        qp = q[:, ps]
        qh = jnp.where((qlane < HEAD_DIM) if h % 2 == 0 else (qlane >= HEAD_DIM),
                       qp, jnp.zeros_like(qp))
        k0 = pl.multiple_of(kb * tk, tk)
        return _dot_t(qh, k_ref[0, pl.ds(k0, tk), ps]) + bias

    def max_step(kb, bias):
        key_ref[kb] = pltpu.bitcast(bias, jnp.int32)
        for h in range(B_HEADS):
            m_ref[h] = _fold(jnp.maximum, m_ref[h], scores(h, kb, bias))

    m_ref[...] = jnp.full(m_ref.shape, NEG, F32)

    def plain_select():
        def body(kb, carry):
            adm_bias = jnp.where(admissible(kb), 0.0, NEG)
            max_step(kb, jnp.where(key_ref[kb] >= thr_t, adm_bias, NEG))
            return carry
        lax.fori_loop(0, nkb, body, 0)

    def tie_select():
        need = jnp.tile(kf - count(lambda x, c: x > c, thr), (1, tk // LANES))

        def body(kb, eq_before):
            keyb = key_ref[kb]
            eq = keyb == thr_t
            eqf = jnp.where(eq, 1.0, 0.0)
            rank = jnp.tile(eq_before, (1, tk // LANES)) + _dot(eqf.astype(BF16), tri_ref[...])
            take = jnp.where(keyb > thr_t, 1, jnp.where(eq, jnp.where(rank < need, 1, 0), 0))
            adm_bias = jnp.where(admissible(kb), 0.0, NEG)
            max_step(kb, jnp.where(take > 0, adm_bias, NEG))
            return eq_before + _row_all_lanes(jnp.sum, _fold(jnp.add, jnp.zeros((tq, LANES), F32), eqf))
        lax.fori_loop(0, nkb, body, jnp.zeros((tq, LANES), F32))

    lax.cond(has_tie, tie_select, plain_select)

    for h in range(B_HEADS):
        m_ref[h] = _row_all_lanes(jnp.max, m_ref[h])
    l_ref[...] = jnp.zeros(l_ref.shape, F32)
    acc_ref[...] = jnp.zeros(acc_ref.shape, F32)

    def acc_body(kb, carry):
        bias = pltpu.bitcast(key_ref[kb], F32)
        k0 = pl.multiple_of(kb * tk, tk)
        for h in range(B_HEADS):
            ps = slice((h // 2) * LANES, (h // 2 + 1) * LANES)
            p = jnp.exp2(scores(h, kb, bias) - jnp.tile(m_ref[h], (1, tk // LANES)))
            l_ref[h] = _fold(jnp.add, l_ref[h], p)
            acc_ref[h] += _dot(p.astype(BF16), v_ref[0, pl.ds(k0, tk), ps])
        return carry

    lax.fori_loop(0, nkb, acc_body, 0)
    _write_heads(o_ref, acc_ref, l_ref, B_HEADS)


def _dsa(iq, iw, ik, q, k, v, *, tq=512, tk=512):
    bsz, seq, _ = q.shape
    assert seq % tq == 0 and seq % tk == 0
    topk = min(TOPK_MAX, seq // 4)
    tri = (np.arange(tk)[:, None] < np.arange(tk)[None, :]).astype(np.float32)
    tri = jnp.asarray(tri, BF16)
    qmap = lambda b, i: (b, i, 0)
    full = lambda b, i: (b, 0, 0)
    stat = pltpu.VMEM((B_HEADS, tq, LANES), F32)
    return pl.pallas_call(
        functools.partial(_dsa_kernel, tq=tq, tk=tk, topk=topk),
        grid=(bsz, seq // tq),
        in_specs=[pl.BlockSpec((1, tq, IQ_W), qmap),
                  pl.BlockSpec((1, tq, IDX_HEADS), qmap),
                  _resident((1, seq, IQ_W), full),
                  pl.BlockSpec((1, tq, B_WP), qmap),
                  _resident((1, seq, B_WP), full),
                  _resident((1, seq, B_WP), full),
                  pl.BlockSpec((tk, tk), lambda b, i: (0, 0))],
        out_specs=pl.BlockSpec((1, tq, B_WP), qmap),
        out_shape=jax.ShapeDtypeStruct((bsz, seq, B_WP), BF16),
        scratch_shapes=[pltpu.VMEM((seq // tk, tq, tk), jnp.int32),
                        pltpu.VMEM((IDX_HEADS * tq, IQ_W), BF16),
                        pltpu.VMEM((tq, LANES), jnp.int32),
                        stat, stat, stat],
        compiler_params=pltpu.CompilerParams(
            dimension_semantics=("arbitrary", "arbitrary"), vmem_limit_bytes=VMEM_LIMIT),
        name="dsa",
    )(iq, iw, ik, q, k, v, tri)


def _rot_cols(w, d):
    k, n = w.shape
    w3 = w.reshape(k, n // d, d)
    return jnp.concatenate([-w3[..., d // 2:], w3[..., :d // 2]], axis=-1).reshape(k, n)


def _pad_cols(w, n):
    return jnp.pad(w, ((0, 0), (0, n - w.shape[1])))


def _prep_proj_weights(w_in, c_w_uq, c_w_ukv):
    offs = np.cumsum([0, A_W, A_W, A_W, B_W, B_W, B_W, IQ_W, IDX_DIM, IDX_HEADS,
                      C_Q_RANK, C_KV_RANK, C_ROPE])
    (aq, ak, av, bq, bk, bv, iq, ik, iw, cq, ckv, ckr) = [
        w_in[:, offs[j]:offs[j + 1]] for j in range(12)]
    ascale = HEAD_DIM ** -0.5
    bscale = HEAD_DIM ** -0.5 * LOG2E
    zeros64 = jnp.zeros((D_MODEL, 64), F32)
    zeros32 = jnp.zeros((D_MODEL, 32), F32)
    kr_group = jnp.concatenate([zeros64, ckr, zeros32], axis=1)
    kr_group_rot = jnp.concatenate([zeros64, _rot_cols(ckr, C_ROPE), zeros32], axis=1)
    plain = [aq * ascale, ak, av, _pad_cols(bv, B_WP), cq, ckv,
             _pad_cols(iw * (IDX_HEADS * IDX_DIM) ** -0.5, 128)]
    rope_main = [_pad_cols(bq * bscale, B_WP), _pad_cols(bk, B_WP), iq, jnp.tile(ik, (1, IDX_HEADS)),
                 kr_group]
    rope_rot = [_pad_cols(_rot_cols(bq, HEAD_DIM) * bscale, B_WP), _pad_cols(_rot_cols(bk, HEAD_DIM), B_WP),
                _rot_cols(iq, IDX_DIM), jnp.tile(_rot_cols(ik, IDX_DIM), (1, IDX_HEADS)), kr_group_rot]
    wcat = jnp.concatenate(plain + rope_main + rope_rot, axis=1).astype(BF16)

    cscale = (C_NOPE + C_ROPE) ** -0.5 * LOG2E
    uq = c_w_uq.reshape(C_Q_RANK, C_HEADS, C_NOPE + C_ROPE) * cscale
    zq = jnp.zeros((C_Q_RANK, C_HEADS, 32), F32)
    uq_main = jnp.concatenate([uq, zq], axis=-1).reshape(C_Q_RANK, C_QW)
    uq_rot_r = jnp.concatenate([-uq[..., C_NOPE + C_ROPE // 2:], uq[..., C_NOPE:C_NOPE + C_ROPE // 2]],
                               axis=-1)
    uq_rot = jnp.concatenate([jnp.zeros((C_Q_RANK, C_HEADS, C_NOPE), F32), uq_rot_r, zq],
                             axis=-1).reshape(C_Q_RANK, C_QW)
    wuq = jnp.concatenate([uq_main, uq_rot], axis=1).astype(BF16)
    ukv = c_w_ukv.reshape(C_KV_RANK, C_HEADS, C_NOPE + C_V)
    wukn = jnp.concatenate([ukv[..., :C_NOPE], jnp.zeros((C_KV_RANK, C_HEADS, 64), F32)],
                           axis=-1).reshape(C_KV_RANK, C_QW).astype(BF16)
    wuv = _pad_cols(ukv[..., C_NOPE:].reshape(C_KV_RANK, C_HEADS * C_V), 384).astype(BF16)
    return wcat, wuq, wukn, wuv


def _rope_tables(seq):
    pos = jnp.arange(seq, dtype=F32)[:, None]

    def cs(d):
        inv = ROPE_THETA ** (-jnp.arange(0, d, 2, dtype=F32) / d)
        ang = pos * inv[None, :]
        c, s = jnp.cos(ang), jnp.sin(ang)
        return (jnp.tile(jnp.concatenate([c, c], axis=1), (1, LANES // d)),
                jnp.tile(jnp.concatenate([s, s], axis=1), (1, LANES // d)))

    c64, s64 = cs(HEAD_DIM)
    c32, s32 = cs(C_ROPE)
    lane = jnp.arange(LANES)[None, :]
    roped = (lane >= C_NOPE) & (lane < C_NOPE + C_ROPE)
    cc = jnp.where(roped, c32, 1.0)
    sc = jnp.where(roped, s32, 0.0)
    return jnp.concatenate([c64, s64, c32, s32, cc, sc], axis=1)


def _chunk_bias(rel_bias):
    rb = rel_bias.astype(F32)
    n_rel = A_QB + A_WIN - 1
    below = jnp.broadcast_to(rb[:, :1], (A_HEADS, A_QB - 1 - (CHUNK - 1)))
    above = jnp.broadcast_to(rb[:, -1:], (A_HEADS, A_PREV + A_QB - 1 - A_REL_MAX))
    by_rel_desc = jnp.concatenate([below, rb, above], axis=1)[:, ::-1]
    assert by_rel_desc.shape[1] == n_rel
    rows = [by_rel_desc[:, A_QB - 1 - r:A_QB - 1 - r + A_WIN] for r in range(A_QB)]
    toeplitz = jnp.stack(rows, axis=1)
    r = np.arange(A_QB)
    ki = np.arange(A_WIN)[None, :] - CHUNK * (r // CHUNK)[:, None]
    inwin = (ki >= 0) & (ki < (A_LEFT_CHUNKS + 1) * CHUNK)
    b = jnp.where(jnp.asarray(inwin)[None], toeplitz, NEG)
    return b.reshape(A_HEADS // 2, 2 * A_QB, A_WIN)


def _prep_wo(w_out):
    pad64 = jnp.zeros((64, D_MODEL), F32)
    return jnp.concatenate([w_out[:A_W], w_out[A_W:A_W + B_W], pad64,
                            w_out[A_W + B_W:], pad64], axis=0).astype(BF16)


def kernel(x, ffn1_norm, ffn1_w_gate, ffn1_w_up, ffn1_w_down, mix_norm, w_in, a_rel_bias, c_q_norm,
           c_kv_norm, c_w_uq, c_w_ukv, w_out, ffn2_norm, ffn2_w_gate, ffn2_w_up, ffn2_w_down,
           final_norm):
    bsz, seq, _ = x.shape
    depth = w_in.shape[0]
    tab = _rope_tables(seq)
    x2 = x.reshape(bsz * seq, D_MODEL)
    for l in range(depth):
        x2 = _ffn(x2, None, None, ffn1_norm[l][None], ffn1_w_gate[l].astype(BF16),
                  ffn1_w_up[l].astype(BF16), ffn1_w_down[l].astype(BF16), None)
        wcat, wuq, wukn, wuv = _prep_proj_weights(w_in[l], c_w_uq[l], c_w_ukv[l])
        (aq, ak, av, bq, bk, bv, iq, ik, iw, cq, ck, cv) = _proj(
            x2, bsz, seq, mix_norm[l][None], wcat, tab, c_q_norm[l][None], c_kv_norm[l][None],
            wuq, wukn, wuv)
        o_a = _chunk_attn(aq, ak, av, _chunk_bias(a_rel_bias[l]))
        o_b = _dsa(iq, iw, ik, bq, bk, bv)
        o_c = _mla(cq, ck, cv)
        x2 = _ffn(x2, (o_a, o_b, o_c), _prep_wo(w_out[l]), ffn2_norm[l][None],
                  ffn2_w_gate[l].astype(BF16), ffn2_w_up[l].astype(BF16),
                  ffn2_w_down[l].astype(BF16), final_norm[None] if l == depth - 1 else None)
    return x2.reshape(bsz, seq, D_MODEL)
```

```python
import functools
import math

import numpy as np
import jax
import jax.numpy as jnp
from jax import lax
from jax.experimental import pallas as pl
from jax.experimental.pallas import tpu as pltpu

F32 = jnp.float32
BF16 = jnp.bfloat16

D_MODEL = 1024
CHUNK = 64
HEAD_DIM = 64
A_HEADS = 6
A_LEFT_CHUNKS = 8
A_REL_MAX = 128
B_HEADS = 5
IDX_HEADS = 8
IDX_DIM = 32
TOPK_MAX = 256
C_HEADS = 5
C_Q_RANK = 384
C_KV_RANK = 256
C_NOPE = 64
C_ROPE = 32
C_V = 64
D_FF = 2816
ROPE_THETA = 10000.0
EPS = 1e-6
NEG = -1e30
LOG2E = math.log2(math.e)

LANES = 128
COUNT_ROWS = 64
EXIT_CHECK_BIT = 10
A_W = A_HEADS * HEAD_DIM
B_W = B_HEADS * HEAD_DIM
B_WP = 384
C_QW = C_HEADS * LANES
IQ_W = IDX_HEADS * IDX_DIM
VMEM_LIMIT = 56 * 1024 * 1024

OFF_AQ, OFF_AK, OFF_AV, OFF_BV, OFF_CQ, OFF_CKV, OFF_IW = 0, 384, 768, 1152, 1536, 1920, 2176
OFF_ROPE = 2304
ROPE_W = 384 + 384 + 256 + 256 + 128
R_BQ, R_BK, R_IQ, R_IK, R_KR = 0, 384, 768, 1024, 1280
W_CAT = OFF_ROPE + 2 * ROPE_W


def _sortable_key_of(x):
    b = np.float32(x).view(np.int32)
    return int(b ^ ((b >> 31) & np.int32(0x7FFFFFFF)))


NEG_KEY = _sortable_key_of(NEG)
INT_MIN = -(2 ** 31)


def _dot(a, b):
    return jnp.dot(a, b, preferred_element_type=F32)


def _dot_t(a, b):
    return lax.dot_general(a, b, (((1,), (1,)), ((), ())), preferred_element_type=F32)


def _rms(x, g):
    ms = jnp.mean(x * x, axis=-1, keepdims=True)
    return x * lax.rsqrt(ms + EPS) * g


def _layer(l, *tail):
    return pl.BlockSpec((None,) + tail, lambda *_: (l,) + (0,) * len(tail))


def _ffn_kernel(*refs, has_attn, has_final, tf):
    it = iter(refs)
    x_ref = next(it)
    if has_attn:
        oa_ref, ob_ref, oc_ref, wo_ref = next(it), next(it), next(it), next(it)
    g_ref, wg_ref, wu_ref, wd_ref = next(it), next(it), next(it), next(it)
    fg_ref = next(it) if has_final else None
    o_ref = next(it)
    acc_ref = next(it)

    x = x_ref[...]
    if has_attn:
        x = x + _dot(oa_ref[0], wo_ref[0:384, :])
        x = x + _dot(ob_ref[0], wo_ref[384:768, :])
        x = x + _dot(oc_ref[0], wo_ref[768:1152, :])
    xn = _rms(x, g_ref[...]).astype(BF16)
    for j in range(D_FF // tf):
        g = _dot(xn, wg_ref[:, j * tf:(j + 1) * tf])
        u = _dot(xn, wu_ref[:, j * tf:(j + 1) * tf])
        h = (g * jax.nn.sigmoid(g) * u).astype(BF16)
        c = _dot(h, wd_ref[j * tf:(j + 1) * tf, :])
        if j == 0:
            acc_ref[...] = c
        else:
            acc_ref[...] += c
    y = x + 0.5 * acc_ref[...]
    if has_final:
        y = _rms(y, fg_ref[...])
    o_ref[...] = y


def _ffn(x2, attn, wo, g, wg, wu, wd, final_g, l, *, tm=512, tf=256):
    n = x2.shape[0]
    has_attn = attn is not None
    has_final = final_g is not None
    const = lambda i: (0, 0)
    row = lambda i: (i, 0)
    in_specs = [pl.BlockSpec((tm, D_MODEL), row)]
    args = [x2]
    if has_attn:
        s = attn[0].shape[1]
        nt = s // tm
        amap = lambda i: (i // nt, i % nt, 0)
        for a in attn:
            in_specs.append(pl.BlockSpec((1, tm, 384), amap))
            args.append(a)
        in_specs.append(_layer(l, 1152, D_MODEL))
        args.append(wo)
    in_specs += [_layer(l, 1, D_MODEL), _layer(l, D_MODEL, D_FF), _layer(l, D_MODEL, D_FF),
                 _layer(l, D_FF, D_MODEL)]
    args += [g, wg, wu, wd]
    if has_final:
        in_specs.append(pl.BlockSpec((1, D_MODEL), const))
        args.append(final_g)
    return pl.pallas_call(
        functools.partial(_ffn_kernel, has_attn=has_attn, has_final=has_final, tf=tf),
        grid=(n // tm,),
        in_specs=in_specs,
        out_specs=pl.BlockSpec((tm, D_MODEL), row),
        out_shape=jax.ShapeDtypeStruct((n, D_MODEL), F32),
        scratch_shapes=[pltpu.VMEM((tm, D_MODEL), F32)],
        compiler_params=pltpu.CompilerParams(
            dimension_semantics=("arbitrary",), vmem_limit_bytes=VMEM_LIMIT),
        name="ffn",
    )(*args)


def _proj_kernel(x_ref, g_ref, w_ref, tab_ref, qn_ref, kvn_ref, wuq_ref, wukn_ref, wuv_ref,
                 aq_ref, ak_ref, av_ref, bq_ref, bk_ref, bv_ref, iq_ref, ik_ref, iw_ref,
                 cq_ref, ck_ref, cv_ref):
    xn = _rms(x_ref[...], g_ref[...]).astype(BF16)

    def mm(off, n):
        return _dot(xn, w_ref[:, off:off + n])

    aq_ref[0] = mm(OFF_AQ, 384).astype(BF16)
    ak_ref[0] = mm(OFF_AK, 384).astype(BF16)
    av_ref[0] = mm(OFF_AV, 384).astype(BF16)
    bv_ref[0] = mm(OFF_BV, 384).astype(BF16)
    iw_ref[0] = mm(OFF_IW, 128)[:, :IDX_HEADS]

    cos64, sin64 = tab_ref[:, 0:128], tab_ref[:, 128:256]
    cos32, sin32 = tab_ref[:, 256:384], tab_ref[:, 384:512]
    cosc, sinc = tab_ref[:, 512:640], tab_ref[:, 640:768]

    def rope_group(r_off, n, cos, sin):
        m = mm(OFF_ROPE + r_off, n)
        s = mm(OFF_ROPE + ROPE_W + r_off, n)
        return [m[:, c * 128:(c + 1) * 128] * cos + s[:, c * 128:(c + 1) * 128] * sin
                for c in range(n // 128)]

    for c, v in enumerate(rope_group(R_BQ, 384, cos64, sin64)):
        bq_ref[0, :, c * 128:(c + 1) * 128] = v.astype(BF16)
    for c, v in enumerate(rope_group(R_BK, 384, cos64, sin64)):
        bk_ref[0, :, c * 128:(c + 1) * 128] = v.astype(BF16)
    for c, v in enumerate(rope_group(R_IQ, 256, cos32, sin32)):
        iq_ref[0, :, c * 128:(c + 1) * 128] = v.astype(BF16)
    for c, v in enumerate(rope_group(R_IK, 256, cos32, sin32)):
        ik_ref[0, :, c * 128:(c + 1) * 128] = v.astype(BF16)
    krg = rope_group(R_KR, 128, cosc, sinc)[0]

    cqn = _rms(mm(OFF_CQ, C_Q_RANK), qn_ref[...]).astype(BF16)
    qm = _dot(cqn, wuq_ref[:, 0:C_QW])
    qs = _dot(cqn, wuq_ref[:, C_QW:2 * C_QW])
    ckn = _rms(mm(OFF_CKV, C_KV_RANK), kvn_ref[...]).astype(BF16)
    kn = _dot(ckn, wukn_ref[...])
    for h in range(C_HEADS):
        sl = slice(h * 128, (h + 1) * 128)
        cq_ref[0, :, sl] = (qm[:, sl] * cosc + qs[:, sl] * sinc).astype(BF16)
        ck_ref[0, :, sl] = (kn[:, sl] + krg).astype(BF16)
    cv_ref[0] = _dot(ckn, wuv_ref[...]).astype(BF16)


def _proj(x2, bsz, seq, g, wcat, tab, qn, kvn, wuq, wukn, wuv, l, *, tm=256):
    nt = seq // tm
    const = lambda i: (0, 0)
    omap = lambda i: (i // nt, i % nt, 0)
    widths = [384, 384, 384, 384, 384, 384, IQ_W, IQ_W, IDX_HEADS, C_QW, C_QW, 384]
    dtypes = [BF16] * 8 + [F32] + [BF16] * 3
    return pl.pallas_call(
        _proj_kernel,
        grid=(bsz * nt,),
        in_specs=[pl.BlockSpec((tm, D_MODEL), lambda i: (i, 0)),
                  _layer(l, 1, D_MODEL),
                  _layer(l, D_MODEL, W_CAT),
                  pl.BlockSpec((tm, 768), lambda i: (i % nt, 0)),
                  _layer(l, 1, C_Q_RANK),
                  _layer(l, 1, C_KV_RANK),
                  _layer(l, C_Q_RANK, 2 * C_QW),
                  _layer(l, C_KV_RANK, C_QW),
                  _layer(l, C_KV_RANK, 384)],
        out_specs=[pl.BlockSpec((1, tm, w), omap) for w in widths],
        out_shape=[jax.ShapeDtypeStruct((bsz, seq, w), d) for w, d in zip(widths, dtypes)],
        compiler_params=pltpu.CompilerParams(
            dimension_semantics=("arbitrary",), vmem_limit_bytes=VMEM_LIMIT),
        name="proj",
    )(x2, g, wcat, tab, qn, kvn, wuq, wukn, wuv)


A_QB = 2 * CHUNK
A_WIN = (A_LEFT_CHUNKS + 2) * CHUNK
A_PREV = A_LEFT_CHUNKS * CHUNK


def _chunk_attn_kernel(q_ref, kp_ref, kc_ref, vp_ref, vc_ref, bias_ref, o_ref, kcat_ref, vcat_ref,
                       *, tq):
    i = pl.program_id(1)
    kcat_ref[0:A_PREV, :] = kp_ref[0]
    kcat_ref[A_PREV:A_PREV + tq, :] = kc_ref[0]
    vcat_ref[0:A_PREV, :] = vp_ref[0]
    vcat_ref[A_PREV:A_PREV + tq, :] = vc_ref[0]
    lane = lax.broadcasted_iota(jnp.int32, (A_QB, LANES), 1)
    col_chunk = lax.broadcasted_iota(jnp.int32, (2 * A_QB, A_WIN), 1) // CHUNK
    for qb in range(tq // A_QB):
        first_chunk = i * (tq // CHUNK) + 2 * qb - A_LEFT_CHUNKS
        valid = (col_chunk + first_chunk) >= 0
        for p in range(A_HEADS // 2):
            ls = slice(p * LANES, (p + 1) * LANES)
            q = q_ref[0, qb * A_QB:(qb + 1) * A_QB, ls]
            qs = jnp.concatenate([jnp.where(lane < HEAD_DIM, q, jnp.zeros_like(q)),
                                  jnp.where(lane >= HEAD_DIM, q, jnp.zeros_like(q))], axis=0)
            kw = kcat_ref[qb * A_QB:qb * A_QB + A_WIN, ls]
            vw = vcat_ref[qb * A_QB:qb * A_QB + A_WIN, ls]
            s = _dot_t(qs, kw) + bias_ref[p]
            s = jnp.where(valid, s, NEG)
            m = jnp.max(s, axis=1, keepdims=True)
            e = jnp.exp(s - m)
            l = jnp.sum(e, axis=1, keepdims=True)
            o = _dot(e.astype(BF16), vw) / l
            o_ref[0, qb * A_QB:(qb + 1) * A_QB, ls] = jnp.where(
                lane < HEAD_DIM, o[0:A_QB], o[A_QB:2 * A_QB]).astype(BF16)


def _chunk_attn(q, k, v, bias, l, *, tq=512):
    bsz, seq, _ = q.shape
    assert tq == A_PREV and seq % tq == 0
    cur = lambda b, i: (b, i, 0)
    prev = lambda b, i: (b, jnp.maximum(i - 1, 0), 0)
    blk = (1, tq, A_W)
    return pl.pallas_call(
        functools.partial(_chunk_attn_kernel, tq=tq),
        grid=(bsz, seq // tq),
        in_specs=[pl.BlockSpec(blk, cur), pl.BlockSpec(blk, prev), pl.BlockSpec(blk, cur),
                  pl.BlockSpec(blk, prev), pl.BlockSpec(blk, cur),
                  _layer(l, A_HEADS // 2, 2 * A_QB, A_WIN)],
        out_specs=pl.BlockSpec(blk, cur),
        out_shape=jax.ShapeDtypeStruct((bsz, seq, A_W), BF16),
        scratch_shapes=[pltpu.VMEM((A_PREV + tq, A_W), BF16), pltpu.VMEM((A_PREV + tq, A_W), BF16)],
        compiler_params=pltpu.CompilerParams(
            dimension_semantics=("arbitrary", "arbitrary"), vmem_limit_bytes=VMEM_LIMIT),
        name="chunk_attn",
    )(q, k, k, v, v, bias)


def _fold(op, acc, s):
    for c in range(s.shape[1] // LANES):
        acc = op(acc, s[:, c * LANES:(c + 1) * LANES])
    return acc


def _row_all_lanes(op, x):
    return jnp.broadcast_to(op(x, axis=1, keepdims=True), x.shape)


def _pair_select(even_head, odd_head):
    lane = lax.broadcasted_iota(jnp.int32, even_head.shape, 1)
    return jnp.where(lane < HEAD_DIM, even_head, odd_head)


def _write_heads(o_ref, acc_ref, l_ref, n_heads):
    for p in range((n_heads + 1) // 2):
        o0 = acc_ref[2 * p] / _row_all_lanes(jnp.sum, l_ref[2 * p])
        if 2 * p + 1 < n_heads:
            o1 = acc_ref[2 * p + 1] / _row_all_lanes(jnp.sum, l_ref[2 * p + 1])
        else:
            o1 = jnp.zeros_like(o0)
        o_ref[0, :, p * LANES:(p + 1) * LANES] = _pair_select(o0, o1).astype(BF16)


def _resident(shape, index_map):
    return pl.BlockSpec(shape, index_map, pipeline_mode=pl.Buffered(1))


def _mla_kernel(q_ref, k_ref, v_ref, o_ref, m_ref, l_ref, acc_ref, *, tq):
    q0 = pl.multiple_of(pl.program_id(1) * tq, tq)
    nfull = q0 // tq
    diag_mask = (lax.broadcasted_iota(jnp.int32, (tq, tq), 1) // CHUNK
                 <= lax.broadcasted_iota(jnp.int32, (tq, tq), 0) // CHUNK)

    def scores(h, k0, masked):
        hs = slice(h * LANES, (h + 1) * LANES)
        s = _dot_t(q_ref[0, :, hs], k_ref[0, pl.ds(k0, tq), hs])
        return jnp.where(diag_mask, s, NEG) if masked else s

    def online_step(k0, masked):
        for h in range(C_HEADS):
            ps = slice((h // 2) * LANES, (h // 2 + 1) * LANES)
            s = scores(h, k0, masked)
            m_old = m_ref[h]
            m_new = jnp.maximum(m_old, _row_all_lanes(
                jnp.max, _fold(jnp.maximum, s[:, :LANES], s[:, LANES:])))
            alpha = jnp.exp2(m_old - m_new)
            p = jnp.exp2(s - jnp.tile(m_new, (1, tq // LANES)))
            l_ref[h] = alpha * l_ref[h] + _fold(jnp.add, p[:, :LANES], p[:, LANES:])
            acc_ref[h] = alpha * acc_ref[h] + _dot(p.astype(BF16), v_ref[0, pl.ds(k0, tq), ps])
            m_ref[h] = m_new

    m_ref[...] = jnp.full(m_ref.shape, NEG, F32)
    l_ref[...] = jnp.zeros(l_ref.shape, F32)
    acc_ref[...] = jnp.zeros(acc_ref.shape, F32)

    def body(kb, carry):
        online_step(pl.multiple_of(kb * tq, tq), False)
        return carry
    lax.fori_loop(0, nfull, body, 0)
    online_step(q0, True)
    _write_heads(o_ref, acc_ref, l_ref, C_HEADS)


def _mla(q, k, v, *, tq=512):
    bsz, seq, _ = q.shape
    assert seq % tq == 0
    stat = pltpu.VMEM((C_HEADS, tq, LANES), F32)
    return pl.pallas_call(
        functools.partial(_mla_kernel, tq=tq),
        grid=(bsz, seq // tq),
        in_specs=[pl.BlockSpec((1, tq, C_QW), lambda b, i: (b, i, 0)),
                  _resident((1, seq, C_QW), lambda b, i: (b, 0, 0)),
                  _resident((1, seq, 384), lambda b, i: (b, 0, 0))],
        out_specs=pl.BlockSpec((1, tq, 384), lambda b, i: (b, i, 0)),
        out_shape=jax.ShapeDtypeStruct((bsz, seq, 384), BF16),
        scratch_shapes=[stat, stat, stat],
        compiler_params=pltpu.CompilerParams(
            dimension_semantics=("arbitrary", "arbitrary"), vmem_limit_bytes=VMEM_LIMIT),
        name="mla",
    )(q, k, v)


def _dsa_kernel(iq_ref, iw_ref, ik_ref, q_ref, k_ref, v_ref, tri_ref, o_ref,
                key_ref, qm_ref, cand_ref, m_ref, l_ref, acc_ref, *, tq, tk, topk):
    q0 = pl.program_id(1) * tq
    nkb = (q0 + tq + tk - 1) // tk
    row_chunk = (q0 + lax.broadcasted_iota(jnp.int32, (tq, tk), 0)) // CHUNK
    col_chunk0 = lax.broadcasted_iota(jnp.int32, (tq, tk), 1) // CHUNK

    def admissible(kb):
        return (col_chunk0 + kb * (tk // CHUNK)) <= row_chunk

    iq = iq_ref[0]
    head_of_lane = lax.broadcasted_iota(jnp.int32, iq.shape, 1) // IDX_DIM
    for h in range(IDX_HEADS):
        qm_ref[h * tq:(h + 1) * tq, :] = jnp.where(head_of_lane == h, iq, jnp.zeros_like(iq))
    w = iw_ref[0]

    sub = 256

    def score_body(kb, carry):
        adm = admissible(kb)
        for c in range(tk // sub):
            k0 = pl.multiple_of(kb * tk + c * sub, sub)
            d = _dot_t(qm_ref[...], ik_ref[0, pl.ds(k0, sub), :])
            sc = jnp.zeros((tq, sub), F32)
            for h in range(IDX_HEADS):
                sc = sc + jnp.maximum(d[h * tq:(h + 1) * tq], 0.0) * w[:, h:h + 1]
            sc = jnp.where(adm[:, c * sub:(c + 1) * sub], sc, NEG)
            bits = pltpu.bitcast(sc, jnp.int32)
            key_ref[kb, :, c * sub:(c + 1) * sub] = bits ^ ((bits >> 31) & jnp.int32(0x7FFFFFFF))
        return carry

    lax.fori_loop(0, nkb, score_body, 0)

    def count(pred_fn, cand):
        cand_ref[...] = cand

        def body(kb, accv):
            parts = []
            for r0 in range(0, tq, COUNT_ROWS):
                rows = slice(r0, r0 + COUNT_ROWS)
                c = cand_ref[rows, :]
                parts.append(_fold(lambda a, x: jnp.where(pred_fn(x, c), a + 1.0, a),
                                   accv[rows], key_ref[kb, rows, :]))
            return jnp.concatenate(parts, axis=0)
        accv = lax.fori_loop(0, nkb, body, jnp.zeros((tq, LANES), F32))
        return _row_all_lanes(jnp.sum, accv)

    def count_ge(cand):
        return count(lambda x, c: x >= c, cand)

    kf = float(topk)
    zero = jnp.zeros((tq, LANES), jnp.int32)
    c_nonneg = count_ge(zero)
    c_pos = count_ge(zero + 1)
    lo0 = jnp.where(c_nonneg >= kf, 0, INT_MIN)
    clo0 = jnp.where(c_nonneg >= kf, c_nonneg, (nkb * tk).astype(F32))
    settled = jnp.where(c_nonneg >= kf, jnp.where(c_pos < kf, 1, 0), 0)

    def sweep(b, lo, clo):
        cand = lo + lax.shift_left(jnp.int32(1), b)
        cc = count_ge(cand)
        ok = cc >= kf
        return jnp.where(ok, cand, lo), jnp.where(ok, cc, clo)

    lo1, clo1 = lax.fori_loop(0, 30 - EXIT_CHECK_BIT, lambda j, c: sweep(30 - j, *c), (lo0, clo0))

    def search_cond(carry):
        b, open_rows, _, _ = carry
        return jnp.logical_and(b >= 0, open_rows > 0)

    def search_body(carry):
        b, _, lo, clo = carry
        lo, clo = sweep(b, lo, clo)
        open_rows = jnp.max(jnp.where(clo == kf, 0, jnp.where(settled > 0, 0, 1)))
        return b - 1, open_rows, lo, clo

    _, _, thr, cnt_thr = lax.while_loop(search_cond, search_body,
                                        (jnp.int32(EXIT_CHECK_BIT), jnp.int32(1), lo1, clo1))
    has_tie = jnp.max(cnt_thr) > kf
    thr_t = jnp.tile(thr, (1, tk // LANES))

    q = q_ref[0]
    qlane = lax.broadcasted_iota(jnp.int32, (tq, LANES), 1)

    def scores(h, kb, bias):
        ps = slice((h // 2) * LANES, (h // 2 + 1) * LANES)
        qp = q[:, ps]
        qh = jnp.where((qlane < HEAD_DIM) if h % 2 == 0 else (qlane >= HEAD_DIM),
                       qp, jnp.zeros_like(qp))
        k0 = pl.multiple_of(kb * tk, tk)
        return _dot_t(qh, k_ref[0, pl.ds(k0, tk), ps]) + bias

    def max_step(kb, bias):
        key_ref[kb] = pltpu.bitcast(bias, jnp.int32)
        for h in range(B_HEADS):
            m_ref[h] = _fold(jnp.maximum, m_ref[h], scores(h, kb, bias))

    m_ref[...] = jnp.full(m_ref.shape, NEG, F32)

    def plain_select():
        def body(kb, carry):
            adm_bias = jnp.where(admissible(kb), 0.0, NEG)
            max_step(kb, jnp.where(key_ref[kb] >= thr_t, adm_bias, NEG))
            return carry
        lax.fori_loop(0, nkb, body, 0)

    def tie_select():
        need = jnp.tile(kf - count(lambda x, c: x > c, thr), (1, tk // LANES))

        def body(kb, eq_before):
            keyb = key_ref[kb]
            eq = keyb == thr_t
            eqf = jnp.where(eq, 1.0, 0.0)
            rank = jnp.tile(eq_before, (1, tk // LANES)) + _dot(eqf.astype(BF16), tri_ref[...])
            take = jnp.where(keyb > thr_t, 1, jnp.where(eq, jnp.where(rank < need, 1, 0), 0))
            adm_bias = jnp.where(admissible(kb), 0.0, NEG)
            max_step(kb, jnp.where(take > 0, adm_bias, NEG))
            return eq_before + _row_all_lanes(jnp.sum, _fold(jnp.add, jnp.zeros((tq, LANES), F32), eqf))
        lax.fori_loop(0, nkb, body, jnp.zeros((tq, LANES), F32))

    lax.cond(has_tie, tie_select, plain_select)

    for h in range(B_HEADS):
        m_ref[h] = _row_all_lanes(jnp.max, m_ref[h])
    l_ref[...] = jnp.zeros(l_ref.shape, F32)
    acc_ref[...] = jnp.zeros(acc_ref.shape, F32)

    def acc_body(kb, carry):
        bias = pltpu.bitcast(key_ref[kb], F32)
        k0 = pl.multiple_of(kb * tk, tk)
        for h in range(B_HEADS):
            ps = slice((h // 2) * LANES, (h // 2 + 1) * LANES)
            p = jnp.exp2(scores(h, kb, bias) - jnp.tile(m_ref[h], (1, tk // LANES)))
            l_ref[h] = _fold(jnp.add, l_ref[h], p)
            acc_ref[h] += _dot(p.astype(BF16), v_ref[0, pl.ds(k0, tk), ps])
        return carry

    lax.fori_loop(0, nkb, acc_body, 0)
    _write_heads(o_ref, acc_ref, l_ref, B_HEADS)


def _dsa(iq, iw, ik, q, k, v, *, tq=512, tk=512):
    bsz, seq, _ = q.shape
    assert seq % tq == 0 and seq % tk == 0
    topk = min(TOPK_MAX, seq // 4)
    tri = (np.arange(tk)[:, None] < np.arange(tk)[None, :]).astype(np.float32)
    tri = jnp.asarray(tri, BF16)
    qmap = lambda b, i: (b, i, 0)
    full = lambda b, i: (b, 0, 0)
    stat = pltpu.VMEM((B_HEADS, tq, LANES), F32)
    return pl.pallas_call(
        functools.partial(_dsa_kernel, tq=tq, tk=tk, topk=topk),
        grid=(bsz, seq // tq),
        in_specs=[pl.BlockSpec((1, tq, IQ_W), qmap),
                  pl.BlockSpec((1, tq, IDX_HEADS), qmap),
                  _resident((1, seq, IQ_W), full),
                  pl.BlockSpec((1, tq, B_WP), qmap),
                  _resident((1, seq, B_WP), full),
                  _resident((1, seq, B_WP), full),
                  pl.BlockSpec((tk, tk), lambda b, i: (0, 0))],
        out_specs=pl.BlockSpec((1, tq, B_WP), qmap),
        out_shape=jax.ShapeDtypeStruct((bsz, seq, B_WP), BF16),
        scratch_shapes=[pltpu.VMEM((seq // tk, tq, tk), jnp.int32),
                        pltpu.VMEM((IDX_HEADS * tq, IQ_W), BF16),
                        pltpu.VMEM((tq, LANES), jnp.int32),
                        stat, stat, stat],
        compiler_params=pltpu.CompilerParams(
            dimension_semantics=("arbitrary", "arbitrary"), vmem_limit_bytes=VMEM_LIMIT),
        name="dsa",
    )(iq, iw, ik, q, k, v, tri)


def _rot_cols(w, d):
    k, n = w.shape
    w3 = w.reshape(k, n // d, d)
    return jnp.concatenate([-w3[..., d // 2:], w3[..., :d // 2]], axis=-1).reshape(k, n)


def _pad_cols(w, n):
    return jnp.pad(w, ((0, 0), (0, n - w.shape[1])))


def _prep_proj_weights(w_in, c_w_uq, c_w_ukv):
    offs = np.cumsum([0, A_W, A_W, A_W, B_W, B_W, B_W, IQ_W, IDX_DIM, IDX_HEADS,
                      C_Q_RANK, C_KV_RANK, C_ROPE])
    (aq, ak, av, bq, bk, bv, iq, ik, iw, cq, ckv, ckr) = [
        w_in[:, offs[j]:offs[j + 1]] for j in range(12)]
    ascale = HEAD_DIM ** -0.5
    bscale = HEAD_DIM ** -0.5 * LOG2E
    zeros64 = jnp.zeros((D_MODEL, 64), F32)
    zeros32 = jnp.zeros((D_MODEL, 32), F32)
    kr_group = jnp.concatenate([zeros64, ckr, zeros32], axis=1)
    kr_group_rot = jnp.concatenate([zeros64, _rot_cols(ckr, C_ROPE), zeros32], axis=1)
    plain = [aq * ascale, ak, av, _pad_cols(bv, B_WP), cq, ckv,
             _pad_cols(iw * (IDX_HEADS * IDX_DIM) ** -0.5, 128)]
    rope_main = [_pad_cols(bq * bscale, B_WP), _pad_cols(bk, B_WP), iq, jnp.tile(ik, (1, IDX_HEADS)),
                 kr_group]
    rope_rot = [_pad_cols(_rot_cols(bq, HEAD_DIM) * bscale, B_WP), _pad_cols(_rot_cols(bk, HEAD_DIM), B_WP),
                _rot_cols(iq, IDX_DIM), jnp.tile(_rot_cols(ik, IDX_DIM), (1, IDX_HEADS)), kr_group_rot]
    wcat = jnp.concatenate(plain + rope_main + rope_rot, axis=1).astype(BF16)

    cscale = (C_NOPE + C_ROPE) ** -0.5 * LOG2E
    uq = c_w_uq.reshape(C_Q_RANK, C_HEADS, C_NOPE + C_ROPE) * cscale
    zq = jnp.zeros((C_Q_RANK, C_HEADS, 32), F32)
    uq_main = jnp.concatenate([uq, zq], axis=-1).reshape(C_Q_RANK, C_QW)
    uq_rot_r = jnp.concatenate([-uq[..., C_NOPE + C_ROPE // 2:], uq[..., C_NOPE:C_NOPE + C_ROPE // 2]],
                               axis=-1)
    uq_rot = jnp.concatenate([jnp.zeros((C_Q_RANK, C_HEADS, C_NOPE), F32), uq_rot_r, zq],
                             axis=-1).reshape(C_Q_RANK, C_QW)
    wuq = jnp.concatenate([uq_main, uq_rot], axis=1).astype(BF16)
    ukv = c_w_ukv.reshape(C_KV_RANK, C_HEADS, C_NOPE + C_V)
    wukn = jnp.concatenate([ukv[..., :C_NOPE], jnp.zeros((C_KV_RANK, C_HEADS, 64), F32)],
                           axis=-1).reshape(C_KV_RANK, C_QW).astype(BF16)
    wuv = _pad_cols(ukv[..., C_NOPE:].reshape(C_KV_RANK, C_HEADS * C_V), 384).astype(BF16)
    return wcat, wuq, wukn, wuv


def _rope_tables(seq):
    pos = jnp.arange(seq, dtype=F32)[:, None]

    def cs(d):
        inv = ROPE_THETA ** (-jnp.arange(0, d, 2, dtype=F32) / d)
        ang = pos * inv[None, :]
        c, s = jnp.cos(ang), jnp.sin(ang)
        return (jnp.tile(jnp.concatenate([c, c], axis=1), (1, LANES // d)),
                jnp.tile(jnp.concatenate([s, s], axis=1), (1, LANES // d)))

    c64, s64 = cs(HEAD_DIM)
    c32, s32 = cs(C_ROPE)
    lane = jnp.arange(LANES)[None, :]
    roped = (lane >= C_NOPE) & (lane < C_NOPE + C_ROPE)
    cc = jnp.where(roped, c32, 1.0)
    sc = jnp.where(roped, s32, 0.0)
    return jnp.concatenate([c64, s64, c32, s32, cc, sc], axis=1)


def _chunk_bias(rel_bias):
    rb = rel_bias.astype(F32)
    n_rel = A_QB + A_WIN - 1
    below = jnp.broadcast_to(rb[:, :1], (A_HEADS, A_QB - 1 - (CHUNK - 1)))
    above = jnp.broadcast_to(rb[:, -1:], (A_HEADS, A_PREV + A_QB - 1 - A_REL_MAX))
    by_rel_desc = jnp.concatenate([below, rb, above], axis=1)[:, ::-1]
    assert by_rel_desc.shape[1] == n_rel
    v = jnp.roll(by_rel_desc, -(A_QB - 1), axis=1)
    flat = jnp.broadcast_to(v[:, None, :], (A_HEADS, A_QB, n_rel)).reshape(A_HEADS, A_QB * n_rel)
    toeplitz = flat[:, :A_QB * (n_rel - 1)].reshape(A_HEADS, A_QB, n_rel - 1)[:, :, :A_WIN]
    r = np.arange(A_QB)
    ki = np.arange(A_WIN)[None, :] - CHUNK * (r // CHUNK)[:, None]
    inwin = (ki >= 0) & (ki < (A_LEFT_CHUNKS + 1) * CHUNK)
    b = jnp.where(jnp.asarray(inwin)[None], toeplitz, NEG)
    return b.reshape(A_HEADS // 2, 2 * A_QB, A_WIN)


def _prep_wo(w_out):
    pad64 = jnp.zeros((64, D_MODEL), F32)
    return jnp.concatenate([w_out[:A_W], w_out[A_W:A_W + B_W], pad64,
                            w_out[A_W + B_W:], pad64], axis=0).astype(BF16)


def kernel(x, ffn1_norm, ffn1_w_gate, ffn1_w_up, ffn1_w_down, mix_norm, w_in, a_rel_bias, c_q_norm,
           c_kv_norm, c_w_uq, c_w_ukv, w_out, ffn2_norm, ffn2_w_gate, ffn2_w_up, ffn2_w_down,
           final_norm):
    bsz, seq, _ = x.shape
    depth = w_in.shape[0]
    tab = _rope_tables(seq)
    f1g, f1u, f1d = ffn1_w_gate.astype(BF16), ffn1_w_up.astype(BF16), ffn1_w_down.astype(BF16)
    f2g, f2u, f2d = ffn2_w_gate.astype(BF16), ffn2_w_up.astype(BF16), ffn2_w_down.astype(BF16)
    wcat, wuq, wukn, wuv = jax.vmap(_prep_proj_weights)(w_in, c_w_uq, c_w_ukv)
    abias = jax.vmap(_chunk_bias)(a_rel_bias)
    wo = jax.vmap(_prep_wo)(w_out)
    norms = [n[:, None, :] for n in (ffn1_norm, mix_norm, c_q_norm, c_kv_norm, ffn2_norm)]
    n_ffn1, n_mix, n_cq, n_ckv, n_ffn2 = norms
    x2 = x.reshape(bsz * seq, D_MODEL)
    for l in range(depth):
        x2 = _ffn(x2, None, None, n_ffn1, f1g, f1u, f1d, None, l)
        (aq, ak, av, bq, bk, bv, iq, ik, iw, cq, ck, cv) = _proj(
            x2, bsz, seq, n_mix, wcat, tab, n_cq, n_ckv, wuq, wukn, wuv, l)
        o_a = _chunk_attn(aq, ak, av, abias, l)
        o_b = _dsa(iq, iw, ik, bq, bk, bv)
        o_c = _mla(cq, ck, cv)
        x2 = _ffn(x2, (o_a, o_b, o_c), wo, n_ffn2, f2g, f2u, f2d,
                  final_norm[None] if l == depth - 1 else None, l)
    return x2.reshape(bsz, seq, D_MODEL)
```

```python
import functools
import math

import numpy as np
import jax
import jax.numpy as jnp
from jax import lax
from jax.experimental import pallas as pl
from jax.experimental.pallas import tpu as pltpu

F32 = jnp.float32
BF16 = jnp.bfloat16

D_MODEL = 1024
CHUNK = 64
HEAD_DIM = 64
A_HEADS = 6
A_LEFT_CHUNKS = 8
A_REL_MAX = 128
B_HEADS = 5
IDX_HEADS = 8
IDX_DIM = 32
TOPK_MAX = 256
C_HEADS = 5
C_Q_RANK = 384
C_KV_RANK = 256
C_NOPE = 64
C_ROPE = 32
C_V = 64
D_FF = 2816
ROPE_THETA = 10000.0
EPS = 1e-6
NEG = -1e30
LOG2E = math.log2(math.e)

LANES = 128
COUNT_ROWS = 64
EXIT_CHECK_BIT = 10
A_W = A_HEADS * HEAD_DIM
B_W = B_HEADS * HEAD_DIM
B_WP = 384
C_QW = C_HEADS * LANES
IQ_W = IDX_HEADS * IDX_DIM
VMEM_LIMIT = 56 * 1024 * 1024

OFF_AQ, OFF_AK, OFF_AV, OFF_BV, OFF_CQ, OFF_CKV, OFF_IW = 0, 384, 768, 1152, 1536, 1920, 2176
OFF_ROPE = 2304
ROPE_W = 384 + 384 + 256 + 256 + 128
R_BQ, R_BK, R_IQ, R_IK, R_KR = 0, 384, 768, 1024, 1280
W_CAT = OFF_ROPE + 2 * ROPE_W


def _sortable_key_of(x):
    b = np.float32(x).view(np.int32)
    return int(b ^ ((b >> 31) & np.int32(0x7FFFFFFF)))


NEG_KEY = _sortable_key_of(NEG)
INT_MIN = -(2 ** 31)


def _dot(a, b):
    return jnp.dot(a, b, preferred_element_type=F32)


def _dot_t(a, b):
    return lax.dot_general(a, b, (((1,), (1,)), ((), ())), preferred_element_type=F32)


def _rms(x, g):
    ms = jnp.mean(x * x, axis=-1, keepdims=True)
    return x * lax.rsqrt(ms + EPS) * g


def _layer(l, *tail):
    return pl.BlockSpec((None,) + tail, lambda *_: (l,) + (0,) * len(tail))


def _ffn_kernel(*refs, has_attn, has_final, tf):
    it = iter(refs)
    x_ref = next(it)
    if has_attn:
        oa_ref, ob_ref, oc_ref, wo_ref = next(it), next(it), next(it), next(it)
    g_ref, wg_ref, wu_ref, wd_ref = next(it), next(it), next(it), next(it)
    fg_ref = next(it) if has_final else None
    o_ref = next(it)
    acc_ref = next(it)

    x = x_ref[...]
    if has_attn:
        x = x + _dot(oa_ref[0], wo_ref[0:384, :])
        x = x + _dot(ob_ref[0], wo_ref[384:768, :])
        x = x + _dot(oc_ref[0], wo_ref[768:1152, :])
    xn = _rms(x, g_ref[...]).astype(BF16)
    for j in range(D_FF // tf):
        g = _dot(xn, wg_ref[:, j * tf:(j + 1) * tf])
        u = _dot(xn, wu_ref[:, j * tf:(j + 1) * tf])
        h = (g * jax.nn.sigmoid(g) * u).astype(BF16)
        c = _dot(h, wd_ref[j * tf:(j + 1) * tf, :])
        if j == 0:
            acc_ref[...] = c
        else:
            acc_ref[...] += c
    y = x + 0.5 * acc_ref[...]
    if has_final:
        y = _rms(y, fg_ref[...])
    o_ref[...] = y


def _ffn(x2, attn, wo, g, wg, wu, wd, final_g, l, *, tm=512, tf=256):
    n = x2.shape[0]
    has_attn = attn is not None
    has_final = final_g is not None
    const = lambda i: (0, 0)
    row = lambda i: (i, 0)
    in_specs = [pl.BlockSpec((tm, D_MODEL), row)]
    args = [x2]
    if has_attn:
        s = attn[0].shape[1]
        nt = s // tm
        amap = lambda i: (i // nt, i % nt, 0)
        for a in attn:
            in_specs.append(pl.BlockSpec((1, tm, 384), amap))
            args.append(a)
        in_specs.append(_layer(l, 1152, D_MODEL))
        args.append(wo)
    in_specs += [_layer(l, 1, D_MODEL), _layer(l, D_MODEL, D_FF), _layer(l, D_MODEL, D_FF),
                 _layer(l, D_FF, D_MODEL)]
    args += [g, wg, wu, wd]
    if has_final:
        in_specs.append(pl.BlockSpec((1, D_MODEL), const))
        args.append(final_g)
    return pl.pallas_call(
        functools.partial(_ffn_kernel, has_attn=has_attn, has_final=has_final, tf=tf),
        grid=(n // tm,),
        in_specs=in_specs,
        out_specs=pl.BlockSpec((tm, D_MODEL), row),
        out_shape=jax.ShapeDtypeStruct((n, D_MODEL), F32),
        scratch_shapes=[pltpu.VMEM((tm, D_MODEL), F32)],
        compiler_params=pltpu.CompilerParams(
            dimension_semantics=("arbitrary",), vmem_limit_bytes=VMEM_LIMIT),
        name="ffn",
    )(*args)


def _proj_kernel(x_ref, g_ref, w_ref, tab_ref, qn_ref, kvn_ref, wuq_ref, wukn_ref, wuv_ref,
                 aq_ref, ak_ref, av_ref, bq_ref, bk_ref, bv_ref, iq_ref, ik_ref, iw_ref,
                 cq_ref, ck_ref, cv_ref):
    xn = _rms(x_ref[...], g_ref[...]).astype(BF16)

    def mm(off, n):
        return _dot(xn, w_ref[:, off:off + n])

    aq_ref[0] = mm(OFF_AQ, 384).astype(BF16)
    ak_ref[0] = mm(OFF_AK, 384).astype(BF16)
    av_ref[0] = mm(OFF_AV, 384).astype(BF16)
    bv_ref[0] = mm(OFF_BV, 384).astype(BF16)
    iw_ref[0] = mm(OFF_IW, 128)[:, :IDX_HEADS]

    cos64, sin64 = tab_ref[:, 0:128], tab_ref[:, 128:256]
    cos32, sin32 = tab_ref[:, 256:384], tab_ref[:, 384:512]
    cosc, sinc = tab_ref[:, 512:640], tab_ref[:, 640:768]

    def rope_group(r_off, n, cos, sin):
        m = mm(OFF_ROPE + r_off, n)
        s = mm(OFF_ROPE + ROPE_W + r_off, n)
        return [m[:, c * 128:(c + 1) * 128] * cos + s[:, c * 128:(c + 1) * 128] * sin
                for c in range(n // 128)]

    for c, v in enumerate(rope_group(R_BQ, 384, cos64, sin64)):
        bq_ref[0, :, c * 128:(c + 1) * 128] = v.astype(BF16)
    for c, v in enumerate(rope_group(R_BK, 384, cos64, sin64)):
        bk_ref[0, :, c * 128:(c + 1) * 128] = v.astype(BF16)
    for c, v in enumerate(rope_group(R_IQ, 256, cos32, sin32)):
        iq_ref[0, :, c * 128:(c + 1) * 128] = v.astype(BF16)
    for c, v in enumerate(rope_group(R_IK, 256, cos32, sin32)):
        ik_ref[0, :, c * 128:(c + 1) * 128] = v.astype(BF16)
    krg = rope_group(R_KR, 128, cosc, sinc)[0]

    cqn = _rms(mm(OFF_CQ, C_Q_RANK), qn_ref[...]).astype(BF16)
    qm = _dot(cqn, wuq_ref[:, 0:C_QW])
    qs = _dot(cqn, wuq_ref[:, C_QW:2 * C_QW])
    ckn = _rms(mm(OFF_CKV, C_KV_RANK), kvn_ref[...]).astype(BF16)
    kn = _dot(ckn, wukn_ref[...])
    for h in range(C_HEADS):
        sl = slice(h * 128, (h + 1) * 128)
        cq_ref[0, :, sl] = (qm[:, sl] * cosc + qs[:, sl] * sinc).astype(BF16)
        ck_ref[0, :, sl] = (kn[:, sl] + krg).astype(BF16)
    cv_ref[0] = _dot(ckn, wuv_ref[...]).astype(BF16)


def _proj(x2, bsz, seq, g, wcat, tab, qn, kvn, wuq, wukn, wuv, l, *, tm=512):
    nt = seq // tm
    const = lambda i: (0, 0)
    omap = lambda i: (i // nt, i % nt, 0)
    widths = [384, 384, 384, 384, 384, 384, IQ_W, IQ_W, IDX_HEADS, C_QW, C_QW, 384]
    dtypes = [BF16] * 8 + [F32] + [BF16] * 3
    return pl.pallas_call(
        _proj_kernel,
        grid=(bsz * nt,),
        in_specs=[pl.BlockSpec((tm, D_MODEL), lambda i: (i, 0)),
                  _layer(l, 1, D_MODEL),
                  _layer(l, D_MODEL, W_CAT),
                  pl.BlockSpec((tm, 768), lambda i: (i % nt, 0)),
                  _layer(l, 1, C_Q_RANK),
                  _layer(l, 1, C_KV_RANK),
                  _layer(l, C_Q_RANK, 2 * C_QW),
                  _layer(l, C_KV_RANK, C_QW),
                  _layer(l, C_KV_RANK, 384)],
        out_specs=[pl.BlockSpec((1, tm, w), omap) for w in widths],
        out_shape=[jax.ShapeDtypeStruct((bsz, seq, w), d) for w, d in zip(widths, dtypes)],
        compiler_params=pltpu.CompilerParams(
            dimension_semantics=("arbitrary",), vmem_limit_bytes=VMEM_LIMIT),
        name="proj",
    )(x2, g, wcat, tab, qn, kvn, wuq, wukn, wuv)


A_QB = 2 * CHUNK
A_WIN = (A_LEFT_CHUNKS + 2) * CHUNK
A_PREV = A_LEFT_CHUNKS * CHUNK


def _chunk_attn_kernel(q_ref, kp_ref, kc_ref, vp_ref, vc_ref, bias_ref, o_ref, kcat_ref, vcat_ref,
                       *, tq):
    i = pl.program_id(1)
    kcat_ref[0:A_PREV, :] = kp_ref[0]
    kcat_ref[A_PREV:A_PREV + tq, :] = kc_ref[0]
    vcat_ref[0:A_PREV, :] = vp_ref[0]
    vcat_ref[A_PREV:A_PREV + tq, :] = vc_ref[0]
    lane = lax.broadcasted_iota(jnp.int32, (A_QB, LANES), 1)
    col_chunk = lax.broadcasted_iota(jnp.int32, (2 * A_QB, A_WIN), 1) // CHUNK
    for qb in range(tq // A_QB):
        first_chunk = i * (tq // CHUNK) + 2 * qb - A_LEFT_CHUNKS
        valid = (col_chunk + first_chunk) >= 0
        for p in range(A_HEADS // 2):
            ls = slice(p * LANES, (p + 1) * LANES)
            q = q_ref[0, qb * A_QB:(qb + 1) * A_QB, ls]
            qs = jnp.concatenate([jnp.where(lane < HEAD_DIM, q, jnp.zeros_like(q)),
                                  jnp.where(lane >= HEAD_DIM, q, jnp.zeros_like(q))], axis=0)
            kw = kcat_ref[qb * A_QB:qb * A_QB + A_WIN, ls]
            vw = vcat_ref[qb * A_QB:qb * A_QB + A_WIN, ls]
            s = _dot_t(qs, kw) + bias_ref[p]
            s = jnp.where(valid, s, NEG)
            m = jnp.max(s, axis=1, keepdims=True)
            e = jnp.exp(s - m)
            l = jnp.sum(e, axis=1, keepdims=True)
            o = _dot(e.astype(BF16), vw) / l
            o_ref[0, qb * A_QB:(qb + 1) * A_QB, ls] = jnp.where(
                lane < HEAD_DIM, o[0:A_QB], o[A_QB:2 * A_QB]).astype(BF16)


def _chunk_attn(q, k, v, bias, l, *, tq=512):
    bsz, seq, _ = q.shape
    assert tq == A_PREV and seq % tq == 0
    cur = lambda b, i: (b, i, 0)
    prev = lambda b, i: (b, jnp.maximum(i - 1, 0), 0)
    blk = (1, tq, A_W)
    return pl.pallas_call(
        functools.partial(_chunk_attn_kernel, tq=tq),
        grid=(bsz, seq // tq),
        in_specs=[pl.BlockSpec(blk, cur), pl.BlockSpec(blk, prev), pl.BlockSpec(blk, cur),
                  pl.BlockSpec(blk, prev), pl.BlockSpec(blk, cur),
                  _layer(l, A_HEADS // 2, 2 * A_QB, A_WIN)],
        out_specs=pl.BlockSpec(blk, cur),
        out_shape=jax.ShapeDtypeStruct((bsz, seq, A_W), BF16),
        scratch_shapes=[pltpu.VMEM((A_PREV + tq, A_W), BF16), pltpu.VMEM((A_PREV + tq, A_W), BF16)],
        compiler_params=pltpu.CompilerParams(
            dimension_semantics=("arbitrary", "arbitrary"), vmem_limit_bytes=VMEM_LIMIT),
        name="chunk_attn",
    )(q, k, k, v, v, bias)


def _fold(op, acc, s):
    for c in range(s.shape[1] // LANES):
        acc = op(acc, s[:, c * LANES:(c + 1) * LANES])
    return acc


def _row_all_lanes(op, x):
    return jnp.broadcast_to(op(x, axis=1, keepdims=True), x.shape)


def _pair_select(even_head, odd_head):
    lane = lax.broadcasted_iota(jnp.int32, even_head.shape, 1)
    return jnp.where(lane < HEAD_DIM, even_head, odd_head)


def _write_heads(o_ref, acc_ref, l_ref, n_heads):
    for p in range((n_heads + 1) // 2):
        o0 = acc_ref[2 * p] / _row_all_lanes(jnp.sum, l_ref[2 * p])
        if 2 * p + 1 < n_heads:
            o1 = acc_ref[2 * p + 1] / _row_all_lanes(jnp.sum, l_ref[2 * p + 1])
        else:
            o1 = jnp.zeros_like(o0)
        o_ref[0, :, p * LANES:(p + 1) * LANES] = _pair_select(o0, o1).astype(BF16)


def _resident(shape, index_map):
    return pl.BlockSpec(shape, index_map, pipeline_mode=pl.Buffered(1))


def _mla_kernel(q_ref, k_ref, v_ref, o_ref, s0_ref, s1_ref, m_ref, l_ref, acc_ref, *, tq):
    q0 = pl.multiple_of(pl.program_id(1) * tq, tq)
    nfull = q0 // tq
    diag_mask = (lax.broadcasted_iota(jnp.int32, (tq, tq), 1) // CHUNK
                 <= lax.broadcasted_iota(jnp.int32, (tq, tq), 0) // CHUNK)

    def produce(kb, s_ref):
        k0 = pl.multiple_of(kb * tq, tq)
        for h in range(C_HEADS):
            hs = slice(h * LANES, (h + 1) * LANES)
            s_ref[h] = _dot_t(q_ref[0, :, hs], k_ref[0, pl.ds(k0, tq), hs])

    def consume(kb, s_ref, masked):
        k0 = pl.multiple_of(kb * tq, tq)
        for h in range(C_HEADS):
            ps = slice((h // 2) * LANES, (h // 2 + 1) * LANES)
            s = s_ref[h]
            if masked:
                s = jnp.where(diag_mask, s, NEG)
            m_old = m_ref[h]
            m_new = jnp.maximum(m_old, _row_all_lanes(
                jnp.max, _fold(jnp.maximum, s[:, :LANES], s[:, LANES:])))
            alpha = jnp.exp2(m_old - m_new)
            p = jnp.exp2(s - jnp.tile(m_new, (1, tq // LANES)))
            l_ref[h] = alpha * l_ref[h] + _fold(jnp.add, p[:, :LANES], p[:, LANES:])
            acc_ref[h] = alpha * acc_ref[h] + _dot(p.astype(BF16), v_ref[0, pl.ds(k0, tq), ps])
            m_ref[h] = m_new

    m_ref[...] = jnp.full(m_ref.shape, NEG, F32)
    l_ref[...] = jnp.zeros(l_ref.shape, F32)
    acc_ref[...] = jnp.zeros(acc_ref.shape, F32)

    produce(0, s0_ref)

    def body(j, carry):
        produce(2 * j + 1, s1_ref)
        consume(2 * j, s0_ref, False)
        produce(2 * j + 2, s0_ref)
        consume(2 * j + 1, s1_ref, False)
        return carry
    lax.fori_loop(0, nfull // 2, body, 0)

    @pl.when(nfull % 2 == 0)
    def _():
        consume(nfull, s0_ref, True)

    @pl.when(nfull % 2 == 1)
    def _():
        produce(nfull, s1_ref)
        consume(nfull - 1, s0_ref, False)
        consume(nfull, s1_ref, True)
    _write_heads(o_ref, acc_ref, l_ref, C_HEADS)


def _mla(q, k, v, *, tq=512):
    bsz, seq, _ = q.shape
    assert seq % tq == 0
    stat = pltpu.VMEM((C_HEADS, tq, LANES), F32)
    return pl.pallas_call(
        functools.partial(_mla_kernel, tq=tq),
        grid=(bsz, seq // tq),
        in_specs=[pl.BlockSpec((1, tq, C_QW), lambda b, i: (b, i, 0)),
                  _resident((1, seq, C_QW), lambda b, i: (b, 0, 0)),
                  _resident((1, seq, 384), lambda b, i: (b, 0, 0))],
        out_specs=pl.BlockSpec((1, tq, 384), lambda b, i: (b, i, 0)),
        out_shape=jax.ShapeDtypeStruct((bsz, seq, 384), BF16),
        scratch_shapes=[pltpu.VMEM((C_HEADS, tq, tq), F32), pltpu.VMEM((C_HEADS, tq, tq), F32),
                        stat, stat, stat],
        compiler_params=pltpu.CompilerParams(
            dimension_semantics=("arbitrary", "arbitrary"), vmem_limit_bytes=VMEM_LIMIT),
        name="mla",
    )(q, k, v)


def _dsa_kernel(iq_ref, iw_ref, ik_ref, q_ref, k_ref, v_ref, tri_ref, o_ref,
                key_ref, qm_ref, cand_ref, m_ref, l_ref, acc_ref, *, tq, tk, topk):
    q0 = pl.program_id(1) * tq
    nkb = (q0 + tq + tk - 1) // tk
    row_chunk = (q0 + lax.broadcasted_iota(jnp.int32, (tq, tk), 0)) // CHUNK
    col_chunk0 = lax.broadcasted_iota(jnp.int32, (tq, tk), 1) // CHUNK

    def admissible(kb):
        return (col_chunk0 + kb * (tk // CHUNK)) <= row_chunk

    iq = iq_ref[0]
    head_of_lane = lax.broadcasted_iota(jnp.int32, iq.shape, 1) // IDX_DIM
    for h in range(IDX_HEADS):
        qm_ref[h * tq:(h + 1) * tq, :] = jnp.where(head_of_lane == h, iq, jnp.zeros_like(iq))
    w = iw_ref[0]

    sub = 256

    def score_body(kb, carry):
        adm = admissible(kb)
        for c in range(tk // sub):
            k0 = pl.multiple_of(kb * tk + c * sub, sub)
            d = _dot_t(qm_ref[...], ik_ref[0, pl.ds(k0, sub), :])
            sc = jnp.zeros((tq, sub), F32)
            for h in range(IDX_HEADS):
                sc = sc + jnp.maximum(d[h * tq:(h + 1) * tq], 0.0) * w[:, h:h + 1]
            sc = jnp.where(adm[:, c * sub:(c + 1) * sub], sc, NEG)
            bits = pltpu.bitcast(sc, jnp.int32)
            key_ref[kb, :, c * sub:(c + 1) * sub] = bits ^ ((bits >> 31) & jnp.int32(0x7FFFFFFF))
        return carry

    lax.fori_loop(0, nkb, score_body, 0)

    def count(pred_fn, cand):
        cand_ref[...] = cand

        def body(kb, accv):
            parts = []
            for r0 in range(0, tq, COUNT_ROWS):
                rows = slice(r0, r0 + COUNT_ROWS)
                c = cand_ref[rows, :]
                parts.append(_fold(lambda a, x: jnp.where(pred_fn(x, c), a + 1.0, a),
                                   accv[rows], key_ref[kb, rows, :]))
            return jnp.concatenate(parts, axis=0)
        accv = lax.fori_loop(0, nkb, body, jnp.zeros((tq, LANES), F32))
        return _row_all_lanes(jnp.sum, accv)

    def count_ge(cand):
        return count(lambda x, c: x >= c, cand)

    kf = float(topk)
    zero = jnp.zeros((tq, LANES), jnp.int32)
    c_nonneg = count_ge(zero)
    c_pos = count_ge(zero + 1)
    lo0 = jnp.where(c_nonneg >= kf, 0, INT_MIN)
    clo0 = jnp.where(c_nonneg >= kf, c_nonneg, (nkb * tk).astype(F32))
    settled = jnp.where(c_nonneg >= kf, jnp.where(c_pos < kf, 1, 0), 0)

    def sweep(b, lo, clo):
        cand = lo + lax.shift_left(jnp.int32(1), b)
        cc = count_ge(cand)
        ok = cc >= kf
        return jnp.where(ok, cand, lo), jnp.where(ok, cc, clo)

    lo1, clo1 = lax.fori_loop(0, 30 - EXIT_CHECK_BIT, lambda j, c: sweep(30 - j, *c), (lo0, clo0))

    def search_cond(carry):
        b, open_rows, _, _ = carry
        return jnp.logical_and(b >= 0, open_rows > 0)

    def search_body(carry):
        b, _, lo, clo = carry
        lo, clo = sweep(b, lo, clo)
        open_rows = jnp.max(jnp.where(clo == kf, 0, jnp.where(settled > 0, 0, 1)))
        return b - 1, open_rows, lo, clo

    _, _, thr, cnt_thr = lax.while_loop(search_cond, search_body,
                                        (jnp.int32(EXIT_CHECK_BIT), jnp.int32(1), lo1, clo1))
    has_tie = jnp.max(cnt_thr) > kf
    thr_t = jnp.tile(thr, (1, tk // LANES))

    q = q_ref[0]
    qlane = lax.broadcasted_iota(jnp.int32, (tq, LANES), 1)

    def scores(h, kb, bias):
        ps = slice((h // 2) * LANES, (h // 2 + 1) * LANES)
        qp = q[:, ps]
        qh = jnp.where((qlane < HEAD_DIM) if h % 2 == 0 else (qlane >= HEAD_DIM),
                       qp, jnp.zeros_like(qp))
        k0 = pl.multiple_of(kb * tk, tk)
        return _dot_t(qh, k_ref[0, pl.ds(k0, tk), ps]) + bias

    def max_step(kb, bias):
        key_ref[kb] = pltpu.bitcast(bias, jnp.int32)
        for h in range(B_HEADS):
            m_ref[h] = _fold(jnp.maximum, m_ref[h], scores(h, kb, bias))

    m_ref[...] = jnp.full(m_ref.shape, NEG, F32)

    def plain_select():
        def body(kb, carry):
            adm_bias = jnp.where(admissible(kb), 0.0, NEG)
            max_step(kb, jnp.where(key_ref[kb] >= thr_t, adm_bias, NEG))
            return carry
        lax.fori_loop(0, nkb, body, 0)

    def tie_select():
        need = jnp.tile(kf - count(lambda x, c: x > c, thr), (1, tk // LANES))

        def body(kb, eq_before):
            keyb = key_ref[kb]
            eq = keyb == thr_t
            eqf = jnp.where(eq, 1.0, 0.0)
            rank = jnp.tile(eq_before, (1, tk // LANES)) + _dot(eqf.astype(BF16), tri_ref[...])
            take = jnp.where(keyb > thr_t, 1, jnp.where(eq, jnp.where(rank < need, 1, 0), 0))
            adm_bias = jnp.where(admissible(kb), 0.0, NEG)
            max_step(kb, jnp.where(take > 0, adm_bias, NEG))
            return eq_before + _row_all_lanes(jnp.sum, _fold(jnp.add, jnp.zeros((tq, LANES), F32), eqf))
        lax.fori_loop(0, nkb, body, jnp.zeros((tq, LANES), F32))

    lax.cond(has_tie, tie_select, plain_select)

    for h in range(B_HEADS):
        m_ref[h] = _row_all_lanes(jnp.max, m_ref[h])
    l_ref[...] = jnp.zeros(l_ref.shape, F32)
    acc_ref[...] = jnp.zeros(acc_ref.shape, F32)

    def acc_body(kb, carry):
        bias = pltpu.bitcast(key_ref[kb], F32)
        k0 = pl.multiple_of(kb * tk, tk)
        for h in range(B_HEADS):
            ps = slice((h // 2) * LANES, (h // 2 + 1) * LANES)
            p = jnp.exp2(scores(h, kb, bias) - jnp.tile(m_ref[h], (1, tk // LANES)))
            l_ref[h] = _fold(jnp.add, l_ref[h], p)
            acc_ref[h] += _dot(p.astype(BF16), v_ref[0, pl.ds(k0, tk), ps])
        return carry

    lax.fori_loop(0, nkb, acc_body, 0)
    _write_heads(o_ref, acc_ref, l_ref, B_HEADS)


def _dsa(iq, iw, ik, q, k, v, *, tq=512, tk=512):
    bsz, seq, _ = q.shape
    assert seq % tq == 0 and seq % tk == 0
    topk = min(TOPK_MAX, seq // 4)
    tri = (np.arange(tk)[:, None] < np.arange(tk)[None, :]).astype(np.float32)
    tri = jnp.asarray(tri, BF16)
    qmap = lambda b, i: (b, i, 0)
    full = lambda b, i: (b, 0, 0)
    stat = pltpu.VMEM((B_HEADS, tq, LANES), F32)
    return pl.pallas_call(
        functools.partial(_dsa_kernel, tq=tq, tk=tk, topk=topk),
        grid=(bsz, seq // tq),
        in_specs=[pl.BlockSpec((1, tq, IQ_W), qmap),
                  pl.BlockSpec((1, tq, IDX_HEADS), qmap),
                  _resident((1, seq, IQ_W), full),
                  pl.BlockSpec((1, tq, B_WP), qmap),
                  _resident((1, seq, B_WP), full),
                  _resident((1, seq, B_WP), full),
                  pl.BlockSpec((tk, tk), lambda b, i: (0, 0))],
        out_specs=pl.BlockSpec((1, tq, B_WP), qmap),
        out_shape=jax.ShapeDtypeStruct((bsz, seq, B_WP), BF16),
        scratch_shapes=[pltpu.VMEM((seq // tk, tq, tk), jnp.int32),
                        pltpu.VMEM((IDX_HEADS * tq, IQ_W), BF16),
                        pltpu.VMEM((tq, LANES), jnp.int32),
                        stat, stat, stat],
        compiler_params=pltpu.CompilerParams(
            dimension_semantics=("arbitrary", "arbitrary"), vmem_limit_bytes=VMEM_LIMIT),
        name="dsa",
    )(iq, iw, ik, q, k, v, tri)


def _rot_cols(w, d):
    k, n = w.shape
    w3 = w.reshape(k, n // d, d)
    return jnp.concatenate([-w3[..., d // 2:], w3[..., :d // 2]], axis=-1).reshape(k, n)


def _pad_cols(w, n):
    return jnp.pad(w, ((0, 0), (0, n - w.shape[1])))


def _prep_proj_weights(w_in, c_w_uq, c_w_ukv):
    offs = np.cumsum([0, A_W, A_W, A_W, B_W, B_W, B_W, IQ_W, IDX_DIM, IDX_HEADS,
                      C_Q_RANK, C_KV_RANK, C_ROPE])
    (aq, ak, av, bq, bk, bv, iq, ik, iw, cq, ckv, ckr) = [
        w_in[:, offs[j]:offs[j + 1]] for j in range(12)]
    ascale = HEAD_DIM ** -0.5
    bscale = HEAD_DIM ** -0.5 * LOG2E
    zeros64 = jnp.zeros((D_MODEL, 64), F32)
    zeros32 = jnp.zeros((D_MODEL, 32), F32)
    kr_group = jnp.concatenate([zeros64, ckr, zeros32], axis=1)
    kr_group_rot = jnp.concatenate([zeros64, _rot_cols(ckr, C_ROPE), zeros32], axis=1)
    plain = [aq * ascale, ak, av, _pad_cols(bv, B_WP), cq, ckv,
             _pad_cols(iw * (IDX_HEADS * IDX_DIM) ** -0.5, 128)]
    rope_main = [_pad_cols(bq * bscale, B_WP), _pad_cols(bk, B_WP), iq, jnp.tile(ik, (1, IDX_HEADS)),
                 kr_group]
    rope_rot = [_pad_cols(_rot_cols(bq, HEAD_DIM) * bscale, B_WP), _pad_cols(_rot_cols(bk, HEAD_DIM), B_WP),
                _rot_cols(iq, IDX_DIM), jnp.tile(_rot_cols(ik, IDX_DIM), (1, IDX_HEADS)), kr_group_rot]
    wcat = jnp.concatenate(plain + rope_main + rope_rot, axis=1).astype(BF16)

    cscale = (C_NOPE + C_ROPE) ** -0.5 * LOG2E
    uq = c_w_uq.reshape(C_Q_RANK, C_HEADS, C_NOPE + C_ROPE) * cscale
    zq = jnp.zeros((C_Q_RANK, C_HEADS, 32), F32)
    uq_main = jnp.concatenate([uq, zq], axis=-1).reshape(C_Q_RANK, C_QW)
    uq_rot_r = jnp.concatenate([-uq[..., C_NOPE + C_ROPE // 2:], uq[..., C_NOPE:C_NOPE + C_ROPE // 2]],
                               axis=-1)
    uq_rot = jnp.concatenate([jnp.zeros((C_Q_RANK, C_HEADS, C_NOPE), F32), uq_rot_r, zq],
                             axis=-1).reshape(C_Q_RANK, C_QW)
    wuq = jnp.concatenate([uq_main, uq_rot], axis=1).astype(BF16)
    ukv = c_w_ukv.reshape(C_KV_RANK, C_HEADS, C_NOPE + C_V)
    wukn = jnp.concatenate([ukv[..., :C_NOPE], jnp.zeros((C_KV_RANK, C_HEADS, 64), F32)],
                           axis=-1).reshape(C_KV_RANK, C_QW).astype(BF16)
    wuv = _pad_cols(ukv[..., C_NOPE:].reshape(C_KV_RANK, C_HEADS * C_V), 384).astype(BF16)
    return wcat, wuq, wukn, wuv


def _rope_tables(seq):
    pos = jnp.arange(seq, dtype=F32)[:, None]

    def cs(d):
        inv = ROPE_THETA ** (-jnp.arange(0, d, 2, dtype=F32) / d)
        ang = pos * inv[None, :]
        c, s = jnp.cos(ang), jnp.sin(ang)
        return (jnp.tile(jnp.concatenate([c, c], axis=1), (1, LANES // d)),
                jnp.tile(jnp.concatenate([s, s], axis=1), (1, LANES // d)))

    c64, s64 = cs(HEAD_DIM)
    c32, s32 = cs(C_ROPE)
    lane = jnp.arange(LANES)[None, :]
    roped = (lane >= C_NOPE) & (lane < C_NOPE + C_ROPE)
    cc = jnp.where(roped, c32, 1.0)
    sc = jnp.where(roped, s32, 0.0)
    return jnp.concatenate([c64, s64, c32, s32, cc, sc], axis=1)


def _chunk_bias(rel_bias):
    rb = rel_bias.astype(F32)
    n_rel = A_QB + A_WIN - 1
    below = jnp.broadcast_to(rb[:, :1], (A_HEADS, A_QB - 1 - (CHUNK - 1)))
    above = jnp.broadcast_to(rb[:, -1:], (A_HEADS, A_PREV + A_QB - 1 - A_REL_MAX))
    by_rel_desc = jnp.concatenate([below, rb, above], axis=1)[:, ::-1]
    assert by_rel_desc.shape[1] == n_rel
    v = jnp.roll(by_rel_desc, -(A_QB - 1), axis=1)
    flat = jnp.broadcast_to(v[:, None, :], (A_HEADS, A_QB, n_rel)).reshape(A_HEADS, A_QB * n_rel)
    toeplitz = flat[:, :A_QB * (n_rel - 1)].reshape(A_HEADS, A_QB, n_rel - 1)[:, :, :A_WIN]
    r = np.arange(A_QB)
    ki = np.arange(A_WIN)[None, :] - CHUNK * (r // CHUNK)[:, None]
    inwin = (ki >= 0) & (ki < (A_LEFT_CHUNKS + 1) * CHUNK)
    b = jnp.where(jnp.asarray(inwin)[None], toeplitz, NEG)
    return b.reshape(A_HEADS // 2, 2 * A_QB, A_WIN)


def _prep_wo(w_out):
    pad64 = jnp.zeros((64, D_MODEL), F32)
    return jnp.concatenate([w_out[:A_W], w_out[A_W:A_W + B_W], pad64,
                            w_out[A_W + B_W:], pad64], axis=0).astype(BF16)


def kernel(x, ffn1_norm, ffn1_w_gate, ffn1_w_up, ffn1_w_down, mix_norm, w_in, a_rel_bias, c_q_norm,
           c_kv_norm, c_w_uq, c_w_ukv, w_out, ffn2_norm, ffn2_w_gate, ffn2_w_up, ffn2_w_down,
           final_norm):
    bsz, seq, _ = x.shape
    depth = w_in.shape[0]
    tab = _rope_tables(seq)
    f1g, f1u, f1d = ffn1_w_gate.astype(BF16), ffn1_w_up.astype(BF16), ffn1_w_down.astype(BF16)
    f2g, f2u, f2d = ffn2_w_gate.astype(BF16), ffn2_w_up.astype(BF16), ffn2_w_down.astype(BF16)
    wcat, wuq, wukn, wuv = jax.vmap(_prep_proj_weights)(w_in, c_w_uq, c_w_ukv)
    abias = jax.vmap(_chunk_bias)(a_rel_bias)
    wo = jax.vmap(_prep_wo)(w_out)
    norms = [n[:, None, :] for n in (ffn1_norm, mix_norm, c_q_norm, c_kv_norm, ffn2_norm)]
    n_ffn1, n_mix, n_cq, n_ckv, n_ffn2 = norms
    x2 = x.reshape(bsz * seq, D_MODEL)
    for l in range(depth):
        x2 = _ffn(x2, None, None, n_ffn1, f1g, f1u, f1d, None, l)
        (aq, ak, av, bq, bk, bv, iq, ik, iw, cq, ck, cv) = _proj(
            x2, bsz, seq, n_mix, wcat, tab, n_cq, n_ckv, wuq, wukn, wuv, l)
        o_a = _chunk_attn(aq, ak, av, abias, l)
        o_b = _dsa(iq, iw, ik, bq, bk, bv)
        o_c = _mla(cq, ck, cv)
        x2 = _ffn(x2, (o_a, o_b, o_c), wo, n_ffn2, f2g, f2u, f2d,
                  final_norm[None] if l == depth - 1 else None, l)
    return x2.reshape(bsz, seq, D_MODEL)
```

```python
import functools
import math

import numpy as np
import jax
import jax.numpy as jnp
from jax import lax
from jax.experimental import pallas as pl
from jax.experimental.pallas import tpu as pltpu

F32 = jnp.float32
BF16 = jnp.bfloat16

D_MODEL = 1024
CHUNK = 64
HEAD_DIM = 64
A_HEADS = 6
A_LEFT_CHUNKS = 8
A_REL_MAX = 128
B_HEADS = 5
IDX_HEADS = 8
IDX_DIM = 32
TOPK_MAX = 256
C_HEADS = 5
C_Q_RANK = 384
C_KV_RANK = 256
C_NOPE = 64
C_ROPE = 32
C_V = 64
D_FF = 2816
ROPE_THETA = 10000.0
EPS = 1e-6
NEG = -1e30
LOG2E = math.log2(math.e)

LANES = 128
A_W = A_HEADS * HEAD_DIM
B_W = B_HEADS * HEAD_DIM
B_WP = 384
C_QW = C_HEADS * LANES
IQ_W = IDX_HEADS * IDX_DIM
VMEM_LIMIT = 56 * 1024 * 1024

OFF_AQ, OFF_AK, OFF_AV, OFF_BV, OFF_CQ, OFF_CKV, OFF_IW = 0, 384, 768, 1152, 1536, 1920, 2176
OFF_ROPE = 2304
ROPE_W = 384 + 384 + 256 + 256 + 128
R_BQ, R_BK, R_IQ, R_IK, R_KR = 0, 384, 768, 1024, 1280
W_CAT = OFF_ROPE + 2 * ROPE_W


INT_MIN = -(2 ** 31)


def _dot(a, b):
    return jnp.dot(a, b, preferred_element_type=F32)


def _dot_t(a, b):
    return lax.dot_general(a, b, (((1,), (1,)), ((), ())), preferred_element_type=F32)


def _rms(x, g):
    ms = jnp.mean(x * x, axis=-1, keepdims=True)
    return x * lax.rsqrt(ms + EPS) * g


def _layer(l, *tail):
    return pl.BlockSpec((None,) + tail, lambda *_: (l,) + (0,) * len(tail))


def _ffn_kernel(*refs, has_attn, has_final, tf):
    it = iter(refs)
    x_ref = next(it)
    if has_attn:
        oa_ref, ob_ref, oc_ref, wo_ref = next(it), next(it), next(it), next(it)
    g_ref, wg_ref, wu_ref, wd_ref = next(it), next(it), next(it), next(it)
    fg_ref = next(it) if has_final else None
    o_ref = next(it)
    acc_ref = next(it)

    x = x_ref[...]
    if has_attn:
        x = x + _dot(oa_ref[0], wo_ref[0:384, :])
        x = x + _dot(ob_ref[0], wo_ref[384:768, :])
        x = x + _dot(oc_ref[0], wo_ref[768:1152, :])
    xn = _rms(x, g_ref[...]).astype(BF16)
    for j in range(D_FF // tf):
        g = _dot(xn, wg_ref[:, j * tf:(j + 1) * tf])
        u = _dot(xn, wu_ref[:, j * tf:(j + 1) * tf])
        h = (g * jax.nn.sigmoid(g) * u).astype(BF16)
        c = _dot(h, wd_ref[j * tf:(j + 1) * tf, :])
        if j == 0:
            acc_ref[...] = c
        else:
            acc_ref[...] += c
    y = x + 0.5 * acc_ref[...]
    if has_final:
        y = _rms(y, fg_ref[...])
    o_ref[...] = y


def _ffn(x2, attn, wo, g, wg, wu, wd, final_g, l, *, tm=512, tf=256):
    n = x2.shape[0]
    has_attn = attn is not None
    has_final = final_g is not None
    const = lambda i: (0, 0)
    row = lambda i: (i, 0)
    in_specs = [pl.BlockSpec((tm, D_MODEL), row)]
    args = [x2]
    if has_attn:
        s = attn[0].shape[1]
        nt = s // tm
        amap = lambda i: (i // nt, i % nt, 0)
        for a in attn:
            in_specs.append(pl.BlockSpec((1, tm, 384), amap))
            args.append(a)
        in_specs.append(_layer(l, 1152, D_MODEL))
        args.append(wo)
    in_specs += [_layer(l, 1, D_MODEL), _layer(l, D_MODEL, D_FF), _layer(l, D_MODEL, D_FF),
                 _layer(l, D_FF, D_MODEL)]
    args += [g, wg, wu, wd]
    if has_final:
        in_specs.append(pl.BlockSpec((1, D_MODEL), const))
        args.append(final_g)
    return pl.pallas_call(
        functools.partial(_ffn_kernel, has_attn=has_attn, has_final=has_final, tf=tf),
        grid=(n // tm,),
        in_specs=in_specs,
        out_specs=pl.BlockSpec((tm, D_MODEL), row),
        out_shape=jax.ShapeDtypeStruct((n, D_MODEL), F32),
        scratch_shapes=[pltpu.VMEM((tm, D_MODEL), F32)],
        compiler_params=pltpu.CompilerParams(
            dimension_semantics=("arbitrary",), vmem_limit_bytes=VMEM_LIMIT),
        name="ffn",
    )(*args)


def _proj_kernel(x_ref, g_ref, w_ref, tab_ref, qn_ref, kvn_ref, wuq_ref, wukn_ref, wuv_ref,
                 aq_ref, ak_ref, av_ref, bq_ref, bk_ref, bv_ref, iq_ref, ik_ref, iw_ref,
                 cq_ref, ck_ref, cv_ref):
    xn = _rms(x_ref[...], g_ref[...]).astype(BF16)

    def mm(off, n):
        return _dot(xn, w_ref[:, off:off + n])

    aq_ref[0] = mm(OFF_AQ, 384).astype(BF16)
    ak_ref[0] = mm(OFF_AK, 384).astype(BF16)
    av_ref[0] = mm(OFF_AV, 384).astype(BF16)
    bv_ref[0] = mm(OFF_BV, 384).astype(BF16)
    iw_ref[0] = mm(OFF_IW, 128)[:, :IDX_HEADS]

    cos64, sin64 = tab_ref[:, 0:128], tab_ref[:, 128:256]
    cos32, sin32 = tab_ref[:, 256:384], tab_ref[:, 384:512]
    cosc, sinc = tab_ref[:, 512:640], tab_ref[:, 640:768]

    def rope_group(r_off, n, cos, sin):
        m = mm(OFF_ROPE + r_off, n)
        s = mm(OFF_ROPE + ROPE_W + r_off, n)
        return [m[:, c * 128:(c + 1) * 128] * cos + s[:, c * 128:(c + 1) * 128] * sin
                for c in range(n // 128)]

    for c, v in enumerate(rope_group(R_BQ, 384, cos64, sin64)):
        bq_ref[0, :, c * 128:(c + 1) * 128] = v.astype(BF16)
    for c, v in enumerate(rope_group(R_BK, 384, cos64, sin64)):
        bk_ref[0, :, c * 128:(c + 1) * 128] = v.astype(BF16)
    for c, v in enumerate(rope_group(R_IQ, 256, cos32, sin32)):
        iq_ref[0, :, c * 128:(c + 1) * 128] = v.astype(BF16)
    for c, v in enumerate(rope_group(R_IK, 256, cos32, sin32)):
        ik_ref[0, :, c * 128:(c + 1) * 128] = v.astype(BF16)
    krg = rope_group(R_KR, 128, cosc, sinc)[0]

    cqn = _rms(mm(OFF_CQ, C_Q_RANK), qn_ref[...]).astype(BF16)
    qm = _dot(cqn, wuq_ref[:, 0:C_QW])
    qs = _dot(cqn, wuq_ref[:, C_QW:2 * C_QW])
    ckn = _rms(mm(OFF_CKV, C_KV_RANK), kvn_ref[...]).astype(BF16)
    kn = _dot(ckn, wukn_ref[...])
    for h in range(C_HEADS):
        sl = slice(h * 128, (h + 1) * 128)
        cq_ref[0, :, sl] = (qm[:, sl] * cosc + qs[:, sl] * sinc).astype(BF16)
        ck_ref[0, :, sl] = (kn[:, sl] + krg).astype(BF16)
    cv_ref[0] = _dot(ckn, wuv_ref[...]).astype(BF16)


def _proj(x2, bsz, seq, g, wcat, tab, qn, kvn, wuq, wukn, wuv, l, *, tm=512):
    nt = seq // tm
    const = lambda i: (0, 0)
    omap = lambda i: (i // nt, i % nt, 0)
    widths = [384, 384, 384, 384, 384, 384, IQ_W, IQ_W, IDX_HEADS, C_QW, C_QW, 384]
    dtypes = [BF16] * 8 + [F32] + [BF16] * 3
    return pl.pallas_call(
        _proj_kernel,
        grid=(bsz * nt,),
        in_specs=[pl.BlockSpec((tm, D_MODEL), lambda i: (i, 0)),
                  _layer(l, 1, D_MODEL),
                  _layer(l, D_MODEL, W_CAT),
                  pl.BlockSpec((tm, 768), lambda i: (i % nt, 0)),
                  _layer(l, 1, C_Q_RANK),
                  _layer(l, 1, C_KV_RANK),
                  _layer(l, C_Q_RANK, 2 * C_QW),
                  _layer(l, C_KV_RANK, C_QW),
                  _layer(l, C_KV_RANK, 384)],
        out_specs=[pl.BlockSpec((1, tm, w), omap) for w in widths],
        out_shape=[jax.ShapeDtypeStruct((bsz, seq, w), d) for w, d in zip(widths, dtypes)],
        compiler_params=pltpu.CompilerParams(
            dimension_semantics=("arbitrary",), vmem_limit_bytes=VMEM_LIMIT),
        name="proj",
    )(x2, g, wcat, tab, qn, kvn, wuq, wukn, wuv)


A_QB = 2 * CHUNK
A_WIN = (A_LEFT_CHUNKS + 2) * CHUNK
A_PREV = A_LEFT_CHUNKS * CHUNK


def _chunk_attn_kernel(q_ref, kp_ref, kc_ref, vp_ref, vc_ref, bias_ref, o_ref, kcat_ref, vcat_ref,
                       *, tq):
    i = pl.program_id(1)
    kcat_ref[0:A_PREV, :] = kp_ref[0]
    kcat_ref[A_PREV:A_PREV + tq, :] = kc_ref[0]
    vcat_ref[0:A_PREV, :] = vp_ref[0]
    vcat_ref[A_PREV:A_PREV + tq, :] = vc_ref[0]
    lane = lax.broadcasted_iota(jnp.int32, (A_QB, LANES), 1)
    col_chunk = lax.broadcasted_iota(jnp.int32, (2 * A_QB, A_WIN), 1) // CHUNK
    for qb in range(tq // A_QB):
        first_chunk = i * (tq // CHUNK) + 2 * qb - A_LEFT_CHUNKS
        valid = (col_chunk + first_chunk) >= 0
        for p in range(A_HEADS // 2):
            ls = slice(p * LANES, (p + 1) * LANES)
            q = q_ref[0, qb * A_QB:(qb + 1) * A_QB, ls]
            qs = jnp.concatenate([jnp.where(lane < HEAD_DIM, q, jnp.zeros_like(q)),
                                  jnp.where(lane >= HEAD_DIM, q, jnp.zeros_like(q))], axis=0)
            kw = kcat_ref[qb * A_QB:qb * A_QB + A_WIN, ls]
            vw = vcat_ref[qb * A_QB:qb * A_QB + A_WIN, ls]
            s = _dot_t(qs, kw) + bias_ref[p]
            s = jnp.where(valid, s, NEG)
            m = jnp.max(s, axis=1, keepdims=True)
            e = jnp.exp(s - m)
            l = jnp.sum(e, axis=1, keepdims=True)
            o = _dot(e.astype(BF16), vw) / l
            o_ref[0, qb * A_QB:(qb + 1) * A_QB, ls] = jnp.where(
                lane < HEAD_DIM, o[0:A_QB], o[A_QB:2 * A_QB]).astype(BF16)


def _chunk_attn(q, k, v, bias, l, *, tq=512):
    bsz, seq, _ = q.shape
    assert tq == A_PREV and seq % tq == 0
    cur = lambda b, i: (b, i, 0)
    prev = lambda b, i: (b, jnp.maximum(i - 1, 0), 0)
    blk = (1, tq, A_W)
    return pl.pallas_call(
        functools.partial(_chunk_attn_kernel, tq=tq),
        grid=(bsz, seq // tq),
        in_specs=[pl.BlockSpec(blk, cur), pl.BlockSpec(blk, prev), pl.BlockSpec(blk, cur),
                  pl.BlockSpec(blk, prev), pl.BlockSpec(blk, cur),
                  _layer(l, A_HEADS // 2, 2 * A_QB, A_WIN)],
        out_specs=pl.BlockSpec(blk, cur),
        out_shape=jax.ShapeDtypeStruct((bsz, seq, A_W), BF16),
        scratch_shapes=[pltpu.VMEM((A_PREV + tq, A_W), BF16), pltpu.VMEM((A_PREV + tq, A_W), BF16)],
        compiler_params=pltpu.CompilerParams(
            dimension_semantics=("arbitrary", "arbitrary"), vmem_limit_bytes=VMEM_LIMIT),
        name="chunk_attn",
    )(q, k, k, v, v, bias)


def _fold(op, acc, s):
    for c in range(s.shape[1] // LANES):
        acc = op(acc, s[:, c * LANES:(c + 1) * LANES])
    return acc


def _row_all_lanes(op, x):
    return jnp.broadcast_to(op(x, axis=1, keepdims=True), x.shape)


def _pair_select(even_head, odd_head):
    lane = lax.broadcasted_iota(jnp.int32, even_head.shape, 1)
    return jnp.where(lane < HEAD_DIM, even_head, odd_head)


def _write_heads(o_ref, acc_ref, l_ref, n_heads):
    for p in range((n_heads + 1) // 2):
        o0 = acc_ref[2 * p] / _row_all_lanes(jnp.sum, l_ref[2 * p])
        if 2 * p + 1 < n_heads:
            o1 = acc_ref[2 * p + 1] / _row_all_lanes(jnp.sum, l_ref[2 * p + 1])
        else:
            o1 = jnp.zeros_like(o0)
        o_ref[0, :, p * LANES:(p + 1) * LANES] = _pair_select(o0, o1).astype(BF16)


def _resident(shape, index_map):
    return pl.BlockSpec(shape, index_map, pipeline_mode=pl.Buffered(1))


def _mla_kernel(q_ref, k_ref, v_ref, o_ref, s0_ref, s1_ref, m_ref, l_ref, acc_ref, *, tq):
    q0 = pl.multiple_of(pl.program_id(1) * tq, tq)
    nfull = q0 // tq
    diag_mask = (lax.broadcasted_iota(jnp.int32, (tq, tq), 1) // CHUNK
                 <= lax.broadcasted_iota(jnp.int32, (tq, tq), 0) // CHUNK)

    def produce(kb, s_ref):
        k0 = pl.multiple_of(kb * tq, tq)
        for h in range(C_HEADS):
            hs = slice(h * LANES, (h + 1) * LANES)
            s_ref[h] = _dot_t(q_ref[0, :, hs], k_ref[0, pl.ds(k0, tq), hs])

    def consume(kb, s_ref, masked):
        k0 = pl.multiple_of(kb * tq, tq)
        for h in range(C_HEADS):
            ps = slice((h // 2) * LANES, (h // 2 + 1) * LANES)
            s = s_ref[h]
            if masked:
                s = jnp.where(diag_mask, s, NEG)
            m_old = m_ref[h]
            m_new = jnp.maximum(m_old, _row_all_lanes(
                jnp.max, _fold(jnp.maximum, s[:, :LANES], s[:, LANES:])))
            alpha = jnp.exp2(m_old - m_new)
            p = jnp.exp2(s - jnp.tile(m_new, (1, tq // LANES)))
            l_ref[h] = alpha * l_ref[h] + _fold(jnp.add, p[:, :LANES], p[:, LANES:])
            acc_ref[h] = alpha * acc_ref[h] + _dot(p.astype(BF16), v_ref[0, pl.ds(k0, tq), ps])
            m_ref[h] = m_new

    m_ref[...] = jnp.full(m_ref.shape, NEG, F32)
    l_ref[...] = jnp.zeros(l_ref.shape, F32)
    acc_ref[...] = jnp.zeros(acc_ref.shape, F32)

    produce(0, s0_ref)

    def body(j, carry):
        produce(2 * j + 1, s1_ref)
        consume(2 * j, s0_ref, False)
        produce(2 * j + 2, s0_ref)
        consume(2 * j + 1, s1_ref, False)
        return carry
    lax.fori_loop(0, nfull // 2, body, 0)

    @pl.when(nfull % 2 == 0)
    def _():
        consume(nfull, s0_ref, True)

    @pl.when(nfull % 2 == 1)
    def _():
        produce(nfull, s1_ref)
        consume(nfull - 1, s0_ref, False)
        consume(nfull, s1_ref, True)
    _write_heads(o_ref, acc_ref, l_ref, C_HEADS)


def _mla(q, k, v, *, tq=512):
    bsz, seq, _ = q.shape
    assert seq % tq == 0
    stat = pltpu.VMEM((C_HEADS, tq, LANES), F32)
    return pl.pallas_call(
        functools.partial(_mla_kernel, tq=tq),
        grid=(bsz, seq // tq),
        in_specs=[pl.BlockSpec((1, tq, C_QW), lambda b, i: (b, i, 0)),
                  _resident((1, seq, C_QW), lambda b, i: (b, 0, 0)),
                  _resident((1, seq, 384), lambda b, i: (b, 0, 0))],
        out_specs=pl.BlockSpec((1, tq, 384), lambda b, i: (b, i, 0)),
        out_shape=jax.ShapeDtypeStruct((bsz, seq, 384), BF16),
        scratch_shapes=[pltpu.VMEM((C_HEADS, tq, tq), F32), pltpu.VMEM((C_HEADS, tq, tq), F32),
                        stat, stat, stat],
        compiler_params=pltpu.CompilerParams(
            dimension_semantics=("arbitrary", "arbitrary"), vmem_limit_bytes=VMEM_LIMIT),
        name="mla",
    )(q, k, v)


GROUP_COLS = 32
I32_ALL = -1


def _i32(x):
    return int(np.uint32(x).view(np.int32))


def _bit_planes(words):
    a = list(reversed(words))
    j, m = 16, 0x0000FFFF
    while j:
        k = 0
        while k < 32:
            t = (a[k] ^ lax.shift_right_logical(a[k + j], jnp.int32(j))) & jnp.int32(_i32(m))
            a[k] = a[k] ^ t
            a[k + j] = a[k + j] ^ lax.shift_left(t, jnp.int32(j))
            k = (k + j + 1) & ~j
        j >>= 1
        m = (m ^ (m << j)) & 0xFFFFFFFF
    return list(reversed(a))


def _dsa_kernel(iq_ref, iw_ref, ik_ref, q_ref, k_ref, v_ref, o_ref,
                key_ref, qm_ref, cand_ref, great_ref, sel_ref, m_ref, l_ref, acc_ref,
                *, tq, tk, topk):
    q0 = pl.program_id(1) * tq
    nkb = (q0 + tq + tk - 1) // tk
    cols_per_blk = tk // LANES
    blks_per_group = GROUP_COLS // cols_per_blk
    n_groups = key_ref.shape[0] // blks_per_group
    row_chunk = (q0 + lax.broadcasted_iota(jnp.int32, (tq, tk), 0)) // CHUNK
    col_chunk0 = lax.broadcasted_iota(jnp.int32, (tq, tk), 1) // CHUNK

    def admissible(kb):
        return (col_chunk0 + kb * (tk // CHUNK)) <= row_chunk

    iq = iq_ref[0]
    head_of_lane = lax.broadcasted_iota(jnp.int32, iq.shape, 1) // IDX_DIM
    for h in range(IDX_HEADS):
        qm_ref[h * tq:(h + 1) * tq, :] = jnp.where(head_of_lane == h, iq, jnp.zeros_like(iq))
    w = iw_ref[0]

    sub = 256

    def score_body(kb, carry):
        adm = admissible(kb)
        for c in range(tk // sub):
            k0 = pl.multiple_of(kb * tk + c * sub, sub)
            d = _dot_t(qm_ref[...], ik_ref[0, pl.ds(k0, sub), :])
            sc = jnp.zeros((tq, sub), F32)
            for h in range(IDX_HEADS):
                sc = sc + jnp.maximum(d[h * tq:(h + 1) * tq], 0.0) * w[:, h:h + 1]
            sc = jnp.where(adm[:, c * sub:(c + 1) * sub], sc, NEG)
            bits = pltpu.bitcast(sc, jnp.int32)
            key_ref[kb, :, c * sub:(c + 1) * sub] = bits ^ ((bits >> 31) | jnp.int32(INT_MIN))
        return carry

    lax.fori_loop(0, nkb, score_body, 0)

    def clear_body(kb, carry):
        key_ref[kb] = jnp.zeros((tq, tk), jnp.int32)
        return carry
    lax.fori_loop(nkb, key_ref.shape[0], clear_body, 0)

    def slot(g, j):
        return g * blks_per_group + j // cols_per_blk, slice((j % cols_per_blk) * LANES,
                                                             (j % cols_per_blk + 1) * LANES)

    for g in range(n_groups):
        @pl.when(nkb > g * blks_per_group)
        def _(g=g):
            def transpose_rows(rg, carry):
                rows = pl.ds(pl.multiple_of(rg * 8, 8), 8)
                cols = [slot(g, j) for j in range(GROUP_COLS)]
                planes = _bit_planes([key_ref[blk, rows, ls] for blk, ls in cols])
                for (blk, ls), p in zip(cols, planes):
                    key_ref[blk, rows, ls] = p
                return carry
            lax.fori_loop(0, tq // 8, transpose_rows, 0)

    for g in range(n_groups):
        n_cols = jnp.clip((nkb - g * blks_per_group) * cols_per_blk, 0, GROUP_COLS)
        word = jnp.where(n_cols >= GROUP_COLS, jnp.int32(I32_ALL),
                         lax.shift_left(jnp.int32(1), jnp.minimum(n_cols, GROUP_COLS - 1)) - 1)
        cand_ref[g] = jnp.full((tq, LANES), word, jnp.int32)
        great_ref[g] = jnp.zeros((tq, LANES), jnp.int32)

    def refine(plane_of, need, prefer_set):
        picked = []
        for g in range(n_groups):
            p = plane_of(g)
            picked.append(cand_ref[g] & (p if prefer_set else ~p))
        cnt = picked[0] * 0
        for x in picked:
            cnt = cnt + lax.population_count(x)
        c = _row_all_lanes(jnp.sum, cnt.astype(F32))
        keep = c >= need
        for g in range(n_groups):
            great_ref[g] = jnp.where(keep, great_ref[g], great_ref[g] | picked[g])
            cand_ref[g] = jnp.where(keep, picked[g], cand_ref[g] ^ picked[g])
        return jnp.where(keep, need, need - c)

    def value_bits(i, need):
        blk = blks_per_group - 1 - i
        for c in reversed(range(cols_per_blk)):
            ls = slice(c * LANES, (c + 1) * LANES)
            need = refine(lambda g: key_ref[g * blks_per_group + blk, :, ls], need, True)
        return need

    need = lax.fori_loop(0, blks_per_group, value_bits, jnp.full((tq, LANES), float(topk), F32))

    def index_bits(need):
        lane = lax.broadcasted_iota(jnp.int32, (tq, LANES), 1)
        col_bit_words = (0xAAAAAAAA, 0xCCCCCCCC, 0xF0F0F0F0, 0xFF00FF00, 0xFFFF0000)
        n_group_bits = max(n_groups - 1, 0).bit_length()
        for t in reversed(range(n_group_bits)):
            need = refine(lambda g: jnp.full((tq, LANES), I32_ALL if (g >> t) & 1 else 0, jnp.int32),
                          need, False)
        for t in reversed(range(5)):
            need = refine(lambda g: jnp.full((tq, LANES), _i32(col_bit_words[t]), jnp.int32),
                          need, False)
        for t in reversed(range(7)):
            need = refine(lambda g: jnp.where((lane >> t) & 1 == 1, jnp.int32(I32_ALL), 0),
                          need, False)
        return need

    n_cand = cand_ref[0] * 0
    for g in range(n_groups):
        n_cand = n_cand + lax.population_count(cand_ref[g])
    has_tie = jnp.max(jnp.where(_row_all_lanes(jnp.sum, n_cand.astype(F32)) > need, 1, 0)) > 0
    lax.cond(has_tie, index_bits, lambda need: need, need)
    for g in range(n_groups):
        sel_ref[g] = great_ref[g] | cand_ref[g]

    q = q_ref[0]
    qlane = lax.broadcasted_iota(jnp.int32, (tq, LANES), 1)

    def scores(h, kb, bias):
        ps = slice((h // 2) * LANES, (h // 2 + 1) * LANES)
        qp = q[:, ps]
        qh = jnp.where((qlane < HEAD_DIM) if h % 2 == 0 else (qlane >= HEAD_DIM),
                       qp, jnp.zeros_like(qp))
        k0 = pl.multiple_of(kb * tk, tk)
        return _dot_t(qh, k_ref[0, pl.ds(k0, tk), ps]) + bias

    m_ref[...] = jnp.full(m_ref.shape, NEG, F32)

    def max_body(kb, carry):
        words = sel_ref[kb // blks_per_group]
        first_col = (kb % blks_per_group) * cols_per_blk
        taken = jnp.concatenate(
            [lax.shift_right_logical(words, jnp.full(words.shape, first_col + c, jnp.int32)) & 1
             for c in range(cols_per_blk)], axis=1)
        bias = jnp.where(taken == 1, jnp.where(admissible(kb), 0.0, NEG), NEG)
        key_ref[kb] = pltpu.bitcast(bias, jnp.int32)
        for h in range(B_HEADS):
            m_ref[h] = _fold(jnp.maximum, m_ref[h], scores(h, kb, bias))
        return carry

    lax.fori_loop(0, nkb, max_body, 0)

    for h in range(B_HEADS):
        m_ref[h] = _row_all_lanes(jnp.max, m_ref[h])
    l_ref[...] = jnp.zeros(l_ref.shape, F32)
    acc_ref[...] = jnp.zeros(acc_ref.shape, F32)

    def acc_body(kb, carry):
        bias = pltpu.bitcast(key_ref[kb], F32)
        k0 = pl.multiple_of(kb * tk, tk)
        for h in range(B_HEADS):
            ps = slice((h // 2) * LANES, (h // 2 + 1) * LANES)
            p = jnp.exp2(scores(h, kb, bias) - jnp.tile(m_ref[h], (1, tk // LANES)))
            l_ref[h] = _fold(jnp.add, l_ref[h], p)
            acc_ref[h] += _dot(p.astype(BF16), v_ref[0, pl.ds(k0, tk), ps])
        return carry

    lax.fori_loop(0, nkb, acc_body, 0)
    _write_heads(o_ref, acc_ref, l_ref, B_HEADS)


def _dsa(iq, iw, ik, q, k, v, *, tq=512, tk=512):
    bsz, seq, _ = q.shape
    assert seq % tq == 0 and seq % tk == 0 and GROUP_COLS % (tk // LANES) == 0
    topk = min(TOPK_MAX, seq // 4)
    blks_per_group = GROUP_COLS // (tk // LANES)
    n_groups = -(-(seq // tk) // blks_per_group)
    qmap = lambda b, i: (b, i, 0)
    full = lambda b, i: (b, 0, 0)
    stat = pltpu.VMEM((B_HEADS, tq, LANES), F32)
    words = pltpu.VMEM((n_groups, tq, LANES), jnp.int32)
    return pl.pallas_call(
        functools.partial(_dsa_kernel, tq=tq, tk=tk, topk=topk),
        grid=(bsz, seq // tq),
        in_specs=[pl.BlockSpec((1, tq, IQ_W), qmap),
                  pl.BlockSpec((1, tq, IDX_HEADS), qmap),
                  _resident((1, seq, IQ_W), full),
                  pl.BlockSpec((1, tq, B_WP), qmap),
                  _resident((1, seq, B_WP), full),
                  _resident((1, seq, B_WP), full)],
        out_specs=pl.BlockSpec((1, tq, B_WP), qmap),
        out_shape=jax.ShapeDtypeStruct((bsz, seq, B_WP), BF16),
        scratch_shapes=[pltpu.VMEM((n_groups * blks_per_group, tq, tk), jnp.int32),
                        pltpu.VMEM((IDX_HEADS * tq, IQ_W), BF16),
                        words, words, words,
                        stat, stat, stat],
        compiler_params=pltpu.CompilerParams(
            dimension_semantics=("arbitrary", "arbitrary"), vmem_limit_bytes=VMEM_LIMIT),
        name="dsa",
    )(iq, iw, ik, q, k, v)


def _rot_cols(w, d):
    k, n = w.shape
    w3 = w.reshape(k, n // d, d)
    return jnp.concatenate([-w3[..., d // 2:], w3[..., :d // 2]], axis=-1).reshape(k, n)


def _pad_cols(w, n):
    return jnp.pad(w, ((0, 0), (0, n - w.shape[1])))


def _prep_proj_weights(w_in, c_w_uq, c_w_ukv):
    offs = np.cumsum([0, A_W, A_W, A_W, B_W, B_W, B_W, IQ_W, IDX_DIM, IDX_HEADS,
                      C_Q_RANK, C_KV_RANK, C_ROPE])
    (aq, ak, av, bq, bk, bv, iq, ik, iw, cq, ckv, ckr) = [
        w_in[:, offs[j]:offs[j + 1]] for j in range(12)]
    ascale = HEAD_DIM ** -0.5
    bscale = HEAD_DIM ** -0.5 * LOG2E
    zeros64 = jnp.zeros((D_MODEL, 64), F32)
    zeros32 = jnp.zeros((D_MODEL, 32), F32)
    kr_group = jnp.concatenate([zeros64, ckr, zeros32], axis=1)
    kr_group_rot = jnp.concatenate([zeros64, _rot_cols(ckr, C_ROPE), zeros32], axis=1)
    plain = [aq * ascale, ak, av, _pad_cols(bv, B_WP), cq, ckv,
             _pad_cols(iw * (IDX_HEADS * IDX_DIM) ** -0.5, 128)]
    rope_main = [_pad_cols(bq * bscale, B_WP), _pad_cols(bk, B_WP), iq, jnp.tile(ik, (1, IDX_HEADS)),
                 kr_group]
    rope_rot = [_pad_cols(_rot_cols(bq, HEAD_DIM) * bscale, B_WP), _pad_cols(_rot_cols(bk, HEAD_DIM), B_WP),
                _rot_cols(iq, IDX_DIM), jnp.tile(_rot_cols(ik, IDX_DIM), (1, IDX_HEADS)), kr_group_rot]
    wcat = jnp.concatenate(plain + rope_main + rope_rot, axis=1).astype(BF16)

    cscale = (C_NOPE + C_ROPE) ** -0.5 * LOG2E
    uq = c_w_uq.reshape(C_Q_RANK, C_HEADS, C_NOPE + C_ROPE) * cscale
    zq = jnp.zeros((C_Q_RANK, C_HEADS, 32), F32)
    uq_main = jnp.concatenate([uq, zq], axis=-1).reshape(C_Q_RANK, C_QW)
    uq_rot_r = jnp.concatenate([-uq[..., C_NOPE + C_ROPE // 2:], uq[..., C_NOPE:C_NOPE + C_ROPE // 2]],
                               axis=-1)
    uq_rot = jnp.concatenate([jnp.zeros((C_Q_RANK, C_HEADS, C_NOPE), F32), uq_rot_r, zq],
                             axis=-1).reshape(C_Q_RANK, C_QW)
    wuq = jnp.concatenate([uq_main, uq_rot], axis=1).astype(BF16)
    ukv = c_w_ukv.reshape(C_KV_RANK, C_HEADS, C_NOPE + C_V)
    wukn = jnp.concatenate([ukv[..., :C_NOPE], jnp.zeros((C_KV_RANK, C_HEADS, 64), F32)],
                           axis=-1).reshape(C_KV_RANK, C_QW).astype(BF16)
    wuv = _pad_cols(ukv[..., C_NOPE:].reshape(C_KV_RANK, C_HEADS * C_V), 384).astype(BF16)
    return wcat, wuq, wukn, wuv


def _rope_tables(seq):
    pos = jnp.arange(seq, dtype=F32)[:, None]

    def cs(d):
        inv = ROPE_THETA ** (-jnp.arange(0, d, 2, dtype=F32) / d)
        ang = pos * inv[None, :]
        c, s = jnp.cos(ang), jnp.sin(ang)
        return (jnp.tile(jnp.concatenate([c, c], axis=1), (1, LANES // d)),
                jnp.tile(jnp.concatenate([s, s], axis=1), (1, LANES // d)))

    c64, s64 = cs(HEAD_DIM)
    c32, s32 = cs(C_ROPE)
    lane = jnp.arange(LANES)[None, :]
    roped = (lane >= C_NOPE) & (lane < C_NOPE + C_ROPE)
    cc = jnp.where(roped, c32, 1.0)
    sc = jnp.where(roped, s32, 0.0)
    return jnp.concatenate([c64, s64, c32, s32, cc, sc], axis=1)


def _chunk_bias(rel_bias):
    rb = rel_bias.astype(F32)
    n_rel = A_QB + A_WIN - 1
    below = jnp.broadcast_to(rb[:, :1], (A_HEADS, A_QB - 1 - (CHUNK - 1)))
    above = jnp.broadcast_to(rb[:, -1:], (A_HEADS, A_PREV + A_QB - 1 - A_REL_MAX))
    by_rel_desc = jnp.concatenate([below, rb, above], axis=1)[:, ::-1]
    assert by_rel_desc.shape[1] == n_rel
    v = jnp.roll(by_rel_desc, -(A_QB - 1), axis=1)
    flat = jnp.broadcast_to(v[:, None, :], (A_HEADS, A_QB, n_rel)).reshape(A_HEADS, A_QB * n_rel)
    toeplitz = flat[:, :A_QB * (n_rel - 1)].reshape(A_HEADS, A_QB, n_rel - 1)[:, :, :A_WIN]
    r = np.arange(A_QB)
    ki = np.arange(A_WIN)[None, :] - CHUNK * (r // CHUNK)[:, None]
    inwin = (ki >= 0) & (ki < (A_LEFT_CHUNKS + 1) * CHUNK)
    b = jnp.where(jnp.asarray(inwin)[None], toeplitz, NEG)
    return b.reshape(A_HEADS // 2, 2 * A_QB, A_WIN)


def _prep_wo(w_out):
    pad64 = jnp.zeros((64, D_MODEL), F32)
    return jnp.concatenate([w_out[:A_W], w_out[A_W:A_W + B_W], pad64,
                            w_out[A_W + B_W:], pad64], axis=0).astype(BF16)


def kernel(x, ffn1_norm, ffn1_w_gate, ffn1_w_up, ffn1_w_down, mix_norm, w_in, a_rel_bias, c_q_norm,
           c_kv_norm, c_w_uq, c_w_ukv, w_out, ffn2_norm, ffn2_w_gate, ffn2_w_up, ffn2_w_down,
           final_norm):
    bsz, seq, _ = x.shape
    depth = w_in.shape[0]
    tab = _rope_tables(seq)
    f1g, f1u, f1d = ffn1_w_gate.astype(BF16), ffn1_w_up.astype(BF16), ffn1_w_down.astype(BF16)
    f2g, f2u, f2d = ffn2_w_gate.astype(BF16), ffn2_w_up.astype(BF16), ffn2_w_down.astype(BF16)
    wcat, wuq, wukn, wuv = jax.vmap(_prep_proj_weights)(w_in, c_w_uq, c_w_ukv)
    abias = jax.vmap(_chunk_bias)(a_rel_bias)
    wo = jax.vmap(_prep_wo)(w_out)
    norms = [n[:, None, :] for n in (ffn1_norm, mix_norm, c_q_norm, c_kv_norm, ffn2_norm)]
    n_ffn1, n_mix, n_cq, n_ckv, n_ffn2 = norms
    x2 = x.reshape(bsz * seq, D_MODEL)
    for l in range(depth):
        x2 = _ffn(x2, None, None, n_ffn1, f1g, f1u, f1d, None, l)
        (aq, ak, av, bq, bk, bv, iq, ik, iw, cq, ck, cv) = _proj(
            x2, bsz, seq, n_mix, wcat, tab, n_cq, n_ckv, wuq, wukn, wuv, l)
        o_a = _chunk_attn(aq, ak, av, abias, l)
        o_b = _dsa(iq, iw, ik, bq, bk, bv)
        o_c = _mla(cq, ck, cv)
        x2 = _ffn(x2, (o_a, o_b, o_c), wo, n_ffn2, f2g, f2u, f2d,
                  final_norm[None] if l == depth - 1 else None, l)
    return x2.reshape(bsz, seq, D_MODEL)
```

```python
import functools
import math

import numpy as np
import jax
import jax.numpy as jnp
from jax import lax
from jax.experimental import pallas as pl
from jax.experimental.pallas import tpu as pltpu

F32 = jnp.float32
BF16 = jnp.bfloat16

D_MODEL = 1024
CHUNK = 64
HEAD_DIM = 64
A_HEADS = 6
A_LEFT_CHUNKS = 8
A_REL_MAX = 128
B_HEADS = 5
IDX_HEADS = 8
IDX_DIM = 32
TOPK_MAX = 256
C_HEADS = 5
C_Q_RANK = 384
C_KV_RANK = 256
C_NOPE = 64
C_ROPE = 32
C_V = 64
D_FF = 2816
ROPE_THETA = 10000.0
EPS = 1e-6
NEG = -1e30
LOG2E = math.log2(math.e)

LANES = 128
A_W = A_HEADS * HEAD_DIM
B_W = B_HEADS * HEAD_DIM
B_WP = 384
C_QW = C_HEADS * LANES
IQ_W = IDX_HEADS * IDX_DIM
VMEM_LIMIT = 56 * 1024 * 1024

OFF_AQ, OFF_AK, OFF_AV, OFF_BV, OFF_CQ, OFF_CKV, OFF_IW = 0, 384, 768, 1152, 1536, 1920, 2176
OFF_ROPE = 2304
ROPE_W = 384 + 384 + 256 + 256 + 128
R_BQ, R_BK, R_IQ, R_IK, R_KR = 0, 384, 768, 1024, 1280
W_CAT = OFF_ROPE + ROPE_W


INT_MIN = -(2 ** 31)


def _dot(a, b):
    return jnp.dot(a, b, preferred_element_type=F32)


def _dot_t(a, b):
    return lax.dot_general(a, b, (((1,), (1,)), ((), ())), preferred_element_type=F32)


def _rms(x, g):
    ms = jnp.mean(x * x, axis=-1, keepdims=True)
    return x * lax.rsqrt(ms + EPS) * g


def _layer(l, *tail):
    return pl.BlockSpec((None,) + tail, lambda *_: (l,) + (0,) * len(tail))


def _ffn_kernel(*refs, has_attn, has_final, tf):
    it = iter(refs)
    x_ref = next(it)
    if has_attn:
        oa_ref, ob_ref, oc_ref, wo_ref = next(it), next(it), next(it), next(it)
    g_ref, wg_ref, wu_ref, wd_ref = next(it), next(it), next(it), next(it)
    fg_ref = next(it) if has_final else None
    o_ref = next(it)
    acc_ref = next(it)

    x = x_ref[...]
    if has_attn:
        x = x + _dot(oa_ref[0], wo_ref[0:384, :])
        x = x + _dot(ob_ref[0], wo_ref[384:768, :])
        x = x + _dot(oc_ref[0], wo_ref[768:1152, :])
    xn = _rms(x, g_ref[...]).astype(BF16)
    for j in range(D_FF // tf):
        g = _dot(xn, wg_ref[:, j * tf:(j + 1) * tf])
        u = _dot(xn, wu_ref[:, j * tf:(j + 1) * tf])
        h = (g * jax.nn.sigmoid(g) * u).astype(BF16)
        c = _dot(h, wd_ref[j * tf:(j + 1) * tf, :])
        if j == 0:
            acc_ref[...] = c
        else:
            acc_ref[...] += c
    y = x + 0.5 * acc_ref[...]
    if has_final:
        y = _rms(y, fg_ref[...])
    o_ref[...] = y


def _ffn(x2, attn, wo, g, wg, wu, wd, final_g, l, *, tm=512, tf=256):
    n = x2.shape[0]
    has_attn = attn is not None
    has_final = final_g is not None
    const = lambda i: (0, 0)
    row = lambda i: (i, 0)
    in_specs = [pl.BlockSpec((tm, D_MODEL), row)]
    args = [x2]
    if has_attn:
        s = attn[0].shape[1]
        nt = s // tm
        amap = lambda i: (i // nt, i % nt, 0)
        for a in attn:
            in_specs.append(pl.BlockSpec((1, tm, 384), amap))
            args.append(a)
        in_specs.append(_layer(l, 1152, D_MODEL))
        args.append(wo)
    in_specs += [_layer(l, 1, D_MODEL), _layer(l, D_MODEL, D_FF), _layer(l, D_MODEL, D_FF),
                 _layer(l, D_FF, D_MODEL)]
    args += [g, wg, wu, wd]
    if has_final:
        in_specs.append(pl.BlockSpec((1, D_MODEL), const))
        args.append(final_g)
    return pl.pallas_call(
        functools.partial(_ffn_kernel, has_attn=has_attn, has_final=has_final, tf=tf),
        grid=(n // tm,),
        in_specs=in_specs,
        out_specs=pl.BlockSpec((tm, D_MODEL), row),
        out_shape=jax.ShapeDtypeStruct((n, D_MODEL), F32),
        scratch_shapes=[pltpu.VMEM((tm, D_MODEL), F32)],
        compiler_params=pltpu.CompilerParams(
            dimension_semantics=("arbitrary",), vmem_limit_bytes=VMEM_LIMIT),
        name="ffn",
    )(*args)


def _proj_kernel(x_ref, g_ref, w_ref, tab_ref, qn_ref, kvn_ref, wuq_ref, wukn_ref, wuv_ref,
                 aq_ref, ak_ref, av_ref, bq_ref, bk_ref, bv_ref, iq_ref, ik_ref, iw_ref,
                 cq_ref, ck_ref, cv_ref):
    xn = _rms(x_ref[...], g_ref[...]).astype(BF16)

    def mm(off, n):
        return _dot(xn, w_ref[:, off:off + n])

    aq_ref[0] = mm(OFF_AQ, 384).astype(BF16)
    ak_ref[0] = mm(OFF_AK, 384).astype(BF16)
    av_ref[0] = mm(OFF_AV, 384).astype(BF16)
    bv_ref[0] = mm(OFF_BV, 384).astype(BF16)
    iw_ref[0] = mm(OFF_IW, 128)[:, :IDX_HEADS]

    cos64, sin64 = tab_ref[:, 0:128], tab_ref[:, 128:256]
    cos32, sin32 = tab_ref[:, 256:384], tab_ref[:, 384:512]
    cosc, sinc = tab_ref[:, 512:640], tab_ref[:, 640:768]
    lane = lax.broadcasted_iota(jnp.int32, (x_ref.shape[0], LANES), 1)

    def rope(x, d, cos, sin):
        partner = jnp.where(lane % d < d // 2, pltpu.roll(x, LANES - d // 2, axis=1),
                            pltpu.roll(x, d // 2, axis=1))
        return x * cos + partner * sin

    def rope_group(r_off, n, d, cos, sin):
        m = mm(OFF_ROPE + r_off, n)
        return [rope(m[:, c * 128:(c + 1) * 128], d, cos, sin) for c in range(n // 128)]

    for c, v in enumerate(rope_group(R_BQ, 384, HEAD_DIM, cos64, sin64)):
        bq_ref[0, :, c * 128:(c + 1) * 128] = v.astype(BF16)
    for c, v in enumerate(rope_group(R_BK, 384, HEAD_DIM, cos64, sin64)):
        bk_ref[0, :, c * 128:(c + 1) * 128] = v.astype(BF16)
    for c, v in enumerate(rope_group(R_IQ, 256, IDX_DIM, cos32, sin32)):
        iq_ref[0, :, c * 128:(c + 1) * 128] = v.astype(BF16)
    for c, v in enumerate(rope_group(R_IK, 256, IDX_DIM, cos32, sin32)):
        ik_ref[0, :, c * 128:(c + 1) * 128] = v.astype(BF16)
    krg = rope_group(R_KR, 128, C_ROPE, cosc, sinc)[0]

    cqn = _rms(mm(OFF_CQ, C_Q_RANK), qn_ref[...]).astype(BF16)
    qm = _dot(cqn, wuq_ref[...])
    ckn = _rms(mm(OFF_CKV, C_KV_RANK), kvn_ref[...]).astype(BF16)
    kn = _dot(ckn, wukn_ref[...])
    for h in range(C_HEADS):
        sl = slice(h * 128, (h + 1) * 128)
        cq_ref[0, :, sl] = rope(qm[:, sl], C_ROPE, cosc, sinc).astype(BF16)
        ck_ref[0, :, sl] = (kn[:, sl] + krg).astype(BF16)
    cv_ref[0] = _dot(ckn, wuv_ref[...]).astype(BF16)


def _proj(x2, bsz, seq, g, wcat, tab, qn, kvn, wuq, wukn, wuv, l, *, tm=512):
    nt = seq // tm
    const = lambda i: (0, 0)
    omap = lambda i: (i // nt, i % nt, 0)
    widths = [384, 384, 384, 384, 384, 384, IQ_W, IQ_W, IDX_HEADS, C_QW, C_QW, 384]
    dtypes = [BF16] * 8 + [F32] + [BF16] * 3
    return pl.pallas_call(
        _proj_kernel,
        grid=(bsz * nt,),
        in_specs=[pl.BlockSpec((tm, D_MODEL), lambda i: (i, 0)),
                  _layer(l, 1, D_MODEL),
                  _layer(l, D_MODEL, W_CAT),
                  pl.BlockSpec((tm, 768), lambda i: (i % nt, 0)),
                  _layer(l, 1, C_Q_RANK),
                  _layer(l, 1, C_KV_RANK),
                  _layer(l, C_Q_RANK, C_QW),
                  _layer(l, C_KV_RANK, C_QW),
                  _layer(l, C_KV_RANK, 384)],
        out_specs=[pl.BlockSpec((1, tm, w), omap) for w in widths],
        out_shape=[jax.ShapeDtypeStruct((bsz, seq, w), d) for w, d in zip(widths, dtypes)],
        compiler_params=pltpu.CompilerParams(
            dimension_semantics=("arbitrary",), vmem_limit_bytes=VMEM_LIMIT),
        name="proj",
    )(x2, g, wcat, tab, qn, kvn, wuq, wukn, wuv)


A_QB = 2 * CHUNK
A_WIN = (A_LEFT_CHUNKS + 2) * CHUNK
A_PREV = A_LEFT_CHUNKS * CHUNK


def _chunk_attn_kernel(q_ref, kp_ref, kc_ref, vp_ref, vc_ref, bias_ref, o_ref, kcat_ref, vcat_ref,
                       *, tq):
    i = pl.program_id(1)
    kcat_ref[0:A_PREV, :] = kp_ref[0]
    kcat_ref[A_PREV:A_PREV + tq, :] = kc_ref[0]
    vcat_ref[0:A_PREV, :] = vp_ref[0]
    vcat_ref[A_PREV:A_PREV + tq, :] = vc_ref[0]
    lane = lax.broadcasted_iota(jnp.int32, (A_QB, LANES), 1)
    col_chunk = lax.broadcasted_iota(jnp.int32, (2 * A_QB, A_WIN), 1) // CHUNK
    for qb in range(tq // A_QB):
        first_chunk = i * (tq // CHUNK) + 2 * qb - A_LEFT_CHUNKS
        valid = (col_chunk + first_chunk) >= 0
        for p in range(A_HEADS // 2):
            ls = slice(p * LANES, (p + 1) * LANES)
            q = q_ref[0, qb * A_QB:(qb + 1) * A_QB, ls]
            qs = jnp.concatenate([jnp.where(lane < HEAD_DIM, q, jnp.zeros_like(q)),
                                  jnp.where(lane >= HEAD_DIM, q, jnp.zeros_like(q))], axis=0)
            kw = kcat_ref[qb * A_QB:qb * A_QB + A_WIN, ls]
            vw = vcat_ref[qb * A_QB:qb * A_QB + A_WIN, ls]
            s = _dot_t(qs, kw) + bias_ref[p]
            s = jnp.where(valid, s, NEG)
            m = jnp.max(s, axis=1, keepdims=True)
            e = jnp.exp(s - m)
            l = jnp.sum(e, axis=1, keepdims=True)
            o = _dot(e.astype(BF16), vw) / l
            o_ref[0, qb * A_QB:(qb + 1) * A_QB, ls] = jnp.where(
                lane < HEAD_DIM, o[0:A_QB], o[A_QB:2 * A_QB]).astype(BF16)


def _chunk_attn(q, k, v, bias, l, *, tq=512):
    bsz, seq, _ = q.shape
    assert tq == A_PREV and seq % tq == 0
    cur = lambda b, i: (b, i, 0)
    prev = lambda b, i: (b, jnp.maximum(i - 1, 0), 0)
    blk = (1, tq, A_W)
    return pl.pallas_call(
        functools.partial(_chunk_attn_kernel, tq=tq),
        grid=(bsz, seq // tq),
        in_specs=[pl.BlockSpec(blk, cur), pl.BlockSpec(blk, prev), pl.BlockSpec(blk, cur),
                  pl.BlockSpec(blk, prev), pl.BlockSpec(blk, cur),
                  _layer(l, A_HEADS // 2, 2 * A_QB, A_WIN)],
        out_specs=pl.BlockSpec(blk, cur),
        out_shape=jax.ShapeDtypeStruct((bsz, seq, A_W), BF16),
        scratch_shapes=[pltpu.VMEM((A_PREV + tq, A_W), BF16), pltpu.VMEM((A_PREV + tq, A_W), BF16)],
        compiler_params=pltpu.CompilerParams(
            dimension_semantics=("arbitrary", "arbitrary"), vmem_limit_bytes=VMEM_LIMIT),
        name="chunk_attn",
    )(q, k, k, v, v, bias)


def _fold(op, acc, s):
    for c in range(s.shape[1] // LANES):
        acc = op(acc, s[:, c * LANES:(c + 1) * LANES])
    return acc


def _row_all_lanes(op, x):
    return jnp.broadcast_to(op(x, axis=1, keepdims=True), x.shape)


def _pair_select(even_head, odd_head):
    lane = lax.broadcasted_iota(jnp.int32, even_head.shape, 1)
    return jnp.where(lane < HEAD_DIM, even_head, odd_head)


def _write_heads(o_ref, acc_ref, l_ref, n_heads):
    for p in range((n_heads + 1) // 2):
        o0 = acc_ref[2 * p] / _row_all_lanes(jnp.sum, l_ref[2 * p])
        if 2 * p + 1 < n_heads:
            o1 = acc_ref[2 * p + 1] / _row_all_lanes(jnp.sum, l_ref[2 * p + 1])
        else:
            o1 = jnp.zeros_like(o0)
        o_ref[0, :, p * LANES:(p + 1) * LANES] = _pair_select(o0, o1).astype(BF16)


def _resident(shape, index_map):
    return pl.BlockSpec(shape, index_map, pipeline_mode=pl.Buffered(1))


def _mla_kernel(q_ref, k_ref, v_ref, o_ref, s0_ref, s1_ref, m_ref, l_ref, acc_ref, *, tq):
    q0 = pl.multiple_of(pl.program_id(1) * tq, tq)
    nfull = q0 // tq
    diag_mask = (lax.broadcasted_iota(jnp.int32, (tq, tq), 1) // CHUNK
                 <= lax.broadcasted_iota(jnp.int32, (tq, tq), 0) // CHUNK)

    def produce(kb, s_ref):
        k0 = pl.multiple_of(kb * tq, tq)
        for h in range(C_HEADS):
            hs = slice(h * LANES, (h + 1) * LANES)
            s_ref[h] = _dot_t(q_ref[0, :, hs], k_ref[0, pl.ds(k0, tq), hs])

    def consume(kb, s_ref, masked):
        k0 = pl.multiple_of(kb * tq, tq)
        for h in range(C_HEADS):
            ps = slice((h // 2) * LANES, (h // 2 + 1) * LANES)
            s = s_ref[h]
            if masked:
                s = jnp.where(diag_mask, s, NEG)
            m_old = m_ref[h]
            m_new = jnp.maximum(m_old, _row_all_lanes(
                jnp.max, _fold(jnp.maximum, s[:, :LANES], s[:, LANES:])))
            alpha = jnp.exp2(m_old - m_new)
            p = jnp.exp2(s - jnp.tile(m_new, (1, tq // LANES)))
            l_ref[h] = alpha * l_ref[h] + _fold(jnp.add, p[:, :LANES], p[:, LANES:])
            acc_ref[h] = alpha * acc_ref[h] + _dot(p.astype(BF16), v_ref[0, pl.ds(k0, tq), ps])
            m_ref[h] = m_new

    m_ref[...] = jnp.full(m_ref.shape, NEG, F32)
    l_ref[...] = jnp.zeros(l_ref.shape, F32)
    acc_ref[...] = jnp.zeros(acc_ref.shape, F32)

    produce(0, s0_ref)

    def body(j, carry):
        produce(2 * j + 1, s1_ref)
        consume(2 * j, s0_ref, False)
        produce(2 * j + 2, s0_ref)
        consume(2 * j + 1, s1_ref, False)
        return carry
    lax.fori_loop(0, nfull // 2, body, 0)

    @pl.when(nfull % 2 == 0)
    def _():
        consume(nfull, s0_ref, True)

    @pl.when(nfull % 2 == 1)
    def _():
        produce(nfull, s1_ref)
        consume(nfull - 1, s0_ref, False)
        consume(nfull, s1_ref, True)
    _write_heads(o_ref, acc_ref, l_ref, C_HEADS)


def _mla(q, k, v, *, tq=512):
    bsz, seq, _ = q.shape
    assert seq % tq == 0
    stat = pltpu.VMEM((C_HEADS, tq, LANES), F32)
    return pl.pallas_call(
        functools.partial(_mla_kernel, tq=tq),
        grid=(bsz, seq // tq),
        in_specs=[pl.BlockSpec((1, tq, C_QW), lambda b, i: (b, i, 0)),
                  _resident((1, seq, C_QW), lambda b, i: (b, 0, 0)),
                  _resident((1, seq, 384), lambda b, i: (b, 0, 0))],
        out_specs=pl.BlockSpec((1, tq, 384), lambda b, i: (b, i, 0)),
        out_shape=jax.ShapeDtypeStruct((bsz, seq, 384), BF16),
        scratch_shapes=[pltpu.VMEM((C_HEADS, tq, tq), F32), pltpu.VMEM((C_HEADS, tq, tq), F32),
                        stat, stat, stat],
        compiler_params=pltpu.CompilerParams(
            dimension_semantics=("arbitrary", "arbitrary"), vmem_limit_bytes=VMEM_LIMIT),
        name="mla",
    )(q, k, v)


GROUP_COLS = 32
I32_ALL = -1


def _i32(x):
    return int(np.uint32(x).view(np.int32))


def _bit_planes(words):
    a = list(reversed(words))
    j, m = 16, 0x0000FFFF
    while j:
        k = 0
        while k < 32:
            t = (a[k] ^ lax.shift_right_logical(a[k + j], jnp.int32(j))) & jnp.int32(_i32(m))
            a[k] = a[k] ^ t
            a[k + j] = a[k + j] ^ lax.shift_left(t, jnp.int32(j))
            k = (k + j + 1) & ~j
        j >>= 1
        m = (m ^ (m << j)) & 0xFFFFFFFF
    return list(reversed(a))


def _dsa_kernel(iq_ref, iw_ref, ik_ref, q_ref, k_ref, v_ref, o_ref,
                key_ref, qm_ref, cand_ref, great_ref, sel_ref, m_ref, l_ref, acc_ref,
                *, tq, tk, topk):
    q0 = pl.program_id(1) * tq
    nkb = (q0 + tq + tk - 1) // tk
    cols_per_blk = tk // LANES
    blks_per_group = GROUP_COLS // cols_per_blk
    n_groups = key_ref.shape[0] // blks_per_group
    row_chunk = (q0 + lax.broadcasted_iota(jnp.int32, (tq, tk), 0)) // CHUNK
    col_chunk0 = lax.broadcasted_iota(jnp.int32, (tq, tk), 1) // CHUNK

    def admissible(kb):
        return (col_chunk0 + kb * (tk // CHUNK)) <= row_chunk

    iq = iq_ref[0]
    head_of_lane = lax.broadcasted_iota(jnp.int32, iq.shape, 1) // IDX_DIM
    for h in range(IDX_HEADS):
        qm_ref[h * tq:(h + 1) * tq, :] = jnp.where(head_of_lane == h, iq, jnp.zeros_like(iq))
    w = iw_ref[0]

    sub = 256

    def score_body(kb, carry):
        adm = admissible(kb)
        for c in range(tk // sub):
            k0 = pl.multiple_of(kb * tk + c * sub, sub)
            d = _dot_t(qm_ref[...], ik_ref[0, pl.ds(k0, sub), :])
            sc = jnp.zeros((tq, sub), F32)
            for h in range(IDX_HEADS):
                sc = sc + jnp.maximum(d[h * tq:(h + 1) * tq], 0.0) * w[:, h:h + 1]
            sc = jnp.where(adm[:, c * sub:(c + 1) * sub], sc, NEG)
            bits = pltpu.bitcast(sc, jnp.int32)
            key_ref[kb, :, c * sub:(c + 1) * sub] = bits ^ ((bits >> 31) | jnp.int32(INT_MIN))
        return carry

    lax.fori_loop(0, nkb, score_body, 0)

    def clear_body(kb, carry):
        key_ref[kb] = jnp.zeros((tq, tk), jnp.int32)
        return carry
    n_active = (nkb + blks_per_group - 1) // blks_per_group
    lax.fori_loop(nkb, n_active * blks_per_group, clear_body, 0)

    @pl.when(jnp.logical_and(pl.program_id(0) == 0, pl.program_id(1) == 0))
    def _():
        lax.fori_loop(n_active * blks_per_group, key_ref.shape[0], clear_body, 0)

    def slot(g, j):
        return g * blks_per_group + j // cols_per_blk, slice((j % cols_per_blk) * LANES,
                                                             (j % cols_per_blk + 1) * LANES)

    for g in range(n_groups):
        @pl.when(nkb > g * blks_per_group)
        def _(g=g):
            def transpose_rows(rg, carry):
                rows = pl.ds(pl.multiple_of(rg * 8, 8), 8)
                cols = [slot(g, j) for j in range(GROUP_COLS)]
                planes = _bit_planes([key_ref[blk, rows, ls] for blk, ls in cols])
                for (blk, ls), p in zip(cols, planes):
                    key_ref[blk, rows, ls] = p
                return carry
            lax.fori_loop(0, tq // 8, transpose_rows, 0)

    for g in range(n_groups):
        n_cols = jnp.clip((nkb - g * blks_per_group) * cols_per_blk, 0, GROUP_COLS)
        word = jnp.where(n_cols >= GROUP_COLS, jnp.int32(I32_ALL),
                         lax.shift_left(jnp.int32(1), jnp.minimum(n_cols, GROUP_COLS - 1)) - 1)
        cand_ref[g] = jnp.full((tq, LANES), word, jnp.int32)
        great_ref[g] = jnp.zeros((tq, LANES), jnp.int32)

    def refine(plane_of, need, prefer_set, groups=n_groups):
        picked = []
        for g in range(groups):
            p = plane_of(g)
            picked.append(cand_ref[g] & (p if prefer_set else ~p))
        cnt = lax.population_count(picked[0])
        for x in picked[1:]:
            cnt = cnt + lax.population_count(x)
        c = _row_all_lanes(jnp.sum, cnt.astype(F32))
        keep = c >= need
        for g in range(groups):
            great_ref[g] = jnp.where(keep, great_ref[g], great_ref[g] | picked[g])
            cand_ref[g] = jnp.where(keep, picked[g], cand_ref[g] ^ picked[g])
        return jnp.where(keep, need, need - c)

    def value_bits(groups):
        def body(i, need):
            blk = blks_per_group - 1 - i
            for c in reversed(range(cols_per_blk)):
                ls = slice(c * LANES, (c + 1) * LANES)
                need = refine(lambda g: key_ref[g * blks_per_group + blk, :, ls], need, True, groups)
            return need
        return lambda need: lax.fori_loop(0, blks_per_group, body, need)

    need = lax.switch(n_active - 1, [value_bits(g + 1) for g in range(n_groups)],
                      jnp.full((tq, LANES), float(topk), F32))

    def index_bits(need):
        lane = lax.broadcasted_iota(jnp.int32, (tq, LANES), 1)
        col_bit_words = (0xAAAAAAAA, 0xCCCCCCCC, 0xF0F0F0F0, 0xFF00FF00, 0xFFFF0000)
        n_group_bits = max(n_groups - 1, 0).bit_length()
        for t in reversed(range(n_group_bits)):
            need = refine(lambda g: jnp.full((tq, LANES), I32_ALL if (g >> t) & 1 else 0, jnp.int32),
                          need, False)
        for t in reversed(range(5)):
            need = refine(lambda g: jnp.full((tq, LANES), _i32(col_bit_words[t]), jnp.int32),
                          need, False)
        for t in reversed(range(7)):
            need = refine(lambda g: jnp.where((lane >> t) & 1 == 1, jnp.int32(I32_ALL), 0),
                          need, False)
        return need

    n_cand = cand_ref[0] * 0
    for g in range(n_groups):
        n_cand = n_cand + lax.population_count(cand_ref[g])
    has_tie = jnp.max(jnp.where(_row_all_lanes(jnp.sum, n_cand.astype(F32)) > need, 1, 0)) > 0
    lax.cond(has_tie, index_bits, lambda need: need, need)
    for g in range(n_groups):
        sel_ref[g] = great_ref[g] | cand_ref[g]

    q = q_ref[0]
    qlane = lax.broadcasted_iota(jnp.int32, (tq, LANES), 1)

    def scores(h, kb, bias):
        ps = slice((h // 2) * LANES, (h // 2 + 1) * LANES)
        qp = q[:, ps]
        qh = jnp.where((qlane < HEAD_DIM) if h % 2 == 0 else (qlane >= HEAD_DIM),
                       qp, jnp.zeros_like(qp))
        k0 = pl.multiple_of(kb * tk, tk)
        return _dot_t(qh, k_ref[0, pl.ds(k0, tk), ps]) + bias

    m_ref[...] = jnp.full(m_ref.shape, NEG, F32)

    def max_body(kb, carry):
        words = sel_ref[kb // blks_per_group]
        first_col = (kb % blks_per_group) * cols_per_blk
        taken = jnp.concatenate(
            [lax.shift_right_logical(words, jnp.full(words.shape, first_col + c, jnp.int32)) & 1
             for c in range(cols_per_blk)], axis=1)
        bias = jnp.where(taken == 1, jnp.where(admissible(kb), 0.0, NEG), NEG)
        key_ref[kb] = pltpu.bitcast(bias, jnp.int32)
        for h in range(B_HEADS):
            m_ref[h] = _fold(jnp.maximum, m_ref[h], scores(h, kb, bias))
        return carry

    lax.fori_loop(0, nkb, max_body, 0)

    for h in range(B_HEADS):
        m_ref[h] = _row_all_lanes(jnp.max, m_ref[h])
    l_ref[...] = jnp.zeros(l_ref.shape, F32)
    acc_ref[...] = jnp.zeros(acc_ref.shape, F32)

    def acc_body(kb, carry):
        bias = pltpu.bitcast(key_ref[kb], F32)
        k0 = pl.multiple_of(kb * tk, tk)
        for h in range(B_HEADS):
            ps = slice((h // 2) * LANES, (h // 2 + 1) * LANES)
            p = jnp.exp2(scores(h, kb, bias) - jnp.tile(m_ref[h], (1, tk // LANES)))
            l_ref[h] = _fold(jnp.add, l_ref[h], p)
            acc_ref[h] += _dot(p.astype(BF16), v_ref[0, pl.ds(k0, tk), ps])
        return carry

    lax.fori_loop(0, nkb, acc_body, 0)
    _write_heads(o_ref, acc_ref, l_ref, B_HEADS)


def _dsa(iq, iw, ik, q, k, v, *, tq=512, tk=512):
    bsz, seq, _ = q.shape
    assert seq % tq == 0 and seq % tk == 0 and GROUP_COLS % (tk // LANES) == 0
    topk = min(TOPK_MAX, seq // 4)
    blks_per_group = GROUP_COLS // (tk // LANES)
    n_groups = -(-(seq // tk) // blks_per_group)
    qmap = lambda b, i: (b, i, 0)
    full = lambda b, i: (b, 0, 0)
    stat = pltpu.VMEM((B_HEADS, tq, LANES), F32)
    words = pltpu.VMEM((n_groups, tq, LANES), jnp.int32)
    return pl.pallas_call(
        functools.partial(_dsa_kernel, tq=tq, tk=tk, topk=topk),
        grid=(bsz, seq // tq),
        in_specs=[pl.BlockSpec((1, tq, IQ_W), qmap),
                  pl.BlockSpec((1, tq, IDX_HEADS), qmap),
                  _resident((1, seq, IQ_W), full),
                  pl.BlockSpec((1, tq, B_WP), qmap),
                  _resident((1, seq, B_WP), full),
                  _resident((1, seq, B_WP), full)],
        out_specs=pl.BlockSpec((1, tq, B_WP), qmap),
        out_shape=jax.ShapeDtypeStruct((bsz, seq, B_WP), BF16),
        scratch_shapes=[pltpu.VMEM((n_groups * blks_per_group, tq, tk), jnp.int32),
                        pltpu.VMEM((IDX_HEADS * tq, IQ_W), BF16),
                        words, words, words,
                        stat, stat, stat],
        compiler_params=pltpu.CompilerParams(
            dimension_semantics=("arbitrary", "arbitrary"), vmem_limit_bytes=VMEM_LIMIT),
        name="dsa",
    )(iq, iw, ik, q, k, v)


def _pad_cols(w, n):
    return jnp.pad(w, ((0, 0), (0, n - w.shape[1])))


def _prep_proj_weights(w_in, c_w_uq, c_w_ukv):
    offs = np.cumsum([0, A_W, A_W, A_W, B_W, B_W, B_W, IQ_W, IDX_DIM, IDX_HEADS,
                      C_Q_RANK, C_KV_RANK, C_ROPE])
    (aq, ak, av, bq, bk, bv, iq, ik, iw, cq, ckv, ckr) = [
        w_in[:, offs[j]:offs[j + 1]] for j in range(12)]
    ascale = HEAD_DIM ** -0.5
    bscale = HEAD_DIM ** -0.5 * LOG2E
    zeros64 = jnp.zeros((D_MODEL, 64), F32)
    zeros32 = jnp.zeros((D_MODEL, 32), F32)
    kr_group = jnp.concatenate([zeros64, ckr, zeros32], axis=1)
    plain = [aq * ascale, ak, av, _pad_cols(bv, B_WP), cq, ckv,
             _pad_cols(iw * (IDX_HEADS * IDX_DIM) ** -0.5, 128)]
    rope_main = [_pad_cols(bq * bscale, B_WP), _pad_cols(bk, B_WP), iq, jnp.tile(ik, (1, IDX_HEADS)),
                 kr_group]
    wcat = jnp.concatenate(plain + rope_main, axis=1).astype(BF16)

    cscale = (C_NOPE + C_ROPE) ** -0.5 * LOG2E
    uq = c_w_uq.reshape(C_Q_RANK, C_HEADS, C_NOPE + C_ROPE) * cscale
    zq = jnp.zeros((C_Q_RANK, C_HEADS, 32), F32)
    wuq = jnp.concatenate([uq, zq], axis=-1).reshape(C_Q_RANK, C_QW).astype(BF16)
    ukv = c_w_ukv.reshape(C_KV_RANK, C_HEADS, C_NOPE + C_V)
    wukn = jnp.concatenate([ukv[..., :C_NOPE], jnp.zeros((C_KV_RANK, C_HEADS, 64), F32)],
                           axis=-1).reshape(C_KV_RANK, C_QW).astype(BF16)
    wuv = _pad_cols(ukv[..., C_NOPE:].reshape(C_KV_RANK, C_HEADS * C_V), 384).astype(BF16)
    return wcat, wuq, wukn, wuv


def _rope_tables(seq):
    pos = jnp.arange(seq, dtype=F32)[:, None]

    def cs(d):
        inv = ROPE_THETA ** (-jnp.arange(0, d, 2, dtype=F32) / d)
        ang = pos * inv[None, :]
        c, s = jnp.cos(ang), jnp.sin(ang)
        return (jnp.tile(jnp.concatenate([c, c], axis=1), (1, LANES // d)),
                jnp.tile(jnp.concatenate([-s, s], axis=1), (1, LANES // d)))

    c64, s64 = cs(HEAD_DIM)
    c32, s32 = cs(C_ROPE)
    lane = jnp.arange(LANES)[None, :]
    roped = (lane >= C_NOPE) & (lane < C_NOPE + C_ROPE)
    cc = jnp.where(roped, c32, 1.0)
    sc = jnp.where(roped, s32, 0.0)
    return jnp.concatenate([c64, s64, c32, s32, cc, sc], axis=1)


def _chunk_bias(rel_bias):
    rb = rel_bias.astype(F32)
    n_rel = A_QB + A_WIN - 1
    below = jnp.broadcast_to(rb[:, :1], (A_HEADS, A_QB - 1 - (CHUNK - 1)))
    above = jnp.broadcast_to(rb[:, -1:], (A_HEADS, A_PREV + A_QB - 1 - A_REL_MAX))
    by_rel_desc = jnp.concatenate([below, rb, above], axis=1)[:, ::-1]
    assert by_rel_desc.shape[1] == n_rel
    v = jnp.roll(by_rel_desc, -(A_QB - 1), axis=1)
    flat = jnp.broadcast_to(v[:, None, :], (A_HEADS, A_QB, n_rel)).reshape(A_HEADS, A_QB * n_rel)
    toeplitz = flat[:, :A_QB * (n_rel - 1)].reshape(A_HEADS, A_QB, n_rel - 1)[:, :, :A_WIN]
    r = np.arange(A_QB)
    ki = np.arange(A_WIN)[None, :] - CHUNK * (r // CHUNK)[:, None]
    inwin = (ki >= 0) & (ki < (A_LEFT_CHUNKS + 1) * CHUNK)
    b = jnp.where(jnp.asarray(inwin)[None], toeplitz, NEG)
    return b.reshape(A_HEADS // 2, 2 * A_QB, A_WIN)


def _prep_wo(w_out):
    pad64 = jnp.zeros((64, D_MODEL), F32)
    return jnp.concatenate([w_out[:A_W], w_out[A_W:A_W + B_W], pad64,
                            w_out[A_W + B_W:], pad64], axis=0).astype(BF16)


def kernel(x, ffn1_norm, ffn1_w_gate, ffn1_w_up, ffn1_w_down, mix_norm, w_in, a_rel_bias, c_q_norm,
           c_kv_norm, c_w_uq, c_w_ukv, w_out, ffn2_norm, ffn2_w_gate, ffn2_w_up, ffn2_w_down,
           final_norm):
    bsz, seq, _ = x.shape
    depth = w_in.shape[0]
    tab = _rope_tables(seq)
    f1g, f1u, f1d = ffn1_w_gate.astype(BF16), ffn1_w_up.astype(BF16), ffn1_w_down.astype(BF16)
    f2g, f2u, f2d = ffn2_w_gate.astype(BF16), ffn2_w_up.astype(BF16), ffn2_w_down.astype(BF16)
    wcat, wuq, wukn, wuv = jax.vmap(_prep_proj_weights)(w_in, c_w_uq, c_w_ukv)
    abias = jax.vmap(_chunk_bias)(a_rel_bias)
    wo = jax.vmap(_prep_wo)(w_out)
    norms = [n[:, None, :] for n in (ffn1_norm, mix_norm, c_q_norm, c_kv_norm, ffn2_norm)]
    n_ffn1, n_mix, n_cq, n_ckv, n_ffn2 = norms
    x2 = x.reshape(bsz * seq, D_MODEL)
    for l in range(depth):
        x2 = _ffn(x2, None, None, n_ffn1, f1g, f1u, f1d, None, l)
        (aq, ak, av, bq, bk, bv, iq, ik, iw, cq, ck, cv) = _proj(
            x2, bsz, seq, n_mix, wcat, tab, n_cq, n_ckv, wuq, wukn, wuv, l)
        o_a = _chunk_attn(aq, ak, av, abias, l)
        o_b = _dsa(iq, iw, ik, bq, bk, bv)
        o_c = _mla(cq, ck, cv)
        x2 = _ffn(x2, (o_a, o_b, o_c), wo, n_ffn2, f2g, f2u, f2d,
                  final_norm[None] if l == depth - 1 else None, l)
    return x2.reshape(bsz, seq, D_MODEL)
```

```python
import functools
import math

import numpy as np
import jax
import jax.numpy as jnp
from jax import lax
from jax.experimental import pallas as pl
from jax.experimental.pallas import tpu as pltpu

F32 = jnp.float32
BF16 = jnp.bfloat16

D_MODEL = 1024
CHUNK = 64
HEAD_DIM = 64
A_HEADS = 6
A_LEFT_CHUNKS = 8
A_REL_MAX = 128
B_HEADS = 5
IDX_HEADS = 8
IDX_DIM = 32
TOPK_MAX = 256
C_HEADS = 5
C_Q_RANK = 384
C_KV_RANK = 256
C_NOPE = 64
C_ROPE = 32
C_V = 64
D_FF = 2816
ROPE_THETA = 10000.0
EPS = 1e-6
NEG = -1e30
LOG2E = math.log2(math.e)

LANES = 128
A_W = A_HEADS * HEAD_DIM
B_W = B_HEADS * HEAD_DIM
B_WP = 384
C_QW = C_HEADS * LANES
IQ_W = IDX_HEADS * IDX_DIM
VMEM_LIMIT = 56 * 1024 * 1024

OFF_AQ, OFF_AK, OFF_AV, OFF_BV, OFF_CQ, OFF_CKV, OFF_IW = 0, 384, 768, 1152, 1536, 1920, 2176
OFF_ROPE = 2304
ROPE_W = 384 + 384 + 256 + 256 + 128
R_BQ, R_BK, R_IQ, R_IK, R_KR = 0, 384, 768, 1024, 1280
W_CAT = OFF_ROPE + ROPE_W


INT_MIN = -(2 ** 31)


def _dot(a, b):
    return jnp.dot(a, b, preferred_element_type=F32)


def _dot_t(a, b):
    return lax.dot_general(a, b, (((1,), (1,)), ((), ())), preferred_element_type=F32)


def _rms(x, g):
    ms = jnp.mean(x * x, axis=-1, keepdims=True)
    return x * lax.rsqrt(ms + EPS) * g


def _layer(l, *tail):
    return pl.BlockSpec((None,) + tail, lambda *_: (l,) + (0,) * len(tail))


def _ffn_kernel(*refs, has_attn, has_final, tf):
    it = iter(refs)
    x_ref = next(it)
    if has_attn:
        oa_ref, ob_ref, oc_ref, wo_ref = next(it), next(it), next(it), next(it)
    g_ref, wg_ref, wu_ref, wd_ref = next(it), next(it), next(it), next(it)
    fg_ref = next(it) if has_final else None
    o_ref = next(it)
    acc_ref = next(it)

    x = x_ref[...]
    if has_attn:
        x = x + _dot(oa_ref[0], wo_ref[0:384, :])
        x = x + _dot(ob_ref[0], wo_ref[384:768, :])
        x = x + _dot(oc_ref[0], wo_ref[768:1152, :])
    xn = _rms(x, g_ref[...]).astype(BF16)
    for j in range(D_FF // tf):
        g = _dot(xn, wg_ref[:, j * tf:(j + 1) * tf])
        u = _dot(xn, wu_ref[:, j * tf:(j + 1) * tf])
        h = (g * jax.nn.sigmoid(g) * u).astype(BF16)
        c = _dot(h, wd_ref[j * tf:(j + 1) * tf, :])
        if j == 0:
            acc_ref[...] = c
        else:
            acc_ref[...] += c
    y = x + 0.5 * acc_ref[...]
    if has_final:
        y = _rms(y, fg_ref[...])
    o_ref[...] = y


def _ffn(x2, attn, wo, g, wg, wu, wd, final_g, l, *, tm=512, tf=256):
    n = x2.shape[0]
    has_attn = attn is not None
    has_final = final_g is not None
    const = lambda i: (0, 0)
    row = lambda i: (i, 0)
    in_specs = [pl.BlockSpec((tm, D_MODEL), row)]
    args = [x2]
    if has_attn:
        s = attn[0].shape[1]
        nt = s // tm
        amap = lambda i: (i // nt, i % nt, 0)
        for a in attn:
            in_specs.append(pl.BlockSpec((1, tm, 384), amap))
            args.append(a)
        in_specs.append(_layer(l, 1152, D_MODEL))
        args.append(wo)
    in_specs += [_layer(l, 1, D_MODEL), _layer(l, D_MODEL, D_FF), _layer(l, D_MODEL, D_FF),
                 _layer(l, D_FF, D_MODEL)]
    args += [g, wg, wu, wd]
    if has_final:
        in_specs.append(pl.BlockSpec((1, D_MODEL), const))
        args.append(final_g)
    return pl.pallas_call(
        functools.partial(_ffn_kernel, has_attn=has_attn, has_final=has_final, tf=tf),
        grid=(n // tm,),
        in_specs=in_specs,
        out_specs=pl.BlockSpec((tm, D_MODEL), row),
        out_shape=jax.ShapeDtypeStruct((n, D_MODEL), F32),
        scratch_shapes=[pltpu.VMEM((tm, D_MODEL), F32)],
        compiler_params=pltpu.CompilerParams(
            dimension_semantics=("arbitrary",), vmem_limit_bytes=VMEM_LIMIT),
        name="ffn",
    )(*args)


def _proj_kernel(x_ref, g_ref, w_ref, tab_ref, qn_ref, kvn_ref, wuq_ref, wukn_ref, wuv_ref,
                 aq_ref, ak_ref, av_ref, bq_ref, bk_ref, bv_ref, iq_ref, ik_ref, iw_ref,
                 cq_ref, ck_ref, cv_ref):
    xn = _rms(x_ref[...], g_ref[...]).astype(BF16)

    def mm(off, n):
        return _dot(xn, w_ref[:, off:off + n])

    aq_ref[0] = mm(OFF_AQ, 384).astype(BF16)
    ak_ref[0] = mm(OFF_AK, 384).astype(BF16)
    av_ref[0] = mm(OFF_AV, 384).astype(BF16)
    bv_ref[0] = mm(OFF_BV, 384).astype(BF16)
    iw_ref[0] = mm(OFF_IW, 128)[:, :IDX_HEADS]

    cos64, sin64 = tab_ref[:, 0:128], tab_ref[:, 128:256]
    cos32, sin32 = tab_ref[:, 256:384], tab_ref[:, 384:512]
    cosc, sinc = tab_ref[:, 512:640], tab_ref[:, 640:768]
    lane = lax.broadcasted_iota(jnp.int32, (x_ref.shape[0], LANES), 1)

    def rope(x, d, cos, sin):
        partner = jnp.where(lane % d < d // 2, pltpu.roll(x, LANES - d // 2, axis=1),
                            pltpu.roll(x, d // 2, axis=1))
        return x * cos + partner * sin

    def rope_group(r_off, n, d, cos, sin):
        m = mm(OFF_ROPE + r_off, n)
        return [rope(m[:, c * 128:(c + 1) * 128], d, cos, sin) for c in range(n // 128)]

    for c, v in enumerate(rope_group(R_BQ, 384, HEAD_DIM, cos64, sin64)):
        bq_ref[0, :, c * 128:(c + 1) * 128] = v.astype(BF16)
    for c, v in enumerate(rope_group(R_BK, 384, HEAD_DIM, cos64, sin64)):
        bk_ref[0, :, c * 128:(c + 1) * 128] = v.astype(BF16)
    for c, v in enumerate(rope_group(R_IQ, 256, IDX_DIM, cos32, sin32)):
        iq_ref[0, :, c * 128:(c + 1) * 128] = v.astype(BF16)
    for c, v in enumerate(rope_group(R_IK, 256, IDX_DIM, cos32, sin32)):
        ik_ref[0, :, c * 128:(c + 1) * 128] = v.astype(BF16)
    krg = rope_group(R_KR, 128, C_ROPE, cosc, sinc)[0]

    cqn = _rms(mm(OFF_CQ, C_Q_RANK), qn_ref[...]).astype(BF16)
    qm = _dot(cqn, wuq_ref[...])
    ckn = _rms(mm(OFF_CKV, C_KV_RANK), kvn_ref[...]).astype(BF16)
    kn = _dot(ckn, wukn_ref[...])
    for h in range(C_HEADS):
        sl = slice(h * 128, (h + 1) * 128)
        cq_ref[0, :, sl] = rope(qm[:, sl], C_ROPE, cosc, sinc).astype(BF16)
        ck_ref[0, :, sl] = (kn[:, sl] + krg).astype(BF16)
    cv_ref[0] = _dot(ckn, wuv_ref[...]).astype(BF16)


def _proj(x2, bsz, seq, g, wcat, tab, qn, kvn, wuq, wukn, wuv, l, *, tm=512):
    nt = seq // tm
    const = lambda i: (0, 0)
    omap = lambda i: (i // nt, i % nt, 0)
    widths = [384, 384, 384, 384, 384, 384, IQ_W, IQ_W, IDX_HEADS, C_QW, C_QW, 384]
    dtypes = [BF16] * 8 + [F32] + [BF16] * 3
    return pl.pallas_call(
        _proj_kernel,
        grid=(bsz * nt,),
        in_specs=[pl.BlockSpec((tm, D_MODEL), lambda i: (i, 0)),
                  _layer(l, 1, D_MODEL),
                  _layer(l, D_MODEL, W_CAT),
                  pl.BlockSpec((tm, 768), lambda i: (i % nt, 0)),
                  _layer(l, 1, C_Q_RANK),
                  _layer(l, 1, C_KV_RANK),
                  _layer(l, C_Q_RANK, C_QW),
                  _layer(l, C_KV_RANK, C_QW),
                  _layer(l, C_KV_RANK, 384)],
        out_specs=[pl.BlockSpec((1, tm, w), omap) for w in widths],
        out_shape=[jax.ShapeDtypeStruct((bsz, seq, w), d) for w, d in zip(widths, dtypes)],
        compiler_params=pltpu.CompilerParams(
            dimension_semantics=("arbitrary",), vmem_limit_bytes=VMEM_LIMIT),
        name="proj",
    )(x2, g, wcat, tab, qn, kvn, wuq, wukn, wuv)


A_QB = 2 * CHUNK
A_WIN = (A_LEFT_CHUNKS + 2) * CHUNK
A_PREV = A_LEFT_CHUNKS * CHUNK


def _chunk_attn_kernel(q_ref, kp_ref, kc_ref, vp_ref, vc_ref, bias_ref, o_ref, kcat_ref, vcat_ref,
                       *, tq):
    kcat_ref[0:A_PREV, :] = kp_ref[0]
    kcat_ref[A_PREV:A_PREV + tq, :] = kc_ref[0]
    vcat_ref[0:A_PREV, :] = vp_ref[0]
    vcat_ref[A_PREV:A_PREV + tq, :] = vc_ref[0]
    lane = lax.broadcasted_iota(jnp.int32, (A_QB, LANES), 1)
    col_chunk = lax.broadcasted_iota(jnp.int32, (2 * A_QB, A_WIN), 1) // CHUNK

    def attend(first_tile):
        for qb in range(tq // A_QB):
            for p in range(A_HEADS // 2):
                ls = slice(p * LANES, (p + 1) * LANES)
                q = q_ref[0, qb * A_QB:(qb + 1) * A_QB, ls]
                qs = jnp.concatenate([jnp.where(lane < HEAD_DIM, q, jnp.zeros_like(q)),
                                      jnp.where(lane >= HEAD_DIM, q, jnp.zeros_like(q))], axis=0)
                kw = kcat_ref[qb * A_QB:qb * A_QB + A_WIN, ls]
                vw = vcat_ref[qb * A_QB:qb * A_QB + A_WIN, ls]
                s = _dot_t(qs, kw) + bias_ref[p]
                if first_tile:
                    s = jnp.where(col_chunk + (2 * qb - A_LEFT_CHUNKS) >= 0, s, NEG)
                m = _row_all_lanes(jnp.max, _fold(jnp.maximum, s[:, :LANES], s[:, LANES:]))
                e = jnp.exp2(s - jnp.tile(m, (1, A_WIN // LANES)))
                l = _row_all_lanes(jnp.sum, _fold(jnp.add, e[:, :LANES], e[:, LANES:]))
                o = _dot(e.astype(BF16), vw) / l
                o_ref[0, qb * A_QB:(qb + 1) * A_QB, ls] = jnp.where(
                    lane < HEAD_DIM, o[0:A_QB], o[A_QB:2 * A_QB]).astype(BF16)

    @pl.when(pl.program_id(1) == 0)
    def _():
        attend(True)

    @pl.when(pl.program_id(1) > 0)
    def _():
        attend(False)


def _chunk_attn(q, k, v, bias, l, *, tq=512):
    bsz, seq, _ = q.shape
    assert tq == A_PREV and seq % tq == 0
    cur = lambda b, i: (b, i, 0)
    prev = lambda b, i: (b, jnp.maximum(i - 1, 0), 0)
    blk = (1, tq, A_W)
    return pl.pallas_call(
        functools.partial(_chunk_attn_kernel, tq=tq),
        grid=(bsz, seq // tq),
        in_specs=[pl.BlockSpec(blk, cur), pl.BlockSpec(blk, prev), pl.BlockSpec(blk, cur),
                  pl.BlockSpec(blk, prev), pl.BlockSpec(blk, cur),
                  _layer(l, A_HEADS // 2, 2 * A_QB, A_WIN)],
        out_specs=pl.BlockSpec(blk, cur),
        out_shape=jax.ShapeDtypeStruct((bsz, seq, A_W), BF16),
        scratch_shapes=[pltpu.VMEM((A_PREV + tq, A_W), BF16), pltpu.VMEM((A_PREV + tq, A_W), BF16)],
        compiler_params=pltpu.CompilerParams(
            dimension_semantics=("arbitrary", "arbitrary"), vmem_limit_bytes=VMEM_LIMIT),
        name="chunk_attn",
    )(q, k, k, v, v, bias)


def _fold(op, acc, s):
    for c in range(s.shape[1] // LANES):
        acc = op(acc, s[:, c * LANES:(c + 1) * LANES])
    return acc


def _row_all_lanes(op, x):
    return jnp.broadcast_to(op(x, axis=1, keepdims=True), x.shape)


def _pair_select(even_head, odd_head):
    lane = lax.broadcasted_iota(jnp.int32, even_head.shape, 1)
    return jnp.where(lane < HEAD_DIM, even_head, odd_head)


def _write_heads(o_ref, acc_ref, l_ref, n_heads):
    for p in range((n_heads + 1) // 2):
        o0 = acc_ref[2 * p] / _row_all_lanes(jnp.sum, l_ref[2 * p])
        if 2 * p + 1 < n_heads:
            o1 = acc_ref[2 * p + 1] / _row_all_lanes(jnp.sum, l_ref[2 * p + 1])
        else:
            o1 = jnp.zeros_like(o0)
        o_ref[0, :, p * LANES:(p + 1) * LANES] = _pair_select(o0, o1).astype(BF16)


def _resident(shape, index_map):
    return pl.BlockSpec(shape, index_map, pipeline_mode=pl.Buffered(1))


def _mla_kernel(q_ref, k_ref, v_ref, o_ref, s0_ref, s1_ref, m_ref, l_ref, acc_ref, *, tq):
    q0 = pl.multiple_of(pl.program_id(1) * tq, tq)
    nfull = q0 // tq
    diag_mask = (lax.broadcasted_iota(jnp.int32, (tq, tq), 1) // CHUNK
                 <= lax.broadcasted_iota(jnp.int32, (tq, tq), 0) // CHUNK)

    def produce(kb, s_ref):
        k0 = pl.multiple_of(kb * tq, tq)
        for h in range(C_HEADS):
            hs = slice(h * LANES, (h + 1) * LANES)
            s_ref[h] = _dot_t(q_ref[0, :, hs], k_ref[0, pl.ds(k0, tq), hs])

    def consume(kb, s_ref, masked):
        k0 = pl.multiple_of(kb * tq, tq)
        for h in range(C_HEADS):
            ps = slice((h // 2) * LANES, (h // 2 + 1) * LANES)
            s = s_ref[h]
            if masked:
                s = jnp.where(diag_mask, s, NEG)
            m_old = m_ref[h]
            m_new = jnp.maximum(m_old, _row_all_lanes(
                jnp.max, _fold(jnp.maximum, s[:, :LANES], s[:, LANES:])))
            alpha = jnp.exp2(m_old - m_new)
            p = jnp.exp2(s - jnp.tile(m_new, (1, tq // LANES)))
            l_ref[h] = alpha * l_ref[h] + _fold(jnp.add, p[:, :LANES], p[:, LANES:])
            acc_ref[h] = alpha * acc_ref[h] + _dot(p.astype(BF16), v_ref[0, pl.ds(k0, tq), ps])
            m_ref[h] = m_new

    m_ref[...] = jnp.full(m_ref.shape, NEG, F32)
    l_ref[...] = jnp.zeros(l_ref.shape, F32)
    acc_ref[...] = jnp.zeros(acc_ref.shape, F32)

    produce(0, s0_ref)

    def body(j, carry):
        produce(2 * j + 1, s1_ref)
        consume(2 * j, s0_ref, False)
        produce(2 * j + 2, s0_ref)
        consume(2 * j + 1, s1_ref, False)
        return carry
    lax.fori_loop(0, nfull // 2, body, 0)

    @pl.when(nfull % 2 == 0)
    def _():
        consume(nfull, s0_ref, True)

    @pl.when(nfull % 2 == 1)
    def _():
        produce(nfull, s1_ref)
        consume(nfull - 1, s0_ref, False)
        consume(nfull, s1_ref, True)
    _write_heads(o_ref, acc_ref, l_ref, C_HEADS)


def _mla(q, k, v, *, tq=512):
    bsz, seq, _ = q.shape
    assert seq % tq == 0
    stat = pltpu.VMEM((C_HEADS, tq, LANES), F32)
    return pl.pallas_call(
        functools.partial(_mla_kernel, tq=tq),
        grid=(bsz, seq // tq),
        in_specs=[pl.BlockSpec((1, tq, C_QW), lambda b, i: (b, i, 0)),
                  _resident((1, seq, C_QW), lambda b, i: (b, 0, 0)),
                  _resident((1, seq, 384), lambda b, i: (b, 0, 0))],
        out_specs=pl.BlockSpec((1, tq, 384), lambda b, i: (b, i, 0)),
        out_shape=jax.ShapeDtypeStruct((bsz, seq, 384), BF16),
        scratch_shapes=[pltpu.VMEM((C_HEADS, tq, tq), F32), pltpu.VMEM((C_HEADS, tq, tq), F32),
                        stat, stat, stat],
        compiler_params=pltpu.CompilerParams(
            dimension_semantics=("arbitrary", "arbitrary"), vmem_limit_bytes=VMEM_LIMIT),
        name="mla",
    )(q, k, v)


GROUP_COLS = 32
I32_ALL = -1


def _i32(x):
    return int(np.uint32(x).view(np.int32))


def _bit_planes(words):
    a = list(reversed(words))
    j, m = 16, 0x0000FFFF
    while j:
        k = 0
        while k < 32:
            t = (a[k] ^ lax.shift_right_logical(a[k + j], jnp.int32(j))) & jnp.int32(_i32(m))
            a[k] = a[k] ^ t
            a[k + j] = a[k + j] ^ lax.shift_left(t, jnp.int32(j))
            k = (k + j + 1) & ~j
        j >>= 1
        m = (m ^ (m << j)) & 0xFFFFFFFF
    return list(reversed(a))


def _dsa_kernel(iq_ref, iw_ref, ik_ref, q_ref, k_ref, v_ref, o_ref,
                key_ref, qm_ref, wb_ref, cand_ref, great_ref, sel_ref, m_ref, l_ref, acc_ref,
                *, tq, tk, topk):
    q0 = pl.program_id(1) * tq
    nkb = (q0 + tq + tk - 1) // tk
    cols_per_blk = tk // LANES
    blks_per_group = GROUP_COLS // cols_per_blk
    n_groups = key_ref.shape[0] // blks_per_group
    row_chunk = (q0 + lax.broadcasted_iota(jnp.int32, (tq, tk), 0)) // CHUNK
    col_chunk0 = lax.broadcasted_iota(jnp.int32, (tq, tk), 1) // CHUNK

    def admissible(kb):
        return (col_chunk0 + kb * (tk // CHUNK)) <= row_chunk

    iq = iq_ref[0]
    head_of_lane = lax.broadcasted_iota(jnp.int32, iq.shape, 1) // IDX_DIM
    for h in range(IDX_HEADS):
        qm_ref[h * tq:(h + 1) * tq, :] = jnp.where(head_of_lane == h, iq, jnp.zeros_like(iq))
    w = iw_ref[0]
    for h in range(IDX_HEADS):
        wb_ref[h] = jnp.broadcast_to(w[:, h:h + 1], (tq, LANES))

    sub = 256

    def score_body(kb, carry):
        adm = admissible(kb)
        for c in range(tk // sub):
            k0 = pl.multiple_of(kb * tk + c * sub, sub)
            d = _dot_t(qm_ref[...], ik_ref[0, pl.ds(k0, sub), :])
            sc = jnp.zeros((tq, sub), F32)
            for h in range(IDX_HEADS):
                sc = sc + jnp.maximum(d[h * tq:(h + 1) * tq], 0.0) * jnp.tile(wb_ref[h], (1, sub // LANES))
            sc = jnp.where(adm[:, c * sub:(c + 1) * sub], sc, NEG)
            bits = pltpu.bitcast(sc, jnp.int32)
            key_ref[kb, :, c * sub:(c + 1) * sub] = bits ^ ((bits >> 31) | jnp.int32(INT_MIN))
        return carry

    lax.fori_loop(0, nkb, score_body, 0)

    def clear_body(kb, carry):
        key_ref[kb] = jnp.zeros((tq, tk), jnp.int32)
        return carry
    n_active = (nkb + blks_per_group - 1) // blks_per_group
    lax.fori_loop(nkb, n_active * blks_per_group, clear_body, 0)

    def slot(g, j):
        return g * blks_per_group + j // cols_per_blk, slice((j % cols_per_blk) * LANES,
                                                             (j % cols_per_blk + 1) * LANES)

    for g in range(n_groups):
        @pl.when(nkb > g * blks_per_group)
        def _(g=g):
            def transpose_rows(rg, carry):
                rows = pl.ds(pl.multiple_of(rg * 8, 8), 8)
                cols = [slot(g, j) for j in range(GROUP_COLS)]
                planes = _bit_planes([key_ref[blk, rows, ls] for blk, ls in cols])
                for (blk, ls), p in zip(cols, planes):
                    key_ref[blk, rows, ls] = p
                return carry
            lax.fori_loop(0, tq // 8, transpose_rows, 0)

    for g in range(n_groups):
        n_cols = jnp.clip((nkb - g * blks_per_group) * cols_per_blk, 0, GROUP_COLS)
        word = jnp.where(n_cols >= GROUP_COLS, jnp.int32(I32_ALL),
                         lax.shift_left(jnp.int32(1), jnp.minimum(n_cols, GROUP_COLS - 1)) - 1)
        cand_ref[g] = jnp.full((tq, LANES), word, jnp.int32)
        great_ref[g] = jnp.zeros((tq, LANES), jnp.int32)

    def refine(plane_of, need, prefer_set, groups=n_groups):
        picked = []
        for g in range(groups):
            p = plane_of(g)
            picked.append(cand_ref[g] & (p if prefer_set else ~p))
        cnt = lax.population_count(picked[0])
        for x in picked[1:]:
            cnt = cnt + lax.population_count(x)
        c = _row_all_lanes(jnp.sum, cnt.astype(F32))
        keep = c >= need
        for g in range(groups):
            great_ref[g] = jnp.where(keep, great_ref[g], great_ref[g] | picked[g])
            cand_ref[g] = jnp.where(keep, picked[g], cand_ref[g] ^ picked[g])
        return jnp.where(keep, need, need - c)

    def value_bits(groups):
        def body(i, need):
            blk = blks_per_group - 1 - i
            for c in reversed(range(cols_per_blk)):
                ls = slice(c * LANES, (c + 1) * LANES)
                need = refine(lambda g: key_ref[g * blks_per_group + blk, :, ls], need, True, groups)
            return need
        return lambda need: lax.fori_loop(0, blks_per_group, body, need)

    need = lax.switch(n_active - 1, [value_bits(g + 1) for g in range(n_groups)],
                      jnp.full((tq, LANES), float(topk), F32))

    def index_bits(need):
        lane = lax.broadcasted_iota(jnp.int32, (tq, LANES), 1)
        col_bit_words = (0xAAAAAAAA, 0xCCCCCCCC, 0xF0F0F0F0, 0xFF00FF00, 0xFFFF0000)
        n_group_bits = max(n_groups - 1, 0).bit_length()
        for t in reversed(range(n_group_bits)):
            need = refine(lambda g: jnp.full((tq, LANES), I32_ALL if (g >> t) & 1 else 0, jnp.int32),
                          need, False)
        for t in reversed(range(5)):
            need = refine(lambda g: jnp.full((tq, LANES), _i32(col_bit_words[t]), jnp.int32),
                          need, False)
        for t in reversed(range(7)):
            need = refine(lambda g: jnp.where((lane >> t) & 1 == 1, jnp.int32(I32_ALL), 0),
                          need, False)
        return need

    n_cand = cand_ref[0] * 0
    for g in range(n_groups):
        n_cand = n_cand + lax.population_count(cand_ref[g])
    has_tie = jnp.max(jnp.where(_row_all_lanes(jnp.sum, n_cand.astype(F32)) > need, 1, 0)) > 0
    lax.cond(has_tie, index_bits, lambda need: need, need)
    for g in range(n_groups):
        sel_ref[g] = great_ref[g] | cand_ref[g]

    q = q_ref[0]
    qlane = lax.broadcasted_iota(jnp.int32, (tq, LANES), 1)

    def scores(h, kb, bias):
        ps = slice((h // 2) * LANES, (h // 2 + 1) * LANES)
        qp = q[:, ps]
        qh = jnp.where((qlane < HEAD_DIM) if h % 2 == 0 else (qlane >= HEAD_DIM),
                       qp, jnp.zeros_like(qp))
        k0 = pl.multiple_of(kb * tk, tk)
        return _dot_t(qh, k_ref[0, pl.ds(k0, tk), ps]) + bias

    m_ref[...] = jnp.full(m_ref.shape, NEG, F32)

    def max_body(kb, carry):
        words = sel_ref[kb // blks_per_group]
        first_col = (kb % blks_per_group) * cols_per_blk
        taken = jnp.concatenate(
            [lax.shift_right_logical(words, jnp.full(words.shape, first_col + c, jnp.int32)) & 1
             for c in range(cols_per_blk)], axis=1)
        bias = jnp.where(taken == 1, jnp.where(admissible(kb), 0.0, NEG), NEG)
        key_ref[kb] = pltpu.bitcast(bias, jnp.int32)
        for h in range(B_HEADS):
            m_ref[h] = _fold(jnp.maximum, m_ref[h], scores(h, kb, bias))
        return carry

    lax.fori_loop(0, nkb, max_body, 0)

    for h in range(B_HEADS):
        m_ref[h] = _row_all_lanes(jnp.max, m_ref[h])
    l_ref[...] = jnp.zeros(l_ref.shape, F32)
    acc_ref[...] = jnp.zeros(acc_ref.shape, F32)

    def acc_body(kb, carry):
        bias = pltpu.bitcast(key_ref[kb], F32)
        k0 = pl.multiple_of(kb * tk, tk)
        for h in range(B_HEADS):
            ps = slice((h // 2) * LANES, (h // 2 + 1) * LANES)
            p = jnp.exp2(scores(h, kb, bias) - jnp.tile(m_ref[h], (1, tk // LANES)))
            l_ref[h] = _fold(jnp.add, l_ref[h], p)
            acc_ref[h] += _dot(p.astype(BF16), v_ref[0, pl.ds(k0, tk), ps])
        return carry

    lax.fori_loop(0, nkb, acc_body, 0)
    _write_heads(o_ref, acc_ref, l_ref, B_HEADS)


def _dsa(iq, iw, ik, q, k, v, *, tq=512, tk=512):
    bsz, seq, _ = q.shape
    assert seq % tq == 0 and seq % tk == 0 and GROUP_COLS % (tk // LANES) == 0
    topk = min(TOPK_MAX, seq // 4)
    blks_per_group = GROUP_COLS // (tk // LANES)
    n_groups = -(-(seq // tk) // blks_per_group)
    qmap = lambda b, i: (b, i, 0)
    full = lambda b, i: (b, 0, 0)
    stat = pltpu.VMEM((B_HEADS, tq, LANES), F32)
    words = pltpu.VMEM((n_groups, tq, LANES), jnp.int32)
    return pl.pallas_call(
        functools.partial(_dsa_kernel, tq=tq, tk=tk, topk=topk),
        grid=(bsz, seq // tq),
        in_specs=[pl.BlockSpec((1, tq, IQ_W), qmap),
                  pl.BlockSpec((1, tq, IDX_HEADS), qmap),
                  _resident((1, seq, IQ_W), full),
                  pl.BlockSpec((1, tq, B_WP), qmap),
                  _resident((1, seq, B_WP), full),
                  _resident((1, seq, B_WP), full)],
        out_specs=pl.BlockSpec((1, tq, B_WP), qmap),
        out_shape=jax.ShapeDtypeStruct((bsz, seq, B_WP), BF16),
        scratch_shapes=[pltpu.VMEM((n_groups * blks_per_group, tq, tk), jnp.int32),
                        pltpu.VMEM((IDX_HEADS * tq, IQ_W), BF16),
                        pltpu.VMEM((IDX_HEADS, tq, LANES), F32),
                        words, words, words,
                        stat, stat, stat],
        compiler_params=pltpu.CompilerParams(
            dimension_semantics=("arbitrary", "arbitrary"), vmem_limit_bytes=VMEM_LIMIT),
        name="dsa",
    )(iq, iw, ik, q, k, v)


def _pad_cols(w, n):
    return jnp.pad(w, ((0, 0), (0, n - w.shape[1])))


def _prep_proj_weights(w_in, c_w_uq, c_w_ukv):
    offs = np.cumsum([0, A_W, A_W, A_W, B_W, B_W, B_W, IQ_W, IDX_DIM, IDX_HEADS,
                      C_Q_RANK, C_KV_RANK, C_ROPE])
    (aq, ak, av, bq, bk, bv, iq, ik, iw, cq, ckv, ckr) = [
        w_in[:, offs[j]:offs[j + 1]] for j in range(12)]
    ascale = HEAD_DIM ** -0.5 * LOG2E
    bscale = HEAD_DIM ** -0.5 * LOG2E
    zeros64 = jnp.zeros((D_MODEL, 64), F32)
    zeros32 = jnp.zeros((D_MODEL, 32), F32)
    kr_group = jnp.concatenate([zeros64, ckr, zeros32], axis=1)
    plain = [aq * ascale, ak, av, _pad_cols(bv, B_WP), cq, ckv,
             _pad_cols(iw * (IDX_HEADS * IDX_DIM) ** -0.5, 128)]
    rope_main = [_pad_cols(bq * bscale, B_WP), _pad_cols(bk, B_WP), iq, jnp.tile(ik, (1, IDX_HEADS)),
                 kr_group]
    wcat = jnp.concatenate(plain + rope_main, axis=1).astype(BF16)

    cscale = (C_NOPE + C_ROPE) ** -0.5 * LOG2E
    uq = c_w_uq.reshape(C_Q_RANK, C_HEADS, C_NOPE + C_ROPE) * cscale
    zq = jnp.zeros((C_Q_RANK, C_HEADS, 32), F32)
    wuq = jnp.concatenate([uq, zq], axis=-1).reshape(C_Q_RANK, C_QW).astype(BF16)
    ukv = c_w_ukv.reshape(C_KV_RANK, C_HEADS, C_NOPE + C_V)
    wukn = jnp.concatenate([ukv[..., :C_NOPE], jnp.zeros((C_KV_RANK, C_HEADS, 64), F32)],
                           axis=-1).reshape(C_KV_RANK, C_QW).astype(BF16)
    wuv = _pad_cols(ukv[..., C_NOPE:].reshape(C_KV_RANK, C_HEADS * C_V), 384).astype(BF16)
    return wcat, wuq, wukn, wuv


def _rope_tables(seq):
    pos = jnp.arange(seq, dtype=F32)[:, None]

    def cs(d):
        inv = ROPE_THETA ** (-jnp.arange(0, d, 2, dtype=F32) / d)
        ang = pos * inv[None, :]
        c, s = jnp.cos(ang), jnp.sin(ang)
        return (jnp.tile(jnp.concatenate([c, c], axis=1), (1, LANES // d)),
                jnp.tile(jnp.concatenate([-s, s], axis=1), (1, LANES // d)))

    c64, s64 = cs(HEAD_DIM)
    c32, s32 = cs(C_ROPE)
    lane = jnp.arange(LANES)[None, :]
    roped = (lane >= C_NOPE) & (lane < C_NOPE + C_ROPE)
    cc = jnp.where(roped, c32, 1.0)
    sc = jnp.where(roped, s32, 0.0)
    return jnp.concatenate([c64, s64, c32, s32, cc, sc], axis=1)


def _chunk_bias(rel_bias):
    rb = rel_bias.astype(F32)
    n_rel = A_QB + A_WIN - 1
    below = jnp.broadcast_to(rb[:, :1], (A_HEADS, A_QB - 1 - (CHUNK - 1)))
    above = jnp.broadcast_to(rb[:, -1:], (A_HEADS, A_PREV + A_QB - 1 - A_REL_MAX))
    by_rel_desc = jnp.concatenate([below, rb, above], axis=1)[:, ::-1]
    assert by_rel_desc.shape[1] == n_rel
    v = jnp.roll(by_rel_desc, -(A_QB - 1), axis=1)
    flat = jnp.broadcast_to(v[:, None, :], (A_HEADS, A_QB, n_rel)).reshape(A_HEADS, A_QB * n_rel)
    toeplitz = flat[:, :A_QB * (n_rel - 1)].reshape(A_HEADS, A_QB, n_rel - 1)[:, :, :A_WIN]
    r = np.arange(A_QB)
    ki = np.arange(A_WIN)[None, :] - CHUNK * (r // CHUNK)[:, None]
    inwin = (ki >= 0) & (ki < (A_LEFT_CHUNKS + 1) * CHUNK)
    b = jnp.where(jnp.asarray(inwin)[None], toeplitz * LOG2E, NEG)
    return b.reshape(A_HEADS // 2, 2 * A_QB, A_WIN)


def _prep_wo(w_out):
    pad64 = jnp.zeros((64, D_MODEL), F32)
    return jnp.concatenate([w_out[:A_W], w_out[A_W:A_W + B_W], pad64,
                            w_out[A_W + B_W:], pad64], axis=0).astype(BF16)


def kernel(x, ffn1_norm, ffn1_w_gate, ffn1_w_up, ffn1_w_down, mix_norm, w_in, a_rel_bias, c_q_norm,
           c_kv_norm, c_w_uq, c_w_ukv, w_out, ffn2_norm, ffn2_w_gate, ffn2_w_up, ffn2_w_down,
           final_norm):
    bsz, seq, _ = x.shape
    depth = w_in.shape[0]
    tab = _rope_tables(seq)
    f1g, f1u, f1d = ffn1_w_gate.astype(BF16), ffn1_w_up.astype(BF16), ffn1_w_down.astype(BF16)
    f2g, f2u, f2d = ffn2_w_gate.astype(BF16), ffn2_w_up.astype(BF16), ffn2_w_down.astype(BF16)
    wcat, wuq, wukn, wuv = jax.vmap(_prep_proj_weights)(w_in, c_w_uq, c_w_ukv)
    abias = jax.vmap(_chunk_bias)(a_rel_bias)
    wo = jax.vmap(_prep_wo)(w_out)
    norms = [n[:, None, :] for n in (ffn1_norm, mix_norm, c_q_norm, c_kv_norm, ffn2_norm)]
    n_ffn1, n_mix, n_cq, n_ckv, n_ffn2 = norms
    x2 = x.reshape(bsz * seq, D_MODEL)
    for l in range(depth):
        x2 = _ffn(x2, None, None, n_ffn1, f1g, f1u, f1d, None, l)
        (aq, ak, av, bq, bk, bv, iq, ik, iw, cq, ck, cv) = _proj(
            x2, bsz, seq, n_mix, wcat, tab, n_cq, n_ckv, wuq, wukn, wuv, l)
        o_a = _chunk_attn(aq, ak, av, abias, l)
        o_b = _dsa(iq, iw, ik, bq, bk, bv)
        o_c = _mla(cq, ck, cv)
        x2 = _ffn(x2, (o_a, o_b, o_c), wo, n_ffn2, f2g, f2u, f2d,
                  final_norm[None] if l == depth - 1 else None, l)
    return x2.reshape(bsz, seq, D_MODEL)
```

```python
import functools
import math

import numpy as np
import jax
import jax.numpy as jnp
from jax import lax
from jax.experimental import pallas as pl
from jax.experimental.pallas import tpu as pltpu

F32 = jnp.float32
BF16 = jnp.bfloat16

D_MODEL = 1024
CHUNK = 64
HEAD_DIM = 64
A_HEADS = 6
A_LEFT_CHUNKS = 8
A_REL_MAX = 128
B_HEADS = 5
IDX_HEADS = 8
IDX_DIM = 32
TOPK_MAX = 256
C_HEADS = 5
C_Q_RANK = 384
C_KV_RANK = 256
C_NOPE = 64
C_ROPE = 32
C_V = 64
D_FF = 2816
ROPE_THETA = 10000.0
EPS = 1e-6
NEG = -1e30
LOG2E = math.log2(math.e)

LANES = 128
A_W = A_HEADS * HEAD_DIM
B_W = B_HEADS * HEAD_DIM
B_WP = 384
C_QW = C_HEADS * LANES
IQ_W = IDX_HEADS * IDX_DIM
VMEM_LIMIT = 56 * 1024 * 1024

OFF_AQ, OFF_AK, OFF_AV, OFF_BV, OFF_CQ, OFF_CKV, OFF_IW = 0, 384, 768, 1152, 1536, 1920, 2176
OFF_ROPE = 2304
ROPE_W = 384 + 384 + 256 + 256 + 128
R_BQ, R_BK, R_IQ, R_IK, R_KR = 0, 384, 768, 1024, 1280
W_CAT = OFF_ROPE + ROPE_W


INT_MIN = -(2 ** 31)


def _dot(a, b):
    return jnp.dot(a, b, preferred_element_type=F32)


def _dot_t(a, b):
    return lax.dot_general(a, b, (((1,), (1,)), ((), ())), preferred_element_type=F32)


def _rms(x, g):
    ms = jnp.mean(x * x, axis=-1, keepdims=True)
    return x * lax.rsqrt(ms + EPS) * g


def _layer(l, *tail):
    return pl.BlockSpec((None,) + tail, lambda *_: (l,) + (0,) * len(tail))


def _ffn_kernel(*refs, has_attn, has_final, tf):
    it = iter(refs)
    x_ref = next(it)
    if has_attn:
        oa_ref, ob_ref, oc_ref, wo_ref = next(it), next(it), next(it), next(it)
    g_ref, wg_ref, wu_ref, wd_ref = next(it), next(it), next(it), next(it)
    fg_ref = next(it) if has_final else None
    o_ref = next(it)
    acc_ref = next(it)

    x = x_ref[...]
    if has_attn:
        x = x + _dot(oa_ref[0], wo_ref[0:384, :])
        x = x + _dot(ob_ref[0], wo_ref[384:768, :])
        x = x + _dot(oc_ref[0], wo_ref[768:1152, :])
    xn = _rms(x, g_ref[...]).astype(BF16)
    for j in range(D_FF // tf):
        g = _dot(xn, wg_ref[:, j * tf:(j + 1) * tf])
        u = _dot(xn, wu_ref[:, j * tf:(j + 1) * tf])
        h = (g * jax.nn.sigmoid(g) * u).astype(BF16)
        c = _dot(h, wd_ref[j * tf:(j + 1) * tf, :])
        if j == 0:
            acc_ref[...] = c
        else:
            acc_ref[...] += c
    y = x + 0.5 * acc_ref[...]
    if has_final:
        y = _rms(y, fg_ref[...])
    o_ref[...] = y


def _ffn(x2, attn, wo, g, wg, wu, wd, final_g, l, *, tm=512, tf=256):
    n = x2.shape[0]
    has_attn = attn is not None
    has_final = final_g is not None
    const = lambda i: (0, 0)
    row = lambda i: (i, 0)
    in_specs = [pl.BlockSpec((tm, D_MODEL), row)]
    args = [x2]
    if has_attn:
        s = attn[0].shape[1]
        nt = s // tm
        amap = lambda i: (i // nt, i % nt, 0)
        for a in attn:
            in_specs.append(pl.BlockSpec((1, tm, 384), amap))
            args.append(a)
        in_specs.append(_layer(l, 1152, D_MODEL))
        args.append(wo)
    in_specs += [_layer(l, 1, D_MODEL), _layer(l, D_MODEL, D_FF), _layer(l, D_MODEL, D_FF),
                 _layer(l, D_FF, D_MODEL)]
    args += [g, wg, wu, wd]
    if has_final:
        in_specs.append(pl.BlockSpec((1, D_MODEL), const))
        args.append(final_g)
    return pl.pallas_call(
        functools.partial(_ffn_kernel, has_attn=has_attn, has_final=has_final, tf=tf),
        grid=(n // tm,),
        in_specs=in_specs,
        out_specs=pl.BlockSpec((tm, D_MODEL), row),
        out_shape=jax.ShapeDtypeStruct((n, D_MODEL), F32),
        scratch_shapes=[pltpu.VMEM((tm, D_MODEL), F32)],
        compiler_params=pltpu.CompilerParams(
            dimension_semantics=("arbitrary",), vmem_limit_bytes=VMEM_LIMIT),
        name="ffn",
    )(*args)


def _proj_kernel(x_ref, g_ref, w_ref, tab_ref, qn_ref, kvn_ref, wuq_ref, wukn_ref, wuv_ref,
                 aq_ref, ak_ref, av_ref, bq_ref, bk_ref, bv_ref, iq_ref, ik_ref, iw_ref,
                 cq_ref, ck_ref, cv_ref):
    xn = _rms(x_ref[...], g_ref[...]).astype(BF16)

    def mm(off, n):
        return _dot(xn, w_ref[:, off:off + n])

    aq_ref[0] = mm(OFF_AQ, 384).astype(BF16)
    ak_ref[0] = mm(OFF_AK, 384).astype(BF16)
    av_ref[0] = mm(OFF_AV, 384).astype(BF16)
    bv_ref[0] = mm(OFF_BV, 384).astype(BF16)
    iw_ref[0] = mm(OFF_IW, 128)[:, :IDX_HEADS]

    cos64, sin64 = tab_ref[:, 0:128], tab_ref[:, 128:256]
    cos32, sin32 = tab_ref[:, 256:384], tab_ref[:, 384:512]
    cosc, sinc = tab_ref[:, 512:640], tab_ref[:, 640:768]
    lane = lax.broadcasted_iota(jnp.int32, (x_ref.shape[0], LANES), 1)

    def rope(x, d, cos, sin):
        partner = jnp.where(lane % d < d // 2, pltpu.roll(x, LANES - d // 2, axis=1),
                            pltpu.roll(x, d // 2, axis=1))
        return x * cos + partner * sin

    def rope_group(r_off, n, d, cos, sin):
        m = mm(OFF_ROPE + r_off, n)
        return [rope(m[:, c * 128:(c + 1) * 128], d, cos, sin) for c in range(n // 128)]

    for c, v in enumerate(rope_group(R_BQ, 384, HEAD_DIM, cos64, sin64)):
        bq_ref[0, :, c * 128:(c + 1) * 128] = v.astype(BF16)
    for c, v in enumerate(rope_group(R_BK, 384, HEAD_DIM, cos64, sin64)):
        bk_ref[0, :, c * 128:(c + 1) * 128] = v.astype(BF16)
    for c, v in enumerate(rope_group(R_IQ, 256, IDX_DIM, cos32, sin32)):
        iq_ref[0, :, c * 128:(c + 1) * 128] = v.astype(BF16)
    for c, v in enumerate(rope_group(R_IK, 256, IDX_DIM, cos32, sin32)):
        ik_ref[0, :, c * 128:(c + 1) * 128] = v.astype(BF16)
    krg = rope_group(R_KR, 128, C_ROPE, cosc, sinc)[0]

    cqn = _rms(mm(OFF_CQ, C_Q_RANK), qn_ref[...]).astype(BF16)
    qm = _dot(cqn, wuq_ref[...])
    ckn = _rms(mm(OFF_CKV, C_KV_RANK), kvn_ref[...]).astype(BF16)
    kn = _dot(ckn, wukn_ref[...])
    for h in range(C_HEADS):
        sl = slice(h * 128, (h + 1) * 128)
        cq_ref[0, :, sl] = rope(qm[:, sl], C_ROPE, cosc, sinc).astype(BF16)
        ck_ref[0, :, sl] = (kn[:, sl] + krg).astype(BF16)
    cv_ref[0] = _dot(ckn, wuv_ref[...]).astype(BF16)


def _proj(x2, bsz, seq, g, wcat, tab, qn, kvn, wuq, wukn, wuv, l, *, tm=512):
    nt = seq // tm
    const = lambda i: (0, 0)
    omap = lambda i: (i // nt, i % nt, 0)
    widths = [384, 384, 384, 384, 384, 384, IQ_W, IQ_W, IDX_HEADS, C_QW, C_QW, 384]
    dtypes = [BF16] * 8 + [F32] + [BF16] * 3
    return pl.pallas_call(
        _proj_kernel,
        grid=(bsz * nt,),
        in_specs=[pl.BlockSpec((tm, D_MODEL), lambda i: (i, 0)),
                  _layer(l, 1, D_MODEL),
                  _layer(l, D_MODEL, W_CAT),
                  pl.BlockSpec((tm, 768), lambda i: (i % nt, 0)),
                  _layer(l, 1, C_Q_RANK),
                  _layer(l, 1, C_KV_RANK),
                  _layer(l, C_Q_RANK, C_QW),
                  _layer(l, C_KV_RANK, C_QW),
                  _layer(l, C_KV_RANK, 384)],
        out_specs=[pl.BlockSpec((1, tm, w), omap) for w in widths],
        out_shape=[jax.ShapeDtypeStruct((bsz, seq, w), d) for w, d in zip(widths, dtypes)],
        compiler_params=pltpu.CompilerParams(
            dimension_semantics=("arbitrary",), vmem_limit_bytes=VMEM_LIMIT),
        name="proj",
    )(x2, g, wcat, tab, qn, kvn, wuq, wukn, wuv)


A_QB = 2 * CHUNK
A_WIN = (A_LEFT_CHUNKS + 2) * CHUNK
A_PREV = A_LEFT_CHUNKS * CHUNK


def _chunk_attn_kernel(q_ref, kp_ref, kc_ref, vp_ref, vc_ref, bias_ref, o_ref, kcat_ref, vcat_ref,
                       *, tq):
    i = pl.program_id(1)
    kcat_ref[0:A_PREV, :] = kp_ref[0]
    kcat_ref[A_PREV:A_PREV + tq, :] = kc_ref[0]
    vcat_ref[0:A_PREV, :] = vp_ref[0]
    vcat_ref[A_PREV:A_PREV + tq, :] = vc_ref[0]
    lane = lax.broadcasted_iota(jnp.int32, (A_QB, LANES), 1)
    col_chunk = lax.broadcasted_iota(jnp.int32, (2 * A_QB, A_WIN), 1) // CHUNK
    for qb in range(tq // A_QB):
        first_chunk = i * (tq // CHUNK) + 2 * qb - A_LEFT_CHUNKS
        valid = (col_chunk + first_chunk) >= 0
        for p in range(A_HEADS // 2):
            ls = slice(p * LANES, (p + 1) * LANES)
            q = q_ref[0, qb * A_QB:(qb + 1) * A_QB, ls]
            qs = jnp.concatenate([jnp.where(lane < HEAD_DIM, q, jnp.zeros_like(q)),
                                  jnp.where(lane >= HEAD_DIM, q, jnp.zeros_like(q))], axis=0)
            kw = kcat_ref[qb * A_QB:qb * A_QB + A_WIN, ls]
            vw = vcat_ref[qb * A_QB:qb * A_QB + A_WIN, ls]
            s = _dot_t(qs, kw) + bias_ref[p]
            s = jnp.where(valid, s, NEG)
            m = jnp.max(s, axis=1, keepdims=True)
            e = jnp.exp(s - m)
            l = jnp.sum(e, axis=1, keepdims=True)
            o = _dot(e.astype(BF16), vw) / l
            o_ref[0, qb * A_QB:(qb + 1) * A_QB, ls] = jnp.where(
                lane < HEAD_DIM, o[0:A_QB], o[A_QB:2 * A_QB]).astype(BF16)


def _chunk_attn(q, k, v, bias, l, *, tq=512):
    bsz, seq, _ = q.shape
    assert tq == A_PREV and seq % tq == 0
    cur = lambda b, i: (b, i, 0)
    prev = lambda b, i: (b, jnp.maximum(i - 1, 0), 0)
    blk = (1, tq, A_W)
    return pl.pallas_call(
        functools.partial(_chunk_attn_kernel, tq=tq),
        grid=(bsz, seq // tq),
        in_specs=[pl.BlockSpec(blk, cur), pl.BlockSpec(blk, prev), pl.BlockSpec(blk, cur),
                  pl.BlockSpec(blk, prev), pl.BlockSpec(blk, cur),
                  _layer(l, A_HEADS // 2, 2 * A_QB, A_WIN)],
        out_specs=pl.BlockSpec(blk, cur),
        out_shape=jax.ShapeDtypeStruct((bsz, seq, A_W), BF16),
        scratch_shapes=[pltpu.VMEM((A_PREV + tq, A_W), BF16), pltpu.VMEM((A_PREV + tq, A_W), BF16)],
        compiler_params=pltpu.CompilerParams(
            dimension_semantics=("arbitrary", "arbitrary"), vmem_limit_bytes=VMEM_LIMIT),
        name="chunk_attn",
    )(q, k, k, v, v, bias)


def _fold(op, acc, s):
    for c in range(s.shape[1] // LANES):
        acc = op(acc, s[:, c * LANES:(c + 1) * LANES])
    return acc


def _row_all_lanes(op, x):
    return jnp.broadcast_to(op(x, axis=1, keepdims=True), x.shape)


def _pair_select(even_head, odd_head):
    lane = lax.broadcasted_iota(jnp.int32, even_head.shape, 1)
    return jnp.where(lane < HEAD_DIM, even_head, odd_head)


def _write_heads(o_ref, acc_ref, l_ref, n_heads):
    for p in range((n_heads + 1) // 2):
        o0 = acc_ref[2 * p] / _row_all_lanes(jnp.sum, l_ref[2 * p])
        if 2 * p + 1 < n_heads:
            o1 = acc_ref[2 * p + 1] / _row_all_lanes(jnp.sum, l_ref[2 * p + 1])
        else:
            o1 = jnp.zeros_like(o0)
        o_ref[0, :, p * LANES:(p + 1) * LANES] = _pair_select(o0, o1).astype(BF16)


def _resident(shape, index_map):
    return pl.BlockSpec(shape, index_map, pipeline_mode=pl.Buffered(1))


def _mla_kernel(q_ref, k_ref, v_ref, o_ref, s0_ref, s1_ref, m_ref, l_ref, acc_ref, *, tq):
    q0 = pl.multiple_of(pl.program_id(1) * tq, tq)
    nfull = q0 // tq
    diag_mask = (lax.broadcasted_iota(jnp.int32, (tq, tq), 1) // CHUNK
                 <= lax.broadcasted_iota(jnp.int32, (tq, tq), 0) // CHUNK)

    def produce(kb, s_ref):
        k0 = pl.multiple_of(kb * tq, tq)
        for h in range(C_HEADS):
            hs = slice(h * LANES, (h + 1) * LANES)
            s_ref[h] = _dot_t(q_ref[0, :, hs], k_ref[0, pl.ds(k0, tq), hs])

    def consume(kb, s_ref, masked):
        k0 = pl.multiple_of(kb * tq, tq)
        for h in range(C_HEADS):
            ps = slice((h // 2) * LANES, (h // 2 + 1) * LANES)
            s = s_ref[h]
            if masked:
                s = jnp.where(diag_mask, s, NEG)
            m_old = m_ref[h]
            m_new = jnp.maximum(m_old, _row_all_lanes(
                jnp.max, _fold(jnp.maximum, s[:, :LANES], s[:, LANES:])))
            alpha = jnp.exp2(m_old - m_new)
            p = jnp.exp2(s - jnp.tile(m_new, (1, tq // LANES)))
            l_ref[h] = alpha * l_ref[h] + _fold(jnp.add, p[:, :LANES], p[:, LANES:])
            acc_ref[h] = alpha * acc_ref[h] + _dot(p.astype(BF16), v_ref[0, pl.ds(k0, tq), ps])
            m_ref[h] = m_new

    m_ref[...] = jnp.full(m_ref.shape, NEG, F32)
    l_ref[...] = jnp.zeros(l_ref.shape, F32)
    acc_ref[...] = jnp.zeros(acc_ref.shape, F32)

    produce(0, s0_ref)

    def body(j, carry):
        produce(2 * j + 1, s1_ref)
        consume(2 * j, s0_ref, False)
        produce(2 * j + 2, s0_ref)
        consume(2 * j + 1, s1_ref, False)
        return carry
    lax.fori_loop(0, nfull // 2, body, 0)

    @pl.when(nfull % 2 == 0)
    def _():
        consume(nfull, s0_ref, True)

    @pl.when(nfull % 2 == 1)
    def _():
        produce(nfull, s1_ref)
        consume(nfull - 1, s0_ref, False)
        consume(nfull, s1_ref, True)
    _write_heads(o_ref, acc_ref, l_ref, C_HEADS)


def _mla(q, k, v, *, tq=512):
    bsz, seq, _ = q.shape
    assert seq % tq == 0
    stat = pltpu.VMEM((C_HEADS, tq, LANES), F32)
    return pl.pallas_call(
        functools.partial(_mla_kernel, tq=tq),
        grid=(bsz, seq // tq),
        in_specs=[pl.BlockSpec((1, tq, C_QW), lambda b, i: (b, i, 0)),
                  _resident((1, seq, C_QW), lambda b, i: (b, 0, 0)),
                  _resident((1, seq, 384), lambda b, i: (b, 0, 0))],
        out_specs=pl.BlockSpec((1, tq, 384), lambda b, i: (b, i, 0)),
        out_shape=jax.ShapeDtypeStruct((bsz, seq, 384), BF16),
        scratch_shapes=[pltpu.VMEM((C_HEADS, tq, tq), F32), pltpu.VMEM((C_HEADS, tq, tq), F32),
                        stat, stat, stat],
        compiler_params=pltpu.CompilerParams(
            dimension_semantics=("arbitrary", "arbitrary"), vmem_limit_bytes=VMEM_LIMIT),
        name="mla",
    )(q, k, v)


GROUP_COLS = 32
I32_ALL = -1


def _i32(x):
    return int(np.uint32(x).view(np.int32))


def _bit_planes(words):
    a = list(reversed(words))
    j, m = 16, 0x0000FFFF
    while j:
        k = 0
        while k < 32:
            t = (a[k] ^ lax.shift_right_logical(a[k + j], jnp.int32(j))) & jnp.int32(_i32(m))
            a[k] = a[k] ^ t
            a[k + j] = a[k + j] ^ lax.shift_left(t, jnp.int32(j))
            k = (k + j + 1) & ~j
        j >>= 1
        m = (m ^ (m << j)) & 0xFFFFFFFF
    return list(reversed(a))


def _dsa_kernel(iq_ref, iw_ref, ik_ref, q_ref, k_ref, v_ref, o_ref,
                key_ref, qm_ref, cand_ref, great_ref, sel_ref, m_ref, l_ref, acc_ref,
                *, tq, tk, topk):
    q0 = pl.program_id(1) * tq
    nkb = (q0 + tq + tk - 1) // tk
    cols_per_blk = tk // LANES
    blks_per_group = GROUP_COLS // cols_per_blk
    n_groups = key_ref.shape[0] // blks_per_group
    row_chunk = (q0 + lax.broadcasted_iota(jnp.int32, (tq, tk), 0)) // CHUNK
    col_chunk0 = lax.broadcasted_iota(jnp.int32, (tq, tk), 1) // CHUNK

    def admissible(kb):
        return (col_chunk0 + kb * (tk // CHUNK)) <= row_chunk

    iq = iq_ref[0]
    head_of_lane = lax.broadcasted_iota(jnp.int32, iq.shape, 1) // IDX_DIM
    for h in range(IDX_HEADS):
        qm_ref[h * tq:(h + 1) * tq, :] = jnp.where(head_of_lane == h, iq, jnp.zeros_like(iq))
    w = iw_ref[0]

    sub = 256

    def score_body(kb, carry):
        adm = admissible(kb)
        for c in range(tk // sub):
            k0 = pl.multiple_of(kb * tk + c * sub, sub)
            d = _dot_t(qm_ref[...], ik_ref[0, pl.ds(k0, sub), :])
            sc = jnp.zeros((tq, sub), F32)
            for h in range(IDX_HEADS):
                sc = sc + jnp.maximum(d[h * tq:(h + 1) * tq], 0.0) * w[:, h:h + 1]
            sc = jnp.where(adm[:, c * sub:(c + 1) * sub], sc, NEG)
            bits = pltpu.bitcast(sc, jnp.int32)
            key_ref[kb, :, c * sub:(c + 1) * sub] = bits ^ ((bits >> 31) | jnp.int32(INT_MIN))
        return carry

    lax.fori_loop(0, nkb, score_body, 0)

    def clear_body(kb, carry):
        key_ref[kb] = jnp.zeros((tq, tk), jnp.int32)
        return carry
    n_active = (nkb + blks_per_group - 1) // blks_per_group
    lax.fori_loop(nkb, n_active * blks_per_group, clear_body, 0)

    def slot(g, j):
        return g * blks_per_group + j // cols_per_blk, slice((j % cols_per_blk) * LANES,
                                                             (j % cols_per_blk + 1) * LANES)

    for g in range(n_groups):
        @pl.when(nkb > g * blks_per_group)
        def _(g=g):
            def transpose_rows(rg, carry):
                rows = pl.ds(pl.multiple_of(rg * 8, 8), 8)
                cols = [slot(g, j) for j in range(GROUP_COLS)]
                planes = _bit_planes([key_ref[blk, rows, ls] for blk, ls in cols])
                for (blk, ls), p in zip(cols, planes):
                    key_ref[blk, rows, ls] = p
                return carry
            lax.fori_loop(0, tq // 8, transpose_rows, 0)

    for g in range(n_groups):
        n_cols = jnp.clip((nkb - g * blks_per_group) * cols_per_blk, 0, GROUP_COLS)
        word = jnp.where(n_cols >= GROUP_COLS, jnp.int32(I32_ALL),
                         lax.shift_left(jnp.int32(1), jnp.minimum(n_cols, GROUP_COLS - 1)) - 1)
        cand_ref[g] = jnp.full((tq, LANES), word, jnp.int32)
        great_ref[g] = jnp.zeros((tq, LANES), jnp.int32)

    def refine(plane_of, need, prefer_set, groups=n_groups):
        picked = []
        for g in range(groups):
            p = plane_of(g)
            picked.append(cand_ref[g] & (p if prefer_set else ~p))
        cnt = lax.population_count(picked[0])
        for x in picked[1:]:
            cnt = cnt + lax.population_count(x)
        c = _row_all_lanes(jnp.sum, cnt.astype(F32))
        keep = c >= need
        for g in range(groups):
            great_ref[g] = jnp.where(keep, great_ref[g], great_ref[g] | picked[g])
            cand_ref[g] = jnp.where(keep, picked[g], cand_ref[g] ^ picked[g])
        return jnp.where(keep, need, need - c)

    def value_bits(groups):
        def body(i, need):
            blk = blks_per_group - 1 - i
            for c in reversed(range(cols_per_blk)):
                ls = slice(c * LANES, (c + 1) * LANES)
                need = refine(lambda g: key_ref[g * blks_per_group + blk, :, ls], need, True, groups)
            return need
        return lambda need: lax.fori_loop(0, blks_per_group, body, need)

    need = lax.switch(n_active - 1, [value_bits(g + 1) for g in range(n_groups)],
                      jnp.full((tq, LANES), float(topk), F32))

    def index_bits(need):
        lane = lax.broadcasted_iota(jnp.int32, (tq, LANES), 1)
        col_bit_words = (0xAAAAAAAA, 0xCCCCCCCC, 0xF0F0F0F0, 0xFF00FF00, 0xFFFF0000)
        n_group_bits = max(n_groups - 1, 0).bit_length()
        for t in reversed(range(n_group_bits)):
            need = refine(lambda g: jnp.full((tq, LANES), I32_ALL if (g >> t) & 1 else 0, jnp.int32),
                          need, False)
        for t in reversed(range(5)):
            need = refine(lambda g: jnp.full((tq, LANES), _i32(col_bit_words[t]), jnp.int32),
                          need, False)
        for t in reversed(range(7)):
            need = refine(lambda g: jnp.where((lane >> t) & 1 == 1, jnp.int32(I32_ALL), 0),
                          need, False)
        return need

    n_cand = cand_ref[0] * 0
    for g in range(n_groups):
        n_cand = n_cand + lax.population_count(cand_ref[g])
    has_tie = jnp.max(jnp.where(_row_all_lanes(jnp.sum, n_cand.astype(F32)) > need, 1, 0)) > 0
    lax.cond(has_tie, index_bits, lambda need: need, need)
    for g in range(n_groups):
        sel_ref[g] = great_ref[g] | cand_ref[g]

    q = q_ref[0]
    qlane = lax.broadcasted_iota(jnp.int32, (tq, LANES), 1)

    def scores(h, kb, bias):
        ps = slice((h // 2) * LANES, (h // 2 + 1) * LANES)
        qp = q[:, ps]
        qh = jnp.where((qlane < HEAD_DIM) if h % 2 == 0 else (qlane >= HEAD_DIM),
                       qp, jnp.zeros_like(qp))
        k0 = pl.multiple_of(kb * tk, tk)
        return _dot_t(qh, k_ref[0, pl.ds(k0, tk), ps]) + bias

    m_ref[...] = jnp.full(m_ref.shape, NEG, F32)

    def max_body(kb, carry):
        words = sel_ref[kb // blks_per_group]
        first_col = (kb % blks_per_group) * cols_per_blk
        taken = jnp.concatenate(
            [lax.shift_right_logical(words, jnp.full(words.shape, first_col + c, jnp.int32)) & 1
             for c in range(cols_per_blk)], axis=1)
        bias = jnp.where(taken == 1, jnp.where(admissible(kb), 0.0, NEG), NEG)
        key_ref[kb] = pltpu.bitcast(bias, jnp.int32)
        for h in range(B_HEADS):
            m_ref[h] = _fold(jnp.maximum, m_ref[h], scores(h, kb, bias))
        return carry

    lax.fori_loop(0, nkb, max_body, 0)

    for h in range(B_HEADS):
        m_ref[h] = _row_all_lanes(jnp.max, m_ref[h])
    l_ref[...] = jnp.zeros(l_ref.shape, F32)
    acc_ref[...] = jnp.zeros(acc_ref.shape, F32)

    def acc_body(kb, carry):
        bias = pltpu.bitcast(key_ref[kb], F32)
        k0 = pl.multiple_of(kb * tk, tk)
        for h in range(B_HEADS):
            ps = slice((h // 2) * LANES, (h // 2 + 1) * LANES)
            p = jnp.exp2(scores(h, kb, bias) - jnp.tile(m_ref[h], (1, tk // LANES)))
            l_ref[h] = _fold(jnp.add, l_ref[h], p)
            acc_ref[h] += _dot(p.astype(BF16), v_ref[0, pl.ds(k0, tk), ps])
        return carry

    lax.fori_loop(0, nkb, acc_body, 0)
    _write_heads(o_ref, acc_ref, l_ref, B_HEADS)


def _dsa(iq, iw, ik, q, k, v, *, tq=512, tk=512):
    bsz, seq, _ = q.shape
    assert seq % tq == 0 and seq % tk == 0 and GROUP_COLS % (tk // LANES) == 0
    topk = min(TOPK_MAX, seq // 4)
    blks_per_group = GROUP_COLS // (tk // LANES)
    n_groups = -(-(seq // tk) // blks_per_group)
    qmap = lambda b, i: (b, i, 0)
    full = lambda b, i: (b, 0, 0)
    stat = pltpu.VMEM((B_HEADS, tq, LANES), F32)
    words = pltpu.VMEM((n_groups, tq, LANES), jnp.int32)
    return pl.pallas_call(
        functools.partial(_dsa_kernel, tq=tq, tk=tk, topk=topk),
        grid=(bsz, seq // tq),
        in_specs=[pl.BlockSpec((1, tq, IQ_W), qmap),
                  pl.BlockSpec((1, tq, IDX_HEADS), qmap),
                  _resident((1, seq, IQ_W), full),
                  pl.BlockSpec((1, tq, B_WP), qmap),
                  _resident((1, seq, B_WP), full),
                  _resident((1, seq, B_WP), full)],
        out_specs=pl.BlockSpec((1, tq, B_WP), qmap),
        out_shape=jax.ShapeDtypeStruct((bsz, seq, B_WP), BF16),
        scratch_shapes=[pltpu.VMEM((n_groups * blks_per_group, tq, tk), jnp.int32),
                        pltpu.VMEM((IDX_HEADS * tq, IQ_W), BF16),
                        words, words, words,
                        stat, stat, stat],
        compiler_params=pltpu.CompilerParams(
            dimension_semantics=("arbitrary", "arbitrary"), vmem_limit_bytes=VMEM_LIMIT),
        name="dsa",
    )(iq, iw, ik, q, k, v)


def _pad_cols(w, n):
    return jnp.pad(w, ((0, 0), (0, n - w.shape[1])))


def _prep_proj_weights(w_in, c_w_uq, c_w_ukv):
    offs = np.cumsum([0, A_W, A_W, A_W, B_W, B_W, B_W, IQ_W, IDX_DIM, IDX_HEADS,
                      C_Q_RANK, C_KV_RANK, C_ROPE])
    (aq, ak, av, bq, bk, bv, iq, ik, iw, cq, ckv, ckr) = [
        w_in[:, offs[j]:offs[j + 1]] for j in range(12)]
    ascale = HEAD_DIM ** -0.5
    bscale = HEAD_DIM ** -0.5 * LOG2E
    zeros64 = jnp.zeros((D_MODEL, 64), F32)
    zeros32 = jnp.zeros((D_MODEL, 32), F32)
    kr_group = jnp.concatenate([zeros64, ckr, zeros32], axis=1)
    plain = [aq * ascale, ak, av, _pad_cols(bv, B_WP), cq, ckv,
             _pad_cols(iw * (IDX_HEADS * IDX_DIM) ** -0.5, 128)]
    rope_main = [_pad_cols(bq * bscale, B_WP), _pad_cols(bk, B_WP), iq, jnp.tile(ik, (1, IDX_HEADS)),
                 kr_group]
    wcat = jnp.concatenate(plain + rope_main, axis=1).astype(BF16)

    cscale = (C_NOPE + C_ROPE) ** -0.5 * LOG2E
    uq = c_w_uq.reshape(C_Q_RANK, C_HEADS, C_NOPE + C_ROPE) * cscale
    zq = jnp.zeros((C_Q_RANK, C_HEADS, 32), F32)
    wuq = jnp.concatenate([uq, zq], axis=-1).reshape(C_Q_RANK, C_QW).astype(BF16)
    ukv = c_w_ukv.reshape(C_KV_RANK, C_HEADS, C_NOPE + C_V)
    wukn = jnp.concatenate([ukv[..., :C_NOPE], jnp.zeros((C_KV_RANK, C_HEADS, 64), F32)],
                           axis=-1).reshape(C_KV_RANK, C_QW).astype(BF16)
    wuv = _pad_cols(ukv[..., C_NOPE:].reshape(C_KV_RANK, C_HEADS * C_V), 384).astype(BF16)
    return wcat, wuq, wukn, wuv


def _rope_tables(seq):
    pos = jnp.arange(seq, dtype=F32)[:, None]

    def cs(d):
        inv = ROPE_THETA ** (-jnp.arange(0, d, 2, dtype=F32) / d)
        ang = pos * inv[None, :]
        c, s = jnp.cos(ang), jnp.sin(ang)
        return (jnp.tile(jnp.concatenate([c, c], axis=1), (1, LANES // d)),
                jnp.tile(jnp.concatenate([-s, s], axis=1), (1, LANES // d)))

    c64, s64 = cs(HEAD_DIM)
    c32, s32 = cs(C_ROPE)
    lane = jnp.arange(LANES)[None, :]
    roped = (lane >= C_NOPE) & (lane < C_NOPE + C_ROPE)
    cc = jnp.where(roped, c32, 1.0)
    sc = jnp.where(roped, s32, 0.0)
    return jnp.concatenate([c64, s64, c32, s32, cc, sc], axis=1)


def _chunk_bias(rel_bias):
    rb = rel_bias.astype(F32)
    n_rel = A_QB + A_WIN - 1
    below = jnp.broadcast_to(rb[:, :1], (A_HEADS, A_QB - 1 - (CHUNK - 1)))
    above = jnp.broadcast_to(rb[:, -1:], (A_HEADS, A_PREV + A_QB - 1 - A_REL_MAX))
    by_rel_desc = jnp.concatenate([below, rb, above], axis=1)[:, ::-1]
    assert by_rel_desc.shape[1] == n_rel
    v = jnp.roll(by_rel_desc, -(A_QB - 1), axis=1)
    flat = jnp.broadcast_to(v[:, None, :], (A_HEADS, A_QB, n_rel)).reshape(A_HEADS, A_QB * n_rel)
    toeplitz = flat[:, :A_QB * (n_rel - 1)].reshape(A_HEADS, A_QB, n_rel - 1)[:, :, :A_WIN]
    r = np.arange(A_QB)
    ki = np.arange(A_WIN)[None, :] - CHUNK * (r // CHUNK)[:, None]
    inwin = (ki >= 0) & (ki < (A_LEFT_CHUNKS + 1) * CHUNK)
    b = jnp.where(jnp.asarray(inwin)[None], toeplitz, NEG)
    return b.reshape(A_HEADS // 2, 2 * A_QB, A_WIN)


def _prep_wo(w_out):
    pad64 = jnp.zeros((64, D_MODEL), F32)
    return jnp.concatenate([w_out[:A_W], w_out[A_W:A_W + B_W], pad64,
                            w_out[A_W + B_W:], pad64], axis=0).astype(BF16)


def kernel(x, ffn1_norm, ffn1_w_gate, ffn1_w_up, ffn1_w_down, mix_norm, w_in, a_rel_bias, c_q_norm,
           c_kv_norm, c_w_uq, c_w_ukv, w_out, ffn2_norm, ffn2_w_gate, ffn2_w_up, ffn2_w_down,
           final_norm):
    bsz, seq, _ = x.shape
    depth = w_in.shape[0]
    tab = _rope_tables(seq)
    f1g, f1u, f1d = ffn1_w_gate.astype(BF16), ffn1_w_up.astype(BF16), ffn1_w_down.astype(BF16)
    f2g, f2u, f2d = ffn2_w_gate.astype(BF16), ffn2_w_up.astype(BF16), ffn2_w_down.astype(BF16)
    wcat, wuq, wukn, wuv = jax.vmap(_prep_proj_weights)(w_in, c_w_uq, c_w_ukv)
    abias = jax.vmap(_chunk_bias)(a_rel_bias)
    wo = jax.vmap(_prep_wo)(w_out)
    norms = [n[:, None, :] for n in (ffn1_norm, mix_norm, c_q_norm, c_kv_norm, ffn2_norm)]
    n_ffn1, n_mix, n_cq, n_ckv, n_ffn2 = norms
    x2 = x.reshape(bsz * seq, D_MODEL)
    for l in range(depth):
        x2 = _ffn(x2, None, None, n_ffn1, f1g, f1u, f1d, None, l)
        (aq, ak, av, bq, bk, bv, iq, ik, iw, cq, ck, cv) = _proj(
            x2, bsz, seq, n_mix, wcat, tab, n_cq, n_ckv, wuq, wukn, wuv, l)
        o_a = _chunk_attn(aq, ak, av, abias, l)
        o_b = _dsa(iq, iw, ik, bq, bk, bv)
        o_c = _mla(cq, ck, cv)
        x2 = _ffn(x2, (o_a, o_b, o_c), wo, n_ffn2, f2g, f2u, f2d,
                  final_norm[None] if l == depth - 1 else None, l)
    return x2.reshape(bsz, seq, D_MODEL)
```

```python
import functools
import math

import numpy as np
import jax
import jax.numpy as jnp
from jax import lax
from jax.experimental import pallas as pl
from jax.experimental.pallas import tpu as pltpu

F32 = jnp.float32
BF16 = jnp.bfloat16

D_MODEL = 1024
CHUNK = 64
HEAD_DIM = 64
A_HEADS = 6
A_LEFT_CHUNKS = 8
A_REL_MAX = 128
B_HEADS = 5
IDX_HEADS = 8
IDX_DIM = 32
TOPK_MAX = 256
C_HEADS = 5
C_Q_RANK = 384
C_KV_RANK = 256
C_NOPE = 64
C_ROPE = 32
C_V = 64
D_FF = 2816
ROPE_THETA = 10000.0
EPS = 1e-6
NEG = -1e30
LOG2E = math.log2(math.e)

LANES = 128
A_W = A_HEADS * HEAD_DIM
B_W = B_HEADS * HEAD_DIM
B_WP = 384
C_QW = C_HEADS * LANES
IQ_W = IDX_HEADS * IDX_DIM
VMEM_LIMIT = 56 * 1024 * 1024

OFF_AQ, OFF_AK, OFF_AV, OFF_BV, OFF_CQ, OFF_CKV, OFF_IW = 0, 384, 768, 1152, 1536, 1920, 2176
OFF_ROPE = 2304
ROPE_W = 384 + 384 + 256 + 256 + 128
R_BQ, R_BK, R_IQ, R_IK, R_KR = 0, 384, 768, 1024, 1280
W_CAT = OFF_ROPE + ROPE_W


INT_MIN = -(2 ** 31)


def _dot(a, b):
    return jnp.dot(a, b, preferred_element_type=F32)


def _dot_t(a, b):
    return lax.dot_general(a, b, (((1,), (1,)), ((), ())), preferred_element_type=F32)


def _rms(x, g):
    ms = jnp.mean(x * x, axis=-1, keepdims=True)
    return x * lax.rsqrt(ms + EPS) * g


def _layer(l, *tail):
    return pl.BlockSpec((None,) + tail, lambda *_: (l,) + (0,) * len(tail))


def _ffn_kernel(*refs, has_attn, has_final, tf):
    it = iter(refs)
    x_ref = next(it)
    if has_attn:
        oa_ref, ob_ref, oc_ref, wo_ref = next(it), next(it), next(it), next(it)
    g_ref, wg_ref, wu_ref, wd_ref = next(it), next(it), next(it), next(it)
    fg_ref = next(it) if has_final else None
    o_ref = next(it)
    acc_ref = next(it)

    x = x_ref[...]
    if has_attn:
        x = x + _dot(oa_ref[0], wo_ref[0:384, :])
        x = x + _dot(ob_ref[0], wo_ref[384:768, :])
        x = x + _dot(oc_ref[0], wo_ref[768:1152, :])
    xn = _rms(x, g_ref[...]).astype(BF16)
    for j in range(D_FF // tf):
        g = _dot(xn, wg_ref[:, j * tf:(j + 1) * tf])
        u = _dot(xn, wu_ref[:, j * tf:(j + 1) * tf])
        h = (g * jax.nn.sigmoid(g) * u).astype(BF16)
        c = _dot(h, wd_ref[j * tf:(j + 1) * tf, :])
        if j == 0:
            acc_ref[...] = c
        else:
            acc_ref[...] += c
    y = x + 0.5 * acc_ref[...]
    if has_final:
        y = _rms(y, fg_ref[...])
    o_ref[...] = y


def _ffn(x2, attn, wo, g, wg, wu, wd, final_g, l, *, tm=512, tf=256):
    n = x2.shape[0]
    has_attn = attn is not None
    has_final = final_g is not None
    const = lambda i: (0, 0)
    row = lambda i: (i, 0)
    in_specs = [pl.BlockSpec((tm, D_MODEL), row)]
    args = [x2]
    if has_attn:
        s = attn[0].shape[1]
        nt = s // tm
        amap = lambda i: (i // nt, i % nt, 0)
        for a in attn:
            in_specs.append(pl.BlockSpec((1, tm, 384), amap))
            args.append(a)
        in_specs.append(_layer(l, 1152, D_MODEL))
        args.append(wo)
    in_specs += [_layer(l, 1, D_MODEL), _layer(l, D_MODEL, D_FF), _layer(l, D_MODEL, D_FF),
                 _layer(l, D_FF, D_MODEL)]
    args += [g, wg, wu, wd]
    if has_final:
        in_specs.append(pl.BlockSpec((1, D_MODEL), const))
        args.append(final_g)
    return pl.pallas_call(
        functools.partial(_ffn_kernel, has_attn=has_attn, has_final=has_final, tf=tf),
        grid=(n // tm,),
        in_specs=in_specs,
        out_specs=pl.BlockSpec((tm, D_MODEL), row),
        out_shape=jax.ShapeDtypeStruct((n, D_MODEL), F32),
        scratch_shapes=[pltpu.VMEM((tm, D_MODEL), F32)],
        compiler_params=pltpu.CompilerParams(
            dimension_semantics=("arbitrary",), vmem_limit_bytes=VMEM_LIMIT),
        name="ffn",
    )(*args)


def _proj_kernel(x_ref, g_ref, w_ref, tab_ref, qn_ref, kvn_ref, wuq_ref, wukn_ref, wuv_ref,
                 aq_ref, ak_ref, av_ref, bq_ref, bk_ref, bv_ref, iq_ref, ik_ref, iw_ref,
                 cq_ref, ck_ref, cv_ref):
    xn = _rms(x_ref[...], g_ref[...]).astype(BF16)

    def mm(off, n):
        return _dot(xn, w_ref[:, off:off + n])

    aq_ref[0] = mm(OFF_AQ, 384).astype(BF16)
    ak_ref[0] = mm(OFF_AK, 384).astype(BF16)
    av_ref[0] = mm(OFF_AV, 384).astype(BF16)
    bv_ref[0] = mm(OFF_BV, 384).astype(BF16)
    iw_ref[0] = mm(OFF_IW, 128)[:, :IDX_HEADS]

    cos64, sin64 = tab_ref[:, 0:128], tab_ref[:, 128:256]
    cos32, sin32 = tab_ref[:, 256:384], tab_ref[:, 384:512]
    cosc, sinc = tab_ref[:, 512:640], tab_ref[:, 640:768]
    lane = lax.broadcasted_iota(jnp.int32, (x_ref.shape[0], LANES), 1)

    def rope(x, d, cos, sin):
        partner = jnp.where(lane % d < d // 2, pltpu.roll(x, LANES - d // 2, axis=1),
                            pltpu.roll(x, d // 2, axis=1))
        return x * cos + partner * sin

    def rope_group(r_off, n, d, cos, sin):
        m = mm(OFF_ROPE + r_off, n)
        return [rope(m[:, c * 128:(c + 1) * 128], d, cos, sin) for c in range(n // 128)]

    for c, v in enumerate(rope_group(R_BQ, 384, HEAD_DIM, cos64, sin64)):
        bq_ref[0, :, c * 128:(c + 1) * 128] = v.astype(BF16)
    for c, v in enumerate(rope_group(R_BK, 384, HEAD_DIM, cos64, sin64)):
        bk_ref[0, :, c * 128:(c + 1) * 128] = v.astype(BF16)
    for c, v in enumerate(rope_group(R_IQ, 256, IDX_DIM, cos32, sin32)):
        iq_ref[0, :, c * 128:(c + 1) * 128] = v.astype(BF16)
    for c, v in enumerate(rope_group(R_IK, 256, IDX_DIM, cos32, sin32)):
        ik_ref[0, :, c * 128:(c + 1) * 128] = v.astype(BF16)
    krg = rope_group(R_KR, 128, C_ROPE, cosc, sinc)[0]

    cqn = _rms(mm(OFF_CQ, C_Q_RANK), qn_ref[...]).astype(BF16)
    qm = _dot(cqn, wuq_ref[...])
    ckn = _rms(mm(OFF_CKV, C_KV_RANK), kvn_ref[...]).astype(BF16)
    kn = _dot(ckn, wukn_ref[...])
    for h in range(C_HEADS):
        sl = slice(h * 128, (h + 1) * 128)
        cq_ref[0, :, sl] = rope(qm[:, sl], C_ROPE, cosc, sinc).astype(BF16)
        ck_ref[0, :, sl] = (kn[:, sl] + krg).astype(BF16)
    cv_ref[0] = _dot(ckn, wuv_ref[...]).astype(BF16)


def _proj(x2, bsz, seq, g, wcat, tab, qn, kvn, wuq, wukn, wuv, l, *, tm=512):
    nt = seq // tm
    const = lambda i: (0, 0)
    omap = lambda i: (i // nt, i % nt, 0)
    widths = [384, 384, 384, 384, 384, 384, IQ_W, IQ_W, IDX_HEADS, C_QW, C_QW, 384]
    dtypes = [BF16] * 8 + [F32] + [BF16] * 3
    return pl.pallas_call(
        _proj_kernel,
        grid=(bsz * nt,),
        in_specs=[pl.BlockSpec((tm, D_MODEL), lambda i: (i, 0)),
                  _layer(l, 1, D_MODEL),
                  _layer(l, D_MODEL, W_CAT),
                  pl.BlockSpec((tm, 768), lambda i: (i % nt, 0)),
                  _layer(l, 1, C_Q_RANK),
                  _layer(l, 1, C_KV_RANK),
                  _layer(l, C_Q_RANK, C_QW),
                  _layer(l, C_KV_RANK, C_QW),
                  _layer(l, C_KV_RANK, 384)],
        out_specs=[pl.BlockSpec((1, tm, w), omap) for w in widths],
        out_shape=[jax.ShapeDtypeStruct((bsz, seq, w), d) for w, d in zip(widths, dtypes)],
        compiler_params=pltpu.CompilerParams(
            dimension_semantics=("arbitrary",), vmem_limit_bytes=VMEM_LIMIT),
        name="proj",
    )(x2, g, wcat, tab, qn, kvn, wuq, wukn, wuv)


A_QB = 2 * CHUNK
A_WIN = (A_LEFT_CHUNKS + 2) * CHUNK
A_PREV = A_LEFT_CHUNKS * CHUNK


def _chunk_attn_kernel(q_ref, kp_ref, kc_ref, vp_ref, vc_ref, bias_ref, o_ref, kcat_ref, vcat_ref,
                       *, tq):
    i = pl.program_id(1)
    kcat_ref[0:A_PREV, :] = kp_ref[0]
    kcat_ref[A_PREV:A_PREV + tq, :] = kc_ref[0]
    vcat_ref[0:A_PREV, :] = vp_ref[0]
    vcat_ref[A_PREV:A_PREV + tq, :] = vc_ref[0]
    lane = lax.broadcasted_iota(jnp.int32, (A_QB, LANES), 1)
    col_chunk = lax.broadcasted_iota(jnp.int32, (2 * A_QB, A_WIN), 1) // CHUNK
    for qb in range(tq // A_QB):
        first_chunk = i * (tq // CHUNK) + 2 * qb - A_LEFT_CHUNKS
        valid = (col_chunk + first_chunk) >= 0
        for p in range(A_HEADS // 2):
            ls = slice(p * LANES, (p + 1) * LANES)
            q = q_ref[0, qb * A_QB:(qb + 1) * A_QB, ls]
            qs = jnp.concatenate([jnp.where(lane < HEAD_DIM, q, jnp.zeros_like(q)),
                                  jnp.where(lane >= HEAD_DIM, q, jnp.zeros_like(q))], axis=0)
            kw = kcat_ref[qb * A_QB:qb * A_QB + A_WIN, ls]
            vw = vcat_ref[qb * A_QB:qb * A_QB + A_WIN, ls]
            s = _dot_t(qs, kw) + bias_ref[p]
            s = jnp.where(valid, s, NEG)
            m = jnp.max(s, axis=1, keepdims=True)
            e = jnp.exp(s - m)
            l = jnp.sum(e, axis=1, keepdims=True)
            o = _dot(e.astype(BF16), vw) / l
            o_ref[0, qb * A_QB:(qb + 1) * A_QB, ls] = jnp.where(
                lane < HEAD_DIM, o[0:A_QB], o[A_QB:2 * A_QB]).astype(BF16)


def _chunk_attn(q, k, v, bias, l, *, tq=512):
    bsz, seq, _ = q.shape
    assert tq == A_PREV and seq % tq == 0
    cur = lambda b, i: (b, i, 0)
    prev = lambda b, i: (b, jnp.maximum(i - 1, 0), 0)
    blk = (1, tq, A_W)
    return pl.pallas_call(
        functools.partial(_chunk_attn_kernel, tq=tq),
        grid=(bsz, seq // tq),
        in_specs=[pl.BlockSpec(blk, cur), pl.BlockSpec(blk, prev), pl.BlockSpec(blk, cur),
                  pl.BlockSpec(blk, prev), pl.BlockSpec(blk, cur),
                  _layer(l, A_HEADS // 2, 2 * A_QB, A_WIN)],
        out_specs=pl.BlockSpec(blk, cur),
        out_shape=jax.ShapeDtypeStruct((bsz, seq, A_W), BF16),
        scratch_shapes=[pltpu.VMEM((A_PREV + tq, A_W), BF16), pltpu.VMEM((A_PREV + tq, A_W), BF16)],
        compiler_params=pltpu.CompilerParams(
            dimension_semantics=("arbitrary", "arbitrary"), vmem_limit_bytes=VMEM_LIMIT),
        name="chunk_attn",
    )(q, k, k, v, v, bias)


def _fold(op, acc, s):
    for c in range(s.shape[1] // LANES):
        acc = op(acc, s[:, c * LANES:(c + 1) * LANES])
    return acc


def _row_all_lanes(op, x):
    return jnp.broadcast_to(op(x, axis=1, keepdims=True), x.shape)


def _pair_select(even_head, odd_head):
    lane = lax.broadcasted_iota(jnp.int32, even_head.shape, 1)
    return jnp.where(lane < HEAD_DIM, even_head, odd_head)


def _write_heads(o_ref, acc_ref, l_ref, n_heads):
    for p in range((n_heads + 1) // 2):
        o0 = acc_ref[2 * p] / _row_all_lanes(jnp.sum, l_ref[2 * p])
        if 2 * p + 1 < n_heads:
            o1 = acc_ref[2 * p + 1] / _row_all_lanes(jnp.sum, l_ref[2 * p + 1])
        else:
            o1 = jnp.zeros_like(o0)
        o_ref[0, :, p * LANES:(p + 1) * LANES] = _pair_select(o0, o1).astype(BF16)


def _resident(shape, index_map):
    return pl.BlockSpec(shape, index_map, pipeline_mode=pl.Buffered(1))


def _mla_kernel(q_ref, k_ref, v_ref, o_ref, s0_ref, s1_ref, m_ref, l_ref, acc_ref, *, tq):
    q0 = pl.multiple_of(pl.program_id(1) * tq, tq)
    nfull = q0 // tq
    diag_mask = (lax.broadcasted_iota(jnp.int32, (tq, tq), 1) // CHUNK
                 <= lax.broadcasted_iota(jnp.int32, (tq, tq), 0) // CHUNK)

    def produce(kb, s_ref):
        k0 = pl.multiple_of(kb * tq, tq)
        for h in range(C_HEADS):
            hs = slice(h * LANES, (h + 1) * LANES)
            s_ref[h] = _dot_t(q_ref[0, :, hs], k_ref[0, pl.ds(k0, tq), hs])

    def consume(kb, s_ref, masked):
        k0 = pl.multiple_of(kb * tq, tq)
        for h in range(C_HEADS):
            ps = slice((h // 2) * LANES, (h // 2 + 1) * LANES)
            s = s_ref[h]
            if masked:
                s = jnp.where(diag_mask, s, NEG)
            m_old = m_ref[h]
            m_new = jnp.maximum(m_old, _row_all_lanes(
                jnp.max, _fold(jnp.maximum, s[:, :LANES], s[:, LANES:])))
            alpha = jnp.exp2(m_old - m_new)
            p = jnp.exp2(s - jnp.tile(m_new, (1, tq // LANES)))
            l_ref[h] = alpha * l_ref[h] + _fold(jnp.add, p[:, :LANES], p[:, LANES:])
            acc_ref[h] = alpha * acc_ref[h] + _dot(p.astype(BF16), v_ref[0, pl.ds(k0, tq), ps])
            m_ref[h] = m_new

    m_ref[...] = jnp.full(m_ref.shape, NEG, F32)
    l_ref[...] = jnp.zeros(l_ref.shape, F32)
    acc_ref[...] = jnp.zeros(acc_ref.shape, F32)

    produce(0, s0_ref)

    def body(j, carry):
        produce(2 * j + 1, s1_ref)
        consume(2 * j, s0_ref, False)
        produce(2 * j + 2, s0_ref)
        consume(2 * j + 1, s1_ref, False)
        return carry
    lax.fori_loop(0, nfull // 2, body, 0)

    @pl.when(nfull % 2 == 0)
    def _():
        consume(nfull, s0_ref, True)

    @pl.when(nfull % 2 == 1)
    def _():
        produce(nfull, s1_ref)
        consume(nfull - 1, s0_ref, False)
        consume(nfull, s1_ref, True)
    _write_heads(o_ref, acc_ref, l_ref, C_HEADS)


def _mla(q, k, v, *, tq=512):
    bsz, seq, _ = q.shape
    assert seq % tq == 0
    stat = pltpu.VMEM((C_HEADS, tq, LANES), F32)
    return pl.pallas_call(
        functools.partial(_mla_kernel, tq=tq),
        grid=(bsz, seq // tq),
        in_specs=[pl.BlockSpec((1, tq, C_QW), lambda b, i: (b, i, 0)),
                  _resident((1, seq, C_QW), lambda b, i: (b, 0, 0)),
                  _resident((1, seq, 384), lambda b, i: (b, 0, 0))],
        out_specs=pl.BlockSpec((1, tq, 384), lambda b, i: (b, i, 0)),
        out_shape=jax.ShapeDtypeStruct((bsz, seq, 384), BF16),
        scratch_shapes=[pltpu.VMEM((C_HEADS, tq, tq), F32), pltpu.VMEM((C_HEADS, tq, tq), F32),
                        stat, stat, stat],
        compiler_params=pltpu.CompilerParams(
            dimension_semantics=("arbitrary", "arbitrary"), vmem_limit_bytes=VMEM_LIMIT),
        name="mla",
    )(q, k, v)


GROUP_COLS = 32
I32_ALL = -1


def _i32(x):
    return int(np.uint32(x).view(np.int32))


def _bit_planes(words):
    a = list(reversed(words))
    j, m = 16, 0x0000FFFF
    while j:
        k = 0
        while k < 32:
            t = (a[k] ^ lax.shift_right_logical(a[k + j], jnp.int32(j))) & jnp.int32(_i32(m))
            a[k] = a[k] ^ t
            a[k + j] = a[k + j] ^ lax.shift_left(t, jnp.int32(j))
            k = (k + j + 1) & ~j
        j >>= 1
        m = (m ^ (m << j)) & 0xFFFFFFFF
    return list(reversed(a))


def _dsa_kernel(iq_ref, iw_ref, ik_ref, q_ref, k_ref, v_ref, o_ref,
                key_ref, qm_ref, cand_ref, great_ref, sel_ref, m_ref, l_ref, acc_ref,
                *, tq, tk, topk):
    q0 = pl.program_id(1) * tq
    nkb = (q0 + tq + tk - 1) // tk
    cols_per_blk = tk // LANES
    blks_per_group = GROUP_COLS // cols_per_blk
    n_groups = key_ref.shape[0] // blks_per_group
    row_chunk = (q0 + lax.broadcasted_iota(jnp.int32, (tq, tk), 0)) // CHUNK
    col_chunk0 = lax.broadcasted_iota(jnp.int32, (tq, tk), 1) // CHUNK

    def admissible(kb):
        return (col_chunk0 + kb * (tk // CHUNK)) <= row_chunk

    iq = iq_ref[0]
    head_of_lane = lax.broadcasted_iota(jnp.int32, iq.shape, 1) // IDX_DIM
    for h in range(IDX_HEADS):
        qm_ref[h * tq:(h + 1) * tq, :] = jnp.where(head_of_lane == h, iq, jnp.zeros_like(iq))
    w = iw_ref[0]

    sub = 256

    def score_body(kb, carry):
        adm = admissible(kb)
        for c in range(tk // sub):
            k0 = pl.multiple_of(kb * tk + c * sub, sub)
            d = _dot_t(qm_ref[...], ik_ref[0, pl.ds(k0, sub), :])
            sc = jnp.zeros((tq, sub), F32)
            for h in range(IDX_HEADS):
                sc = sc + jnp.maximum(d[h * tq:(h + 1) * tq], 0.0) * w[:, h:h + 1]
            sc = jnp.where(adm[:, c * sub:(c + 1) * sub], sc, NEG)
            bits = pltpu.bitcast(sc, jnp.int32)
            key_ref[kb, :, c * sub:(c + 1) * sub] = bits ^ ((bits >> 31) | jnp.int32(INT_MIN))
        return carry

    lax.fori_loop(0, nkb, score_body, 0)

    def clear_body(kb, carry):
        key_ref[kb] = jnp.zeros((tq, tk), jnp.int32)
        return carry
    n_active = (nkb + blks_per_group - 1) // blks_per_group
    lax.fori_loop(nkb, n_active * blks_per_group, clear_body, 0)

    def slot(g, j):
        return g * blks_per_group + j // cols_per_blk, slice((j % cols_per_blk) * LANES,
                                                             (j % cols_per_blk + 1) * LANES)

    for g in range(n_groups):
        @pl.when(nkb > g * blks_per_group)
        def _(g=g):
            def transpose_rows(rg, carry):
                rows = pl.ds(pl.multiple_of(rg * 8, 8), 8)
                cols = [slot(g, j) for j in range(GROUP_COLS)]
                planes = _bit_planes([key_ref[blk, rows, ls] for blk, ls in cols])
                for (blk, ls), p in zip(cols, planes):
                    key_ref[blk, rows, ls] = p
                return carry
            lax.fori_loop(0, tq // 8, transpose_rows, 0)

    for g in range(n_groups):
        n_cols = jnp.clip((nkb - g * blks_per_group) * cols_per_blk, 0, GROUP_COLS)
        word = jnp.where(n_cols >= GROUP_COLS, jnp.int32(I32_ALL),
                         lax.shift_left(jnp.int32(1), jnp.minimum(n_cols, GROUP_COLS - 1)) - 1)
        cand_ref[g] = jnp.full((tq, LANES), word, jnp.int32)
        great_ref[g] = jnp.zeros((tq, LANES), jnp.int32)

    def refine(plane_of, need, prefer_set, groups=n_groups):
        picked = []
        for g in range(groups):
            p = plane_of(g)
            picked.append(cand_ref[g] & (p if prefer_set else ~p))
        cnt = lax.population_count(picked[0])
        for x in picked[1:]:
            cnt = cnt + lax.population_count(x)
        c = _row_all_lanes(jnp.sum, cnt.astype(F32))
        keep = c >= need
        for g in range(groups):
            great_ref[g] = jnp.where(keep, great_ref[g], great_ref[g] | picked[g])
            cand_ref[g] = jnp.where(keep, picked[g], cand_ref[g] ^ picked[g])
        return jnp.where(keep, need, need - c)

    def value_bits(groups):
        def body(i, need):
            blk = blks_per_group - 1 - i
            for c in reversed(range(cols_per_blk)):
                ls = slice(c * LANES, (c + 1) * LANES)
                need = refine(lambda g: key_ref[g * blks_per_group + blk, :, ls], need, True, groups)
            return need
        return lambda need: lax.fori_loop(0, blks_per_group, body, need)

    need = lax.switch(n_active - 1, [value_bits(g + 1) for g in range(n_groups)],
                      jnp.full((tq, LANES), float(topk), F32))

    def index_bits(need):
        lane = lax.broadcasted_iota(jnp.int32, (tq, LANES), 1)
        col_bit_words = (0xAAAAAAAA, 0xCCCCCCCC, 0xF0F0F0F0, 0xFF00FF00, 0xFFFF0000)
        n_group_bits = max(n_groups - 1, 0).bit_length()
        for t in reversed(range(n_group_bits)):
            need = refine(lambda g: jnp.full((tq, LANES), I32_ALL if (g >> t) & 1 else 0, jnp.int32),
                          need, False)
        for t in reversed(range(5)):
            need = refine(lambda g: jnp.full((tq, LANES), _i32(col_bit_words[t]), jnp.int32),
                          need, False)
        for t in reversed(range(7)):
            need = refine(lambda g: jnp.where((lane >> t) & 1 == 1, jnp.int32(I32_ALL), 0),
                          need, False)
        return need

    n_cand = cand_ref[0] * 0
    for g in range(n_groups):
        n_cand = n_cand + lax.population_count(cand_ref[g])
    has_tie = jnp.max(jnp.where(_row_all_lanes(jnp.sum, n_cand.astype(F32)) > need, 1, 0)) > 0
    lax.cond(has_tie, index_bits, lambda need: need, need)
    for g in range(n_groups):
        sel_ref[g] = great_ref[g] | cand_ref[g]

    q = q_ref[0]
    qlane = lax.broadcasted_iota(jnp.int32, (tq, LANES), 1)
    m_ref[...] = jnp.full(m_ref.shape, NEG, F32)
    l_ref[...] = jnp.zeros(l_ref.shape, F32)
    acc_ref[...] = jnp.zeros(acc_ref.shape, F32)

    def attn_body(kb, carry):
        k0 = pl.multiple_of(kb * tk, tk)
        words = sel_ref[kb // blks_per_group]
        first_col = (kb % blks_per_group) * cols_per_blk
        taken = jnp.concatenate(
            [lax.shift_right_logical(words, jnp.full(words.shape, first_col + c, jnp.int32)) & 1
             for c in range(cols_per_blk)], axis=1)
        bias = jnp.where(taken == 1, jnp.where(admissible(kb), 0.0, 2 * NEG), 2 * NEG)
        for h in range(B_HEADS):
            ps = slice((h // 2) * LANES, (h // 2 + 1) * LANES)
            qp = q[:, ps]
            qh = jnp.where((qlane < HEAD_DIM) if h % 2 == 0 else (qlane >= HEAD_DIM),
                           qp, jnp.zeros_like(qp))
            s = _dot_t(qh, k_ref[0, pl.ds(k0, tk), ps]) + bias
            m_old = m_ref[h]
            m_new = jnp.maximum(m_old, _row_all_lanes(
                jnp.max, _fold(jnp.maximum, s[:, :LANES], s[:, LANES:])))
            alpha = jnp.exp2(m_old - m_new)
            p = jnp.exp2(s - jnp.tile(m_new, (1, tk // LANES)))
            l_ref[h] = alpha * l_ref[h] + _fold(jnp.add, p[:, :LANES], p[:, LANES:])
            acc_ref[h] = alpha * acc_ref[h] + _dot(p.astype(BF16), v_ref[0, pl.ds(k0, tk), ps])
            m_ref[h] = m_new
        return carry

    lax.fori_loop(0, nkb, attn_body, 0)
    _write_heads(o_ref, acc_ref, l_ref, B_HEADS)


def _dsa(iq, iw, ik, q, k, v, *, tq=512, tk=512):
    bsz, seq, _ = q.shape
    assert seq % tq == 0 and seq % tk == 0 and GROUP_COLS % (tk // LANES) == 0
    topk = min(TOPK_MAX, seq // 4)
    blks_per_group = GROUP_COLS // (tk // LANES)
    n_groups = -(-(seq // tk) // blks_per_group)
    qmap = lambda b, i: (b, i, 0)
    full = lambda b, i: (b, 0, 0)
    stat = pltpu.VMEM((B_HEADS, tq, LANES), F32)
    words = pltpu.VMEM((n_groups, tq, LANES), jnp.int32)
    return pl.pallas_call(
        functools.partial(_dsa_kernel, tq=tq, tk=tk, topk=topk),
        grid=(bsz, seq // tq),
        in_specs=[pl.BlockSpec((1, tq, IQ_W), qmap),
                  pl.BlockSpec((1, tq, IDX_HEADS), qmap),
                  _resident((1, seq, IQ_W), full),
                  pl.BlockSpec((1, tq, B_WP), qmap),
                  _resident((1, seq, B_WP), full),
                  _resident((1, seq, B_WP), full)],
        out_specs=pl.BlockSpec((1, tq, B_WP), qmap),
        out_shape=jax.ShapeDtypeStruct((bsz, seq, B_WP), BF16),
        scratch_shapes=[pltpu.VMEM((n_groups * blks_per_group, tq, tk), jnp.int32),
                        pltpu.VMEM((IDX_HEADS * tq, IQ_W), BF16),
                        words, words, words,
                        stat, stat, stat],
        compiler_params=pltpu.CompilerParams(
            dimension_semantics=("arbitrary", "arbitrary"), vmem_limit_bytes=VMEM_LIMIT),
        name="dsa",
    )(iq, iw, ik, q, k, v)


def _pad_cols(w, n):
    return jnp.pad(w, ((0, 0), (0, n - w.shape[1])))


def _prep_proj_weights(w_in, c_w_uq, c_w_ukv):
    offs = np.cumsum([0, A_W, A_W, A_W, B_W, B_W, B_W, IQ_W, IDX_DIM, IDX_HEADS,
                      C_Q_RANK, C_KV_RANK, C_ROPE])
    (aq, ak, av, bq, bk, bv, iq, ik, iw, cq, ckv, ckr) = [
        w_in[:, offs[j]:offs[j + 1]] for j in range(12)]
    ascale = HEAD_DIM ** -0.5
    bscale = HEAD_DIM ** -0.5 * LOG2E
    zeros64 = jnp.zeros((D_MODEL, 64), F32)
    zeros32 = jnp.zeros((D_MODEL, 32), F32)
    kr_group = jnp.concatenate([zeros64, ckr, zeros32], axis=1)
    plain = [aq * ascale, ak, av, _pad_cols(bv, B_WP), cq, ckv,
             _pad_cols(iw * (IDX_HEADS * IDX_DIM) ** -0.5, 128)]
    rope_main = [_pad_cols(bq * bscale, B_WP), _pad_cols(bk, B_WP), iq, jnp.tile(ik, (1, IDX_HEADS)),
                 kr_group]
    wcat = jnp.concatenate(plain + rope_main, axis=1).astype(BF16)

    cscale = (C_NOPE + C_ROPE) ** -0.5 * LOG2E
    uq = c_w_uq.reshape(C_Q_RANK, C_HEADS, C_NOPE + C_ROPE) * cscale
    zq = jnp.zeros((C_Q_RANK, C_HEADS, 32), F32)
    wuq = jnp.concatenate([uq, zq], axis=-1).reshape(C_Q_RANK, C_QW).astype(BF16)
    ukv = c_w_ukv.reshape(C_KV_RANK, C_HEADS, C_NOPE + C_V)
    wukn = jnp.concatenate([ukv[..., :C_NOPE], jnp.zeros((C_KV_RANK, C_HEADS, 64), F32)],
                           axis=-1).reshape(C_KV_RANK, C_QW).astype(BF16)
    wuv = _pad_cols(ukv[..., C_NOPE:].reshape(C_KV_RANK, C_HEADS * C_V), 384).astype(BF16)
    return wcat, wuq, wukn, wuv


def _rope_tables(seq):
    pos = jnp.arange(seq, dtype=F32)[:, None]

    def cs(d):
        inv = ROPE_THETA ** (-jnp.arange(0, d, 2, dtype=F32) / d)
        ang = pos * inv[None, :]
        c, s = jnp.cos(ang), jnp.sin(ang)
        return (jnp.tile(jnp.concatenate([c, c], axis=1), (1, LANES // d)),
                jnp.tile(jnp.concatenate([-s, s], axis=1), (1, LANES // d)))

    c64, s64 = cs(HEAD_DIM)
    c32, s32 = cs(C_ROPE)
    lane = jnp.arange(LANES)[None, :]
    roped = (lane >= C_NOPE) & (lane < C_NOPE + C_ROPE)
    cc = jnp.where(roped, c32, 1.0)
    sc = jnp.where(roped, s32, 0.0)
    return jnp.concatenate([c64, s64, c32, s32, cc, sc], axis=1)


def _chunk_bias(rel_bias):
    rb = rel_bias.astype(F32)
    n_rel = A_QB + A_WIN - 1
    below = jnp.broadcast_to(rb[:, :1], (A_HEADS, A_QB - 1 - (CHUNK - 1)))
    above = jnp.broadcast_to(rb[:, -1:], (A_HEADS, A_PREV + A_QB - 1 - A_REL_MAX))
    by_rel_desc = jnp.concatenate([below, rb, above], axis=1)[:, ::-1]
    assert by_rel_desc.shape[1] == n_rel
    v = jnp.roll(by_rel_desc, -(A_QB - 1), axis=1)
    flat = jnp.broadcast_to(v[:, None, :], (A_HEADS, A_QB, n_rel)).reshape(A_HEADS, A_QB * n_rel)
    toeplitz = flat[:, :A_QB * (n_rel - 1)].reshape(A_HEADS, A_QB, n_rel - 1)[:, :, :A_WIN]
    r = np.arange(A_QB)
    ki = np.arange(A_WIN)[None, :] - CHUNK * (r // CHUNK)[:, None]
    inwin = (ki >= 0) & (ki < (A_LEFT_CHUNKS + 1) * CHUNK)
    b = jnp.where(jnp.asarray(inwin)[None], toeplitz, NEG)
    return b.reshape(A_HEADS // 2, 2 * A_QB, A_WIN)


def _prep_wo(w_out):
    pad64 = jnp.zeros((64, D_MODEL), F32)
    return jnp.concatenate([w_out[:A_W], w_out[A_W:A_W + B_W], pad64,
                            w_out[A_W + B_W:], pad64], axis=0).astype(BF16)


def kernel(x, ffn1_norm, ffn1_w_gate, ffn1_w_up, ffn1_w_down, mix_norm, w_in, a_rel_bias, c_q_norm,
           c_kv_norm, c_w_uq, c_w_ukv, w_out, ffn2_norm, ffn2_w_gate, ffn2_w_up, ffn2_w_down,
           final_norm):
    bsz, seq, _ = x.shape
    depth = w_in.shape[0]
    tab = _rope_tables(seq)
    f1g, f1u, f1d = ffn1_w_gate.astype(BF16), ffn1_w_up.astype(BF16), ffn1_w_down.astype(BF16)
    f2g, f2u, f2d = ffn2_w_gate.astype(BF16), ffn2_w_up.astype(BF16), ffn2_w_down.astype(BF16)
    wcat, wuq, wukn, wuv = jax.vmap(_prep_proj_weights)(w_in, c_w_uq, c_w_ukv)
    abias = jax.vmap(_chunk_bias)(a_rel_bias)
    wo = jax.vmap(_prep_wo)(w_out)
    norms = [n[:, None, :] for n in (ffn1_norm, mix_norm, c_q_norm, c_kv_norm, ffn2_norm)]
    n_ffn1, n_mix, n_cq, n_ckv, n_ffn2 = norms
    x2 = x.reshape(bsz * seq, D_MODEL)
    for l in range(depth):
        x2 = _ffn(x2, None, None, n_ffn1, f1g, f1u, f1d, None, l)
        (aq, ak, av, bq, bk, bv, iq, ik, iw, cq, ck, cv) = _proj(
            x2, bsz, seq, n_mix, wcat, tab, n_cq, n_ckv, wuq, wukn, wuv, l)
        o_a = _chunk_attn(aq, ak, av, abias, l)
        o_b = _dsa(iq, iw, ik, bq, bk, bv)
        o_c = _mla(cq, ck, cv)
        x2 = _ffn(x2, (o_a, o_b, o_c), wo, n_ffn2, f2g, f2u, f2d,
                  final_norm[None] if l == depth - 1 else None, l)
    return x2.reshape(bsz, seq, D_MODEL)
```

```python
import functools
import math

import numpy as np
import jax
import jax.numpy as jnp
from jax import lax
from jax.experimental import pallas as pl
from jax.experimental.pallas import tpu as pltpu

F32 = jnp.float32
BF16 = jnp.bfloat16

D_MODEL = 1024
CHUNK = 64
HEAD_DIM = 64
A_HEADS = 6
A_LEFT_CHUNKS = 8
A_REL_MAX = 128
B_HEADS = 5
IDX_HEADS = 8
IDX_DIM = 32
TOPK_MAX = 256
C_HEADS = 5
C_Q_RANK = 384
C_KV_RANK = 256
C_NOPE = 64
C_ROPE = 32
C_V = 64
D_FF = 2816
ROPE_THETA = 10000.0
EPS = 1e-6
NEG = -1e30
LOG2E = math.log2(math.e)

LANES = 128
A_W = A_HEADS * HEAD_DIM
B_W = B_HEADS * HEAD_DIM
B_WP = 384
C_QW = C_HEADS * LANES
IQ_W = IDX_HEADS * IDX_DIM
VMEM_LIMIT = 56 * 1024 * 1024

OFF_AQ, OFF_AK, OFF_AV, OFF_BV, OFF_CQ, OFF_CKV, OFF_IW = 0, 384, 768, 1152, 1536, 1920, 2176
OFF_ROPE = 2304
ROPE_W = 384 + 384 + 256 + 256 + 128
R_BQ, R_BK, R_IQ, R_IK, R_KR = 0, 384, 768, 1024, 1280
W_CAT = OFF_ROPE + ROPE_W


INT_MIN = -(2 ** 31)


def _dot(a, b):
    return jnp.dot(a, b, preferred_element_type=F32)


def _dot_t(a, b):
    return lax.dot_general(a, b, (((1,), (1,)), ((), ())), preferred_element_type=F32)


def _rms(x, g):
    ms = jnp.mean(x * x, axis=-1, keepdims=True)
    return x * lax.rsqrt(ms + EPS) * g


def _layer(l, *tail):
    return pl.BlockSpec((None,) + tail, lambda *_: (l,) + (0,) * len(tail))


def _ffn_kernel(*refs, has_attn, has_final, tf):
    it = iter(refs)
    x_ref = next(it)
    if has_attn:
        oa_ref, ob_ref, oc_ref, wo_ref = next(it), next(it), next(it), next(it)
    g_ref, wg_ref, wu_ref, wd_ref = next(it), next(it), next(it), next(it)
    fg_ref = next(it) if has_final else None
    o_ref = next(it)
    acc_ref = next(it)

    x = x_ref[...]
    if has_attn:
        x = x + _dot(oa_ref[0], wo_ref[0:384, :])
        x = x + _dot(ob_ref[0], wo_ref[384:768, :])
        x = x + _dot(oc_ref[0], wo_ref[768:1152, :])
    xn = _rms(x, g_ref[...]).astype(BF16)
    for j in range(D_FF // tf):
        g = _dot(xn, wg_ref[:, j * tf:(j + 1) * tf])
        u = _dot(xn, wu_ref[:, j * tf:(j + 1) * tf])
        h = (g * jax.nn.sigmoid(g) * u).astype(BF16)
        c = _dot(h, wd_ref[j * tf:(j + 1) * tf, :])
        if j == 0:
            acc_ref[...] = c
        else:
            acc_ref[...] += c
    y = x + 0.5 * acc_ref[...]
    if has_final:
        y = _rms(y, fg_ref[...])
    o_ref[...] = y


def _ffn(x2, attn, wo, g, wg, wu, wd, final_g, l, *, tm=512, tf=256):
    n = x2.shape[0]
    has_attn = attn is not None
    has_final = final_g is not None
    const = lambda i: (0, 0)
    row = lambda i: (i, 0)
    in_specs = [pl.BlockSpec((tm, D_MODEL), row)]
    args = [x2]
    if has_attn:
        s = attn[0].shape[1]
        nt = s // tm
        amap = lambda i: (i // nt, i % nt, 0)
        for a in attn:
            in_specs.append(pl.BlockSpec((1, tm, 384), amap))
            args.append(a)
        in_specs.append(_layer(l, 1152, D_MODEL))
        args.append(wo)
    in_specs += [_layer(l, 1, D_MODEL), _layer(l, D_MODEL, D_FF), _layer(l, D_MODEL, D_FF),
                 _layer(l, D_FF, D_MODEL)]
    args += [g, wg, wu, wd]
    if has_final:
        in_specs.append(pl.BlockSpec((1, D_MODEL), const))
        args.append(final_g)
    return pl.pallas_call(
        functools.partial(_ffn_kernel, has_attn=has_attn, has_final=has_final, tf=tf),
        grid=(n // tm,),
        in_specs=in_specs,
        out_specs=pl.BlockSpec((tm, D_MODEL), row),
        out_shape=jax.ShapeDtypeStruct((n, D_MODEL), F32),
        scratch_shapes=[pltpu.VMEM((tm, D_MODEL), F32)],
        compiler_params=pltpu.CompilerParams(
            dimension_semantics=("arbitrary",), vmem_limit_bytes=VMEM_LIMIT),
        name="ffn",
    )(*args)


def _proj_kernel(x_ref, g_ref, w_ref, tab_ref, qn_ref, kvn_ref, wuq_ref, wukn_ref, wuv_ref,
                 aq_ref, ak_ref, av_ref, bq_ref, bk_ref, bv_ref, iq_ref, ik_ref, iw_ref,
                 cq_ref, ck_ref, cv_ref):
    xn = _rms(x_ref[...], g_ref[...]).astype(BF16)

    def mm(off, n):
        return _dot(xn, w_ref[:, off:off + n])

    aq_ref[0] = mm(OFF_AQ, 384).astype(BF16)
    ak_ref[0] = mm(OFF_AK, 384).astype(BF16)
    av_ref[0] = mm(OFF_AV, 384).astype(BF16)
    bv_ref[0] = mm(OFF_BV, 384).astype(BF16)
    iw_ref[0] = mm(OFF_IW, 128)[:, :IDX_HEADS]

    cos64, sin64 = tab_ref[:, 0:128], tab_ref[:, 128:256]
    cos32, sin32 = tab_ref[:, 256:384], tab_ref[:, 384:512]
    cosc, sinc = tab_ref[:, 512:640], tab_ref[:, 640:768]
    lane = lax.broadcasted_iota(jnp.int32, (x_ref.shape[0], LANES), 1)

    def rope(x, d, cos, sin):
        partner = jnp.where(lane % d < d // 2, pltpu.roll(x, LANES - d // 2, axis=1),
                            pltpu.roll(x, d // 2, axis=1))
        return x * cos + partner * sin

    def rope_group(r_off, n, d, cos, sin):
        m = mm(OFF_ROPE + r_off, n)
        return [rope(m[:, c * 128:(c + 1) * 128], d, cos, sin) for c in range(n // 128)]

    for c, v in enumerate(rope_group(R_BQ, 384, HEAD_DIM, cos64, sin64)):
        bq_ref[0, :, c * 128:(c + 1) * 128] = v.astype(BF16)
    for c, v in enumerate(rope_group(R_BK, 384, HEAD_DIM, cos64, sin64)):
        bk_ref[0, :, c * 128:(c + 1) * 128] = v.astype(BF16)
    for c, v in enumerate(rope_group(R_IQ, 256, IDX_DIM, cos32, sin32)):
        iq_ref[0, :, c * 128:(c + 1) * 128] = v.astype(BF16)
    for c, v in enumerate(rope_group(R_IK, 256, IDX_DIM, cos32, sin32)):
        ik_ref[0, :, c * 128:(c + 1) * 128] = v.astype(BF16)
    krg = rope_group(R_KR, 128, C_ROPE, cosc, sinc)[0]

    cqn = _rms(mm(OFF_CQ, C_Q_RANK), qn_ref[...]).astype(BF16)
    qm = _dot(cqn, wuq_ref[...])
    ckn = _rms(mm(OFF_CKV, C_KV_RANK), kvn_ref[...]).astype(BF16)
    kn = _dot(ckn, wukn_ref[...])
    for h in range(C_HEADS):
        sl = slice(h * 128, (h + 1) * 128)
        cq_ref[0, :, sl] = rope(qm[:, sl], C_ROPE, cosc, sinc).astype(BF16)
        ck_ref[0, :, sl] = (kn[:, sl] + krg).astype(BF16)
    cv_ref[0] = _dot(ckn, wuv_ref[...]).astype(BF16)


def _proj(x2, bsz, seq, g, wcat, tab, qn, kvn, wuq, wukn, wuv, l, *, tm=512):
    nt = seq // tm
    const = lambda i: (0, 0)
    omap = lambda i: (i // nt, i % nt, 0)
    widths = [384, 384, 384, 384, 384, 384, IQ_W, IQ_W, IDX_HEADS, C_QW, C_QW, 384]
    dtypes = [BF16] * 8 + [F32] + [BF16] * 3
    return pl.pallas_call(
        _proj_kernel,
        grid=(bsz * nt,),
        in_specs=[pl.BlockSpec((tm, D_MODEL), lambda i: (i, 0)),
                  _layer(l, 1, D_MODEL),
                  _layer(l, D_MODEL, W_CAT),
                  pl.BlockSpec((tm, 768), lambda i: (i % nt, 0)),
                  _layer(l, 1, C_Q_RANK),
                  _layer(l, 1, C_KV_RANK),
                  _layer(l, C_Q_RANK, C_QW),
                  _layer(l, C_KV_RANK, C_QW),
                  _layer(l, C_KV_RANK, 384)],
        out_specs=[pl.BlockSpec((1, tm, w), omap) for w in widths],
        out_shape=[jax.ShapeDtypeStruct((bsz, seq, w), d) for w, d in zip(widths, dtypes)],
        compiler_params=pltpu.CompilerParams(
            dimension_semantics=("arbitrary",), vmem_limit_bytes=VMEM_LIMIT),
        name="proj",
    )(x2, g, wcat, tab, qn, kvn, wuq, wukn, wuv)


A_QB = 2 * CHUNK
A_WIN = (A_LEFT_CHUNKS + 2) * CHUNK
A_PREV = A_LEFT_CHUNKS * CHUNK


def _chunk_attn_kernel(q_ref, kp_ref, kc_ref, vp_ref, vc_ref, bias_ref, o_ref, kcat_ref, vcat_ref,
                       *, tq):
    i = pl.program_id(1)
    kcat_ref[0:A_PREV, :] = kp_ref[0]
    kcat_ref[A_PREV:A_PREV + tq, :] = kc_ref[0]
    vcat_ref[0:A_PREV, :] = vp_ref[0]
    vcat_ref[A_PREV:A_PREV + tq, :] = vc_ref[0]
    lane = lax.broadcasted_iota(jnp.int32, (A_QB, LANES), 1)
    col_chunk = lax.broadcasted_iota(jnp.int32, (2 * A_QB, A_WIN), 1) // CHUNK
    for qb in range(tq // A_QB):
        first_chunk = i * (tq // CHUNK) + 2 * qb - A_LEFT_CHUNKS
        valid = (col_chunk + first_chunk) >= 0
        for p in range(A_HEADS // 2):
            ls = slice(p * LANES, (p + 1) * LANES)
            q = q_ref[0, qb * A_QB:(qb + 1) * A_QB, ls]
            qs = jnp.concatenate([jnp.where(lane < HEAD_DIM, q, jnp.zeros_like(q)),
                                  jnp.where(lane >= HEAD_DIM, q, jnp.zeros_like(q))], axis=0)
            kw = kcat_ref[qb * A_QB:qb * A_QB + A_WIN, ls]
            vw = vcat_ref[qb * A_QB:qb * A_QB + A_WIN, ls]
            s = _dot_t(qs, kw) + bias_ref[p]
            s = jnp.where(valid, s, NEG)
            m = jnp.max(s, axis=1, keepdims=True)
            e = jnp.exp(s - m)
            l = jnp.sum(e, axis=1, keepdims=True)
            o = _dot(e.astype(BF16), vw) / l
            o_ref[0, qb * A_QB:(qb + 1) * A_QB, ls] = jnp.where(
                lane < HEAD_DIM, o[0:A_QB], o[A_QB:2 * A_QB]).astype(BF16)


def _chunk_attn(q, k, v, bias, l, *, tq=512):
    bsz, seq, _ = q.shape
    assert tq == A_PREV and seq % tq == 0
    cur = lambda b, i: (b, i, 0)
    prev = lambda b, i: (b, jnp.maximum(i - 1, 0), 0)
    blk = (1, tq, A_W)
    return pl.pallas_call(
        functools.partial(_chunk_attn_kernel, tq=tq),
        grid=(bsz, seq // tq),
        in_specs=[pl.BlockSpec(blk, cur), pl.BlockSpec(blk, prev), pl.BlockSpec(blk, cur),
                  pl.BlockSpec(blk, prev), pl.BlockSpec(blk, cur),
                  _layer(l, A_HEADS // 2, 2 * A_QB, A_WIN)],
        out_specs=pl.BlockSpec(blk, cur),
        out_shape=jax.ShapeDtypeStruct((bsz, seq, A_W), BF16),
        scratch_shapes=[pltpu.VMEM((A_PREV + tq, A_W), BF16), pltpu.VMEM((A_PREV + tq, A_W), BF16)],
        compiler_params=pltpu.CompilerParams(
            dimension_semantics=("arbitrary", "arbitrary"), vmem_limit_bytes=VMEM_LIMIT),
        name="chunk_attn",
    )(q, k, k, v, v, bias)


def _fold(op, acc, s):
    for c in range(s.shape[1] // LANES):
        acc = op(acc, s[:, c * LANES:(c + 1) * LANES])
    return acc


def _row_all_lanes(op, x):
    return jnp.broadcast_to(op(x, axis=1, keepdims=True), x.shape)


def _pair_select(even_head, odd_head):
    lane = lax.broadcasted_iota(jnp.int32, even_head.shape, 1)
    return jnp.where(lane < HEAD_DIM, even_head, odd_head)


def _write_heads(o_ref, acc_ref, l_ref, n_heads):
    for p in range((n_heads + 1) // 2):
        o0 = acc_ref[2 * p] / _row_all_lanes(jnp.sum, l_ref[2 * p])
        if 2 * p + 1 < n_heads:
            o1 = acc_ref[2 * p + 1] / _row_all_lanes(jnp.sum, l_ref[2 * p + 1])
        else:
            o1 = jnp.zeros_like(o0)
        o_ref[0, :, p * LANES:(p + 1) * LANES] = _pair_select(o0, o1).astype(BF16)


def _resident(shape, index_map):
    return pl.BlockSpec(shape, index_map, pipeline_mode=pl.Buffered(1))


def _mla_kernel(q_ref, k_ref, v_ref, o_ref, s0_ref, s1_ref, m_ref, l_ref, acc_ref, *, tq):
    q0 = pl.multiple_of(pl.program_id(1) * tq, tq)
    nfull = q0 // tq
    diag_mask = (lax.broadcasted_iota(jnp.int32, (tq, tq), 1) // CHUNK
                 <= lax.broadcasted_iota(jnp.int32, (tq, tq), 0) // CHUNK)

    def produce(kb, s_ref):
        k0 = pl.multiple_of(kb * tq, tq)
        for h in range(C_HEADS):
            hs = slice(h * LANES, (h + 1) * LANES)
            s_ref[h] = _dot_t(q_ref[0, :, hs], k_ref[0, pl.ds(k0, tq), hs])

    def consume(kb, s_ref, masked):
        k0 = pl.multiple_of(kb * tq, tq)
        for h in range(C_HEADS):
            ps = slice((h // 2) * LANES, (h // 2 + 1) * LANES)
            s = s_ref[h]
            if masked:
                s = jnp.where(diag_mask, s, NEG)
            m_old = m_ref[h]
            m_new = jnp.maximum(m_old, _row_all_lanes(
                jnp.max, _fold(jnp.maximum, s[:, :LANES], s[:, LANES:])))
            alpha = jnp.exp2(m_old - m_new)
            p = jnp.exp2(s - jnp.tile(m_new, (1, tq // LANES)))
            l_ref[h] = alpha * l_ref[h] + _fold(jnp.add, p[:, :LANES], p[:, LANES:])
            acc_ref[h] = alpha * acc_ref[h] + _dot(p.astype(BF16), v_ref[0, pl.ds(k0, tq), ps])
            m_ref[h] = m_new

    m_ref[...] = jnp.full(m_ref.shape, NEG, F32)
    l_ref[...] = jnp.zeros(l_ref.shape, F32)
    acc_ref[...] = jnp.zeros(acc_ref.shape, F32)

    produce(0, s0_ref)

    def body(j, carry):
        produce(2 * j + 1, s1_ref)
        consume(2 * j, s0_ref, False)
        produce(2 * j + 2, s0_ref)
        consume(2 * j + 1, s1_ref, False)
        return carry
    lax.fori_loop(0, nfull // 2, body, 0)

    @pl.when(nfull % 2 == 0)
    def _():
        consume(nfull, s0_ref, True)

    @pl.when(nfull % 2 == 1)
    def _():
        produce(nfull, s1_ref)
        consume(nfull - 1, s0_ref, False)
        consume(nfull, s1_ref, True)
    _write_heads(o_ref, acc_ref, l_ref, C_HEADS)


def _mla(q, k, v, *, tq=512):
    bsz, seq, _ = q.shape
    assert seq % tq == 0
    stat = pltpu.VMEM((C_HEADS, tq, LANES), F32)
    return pl.pallas_call(
        functools.partial(_mla_kernel, tq=tq),
        grid=(bsz, seq // tq),
        in_specs=[pl.BlockSpec((1, tq, C_QW), lambda b, i: (b, i, 0)),
                  _resident((1, seq, C_QW), lambda b, i: (b, 0, 0)),
                  _resident((1, seq, 384), lambda b, i: (b, 0, 0))],
        out_specs=pl.BlockSpec((1, tq, 384), lambda b, i: (b, i, 0)),
        out_shape=jax.ShapeDtypeStruct((bsz, seq, 384), BF16),
        scratch_shapes=[pltpu.VMEM((C_HEADS, tq, tq), F32), pltpu.VMEM((C_HEADS, tq, tq), F32),
                        stat, stat, stat],
        compiler_params=pltpu.CompilerParams(
            dimension_semantics=("arbitrary", "arbitrary"), vmem_limit_bytes=VMEM_LIMIT),
        name="mla",
    )(q, k, v)


GROUP_COLS = 32
I32_ALL = -1


def _i32(x):
    return int(np.uint32(x).view(np.int32))


def _bit_planes(words):
    a = list(reversed(words))
    j, m = 16, 0x0000FFFF
    while j:
        k = 0
        while k < 32:
            t = (a[k] ^ lax.shift_right_logical(a[k + j], jnp.int32(j))) & jnp.int32(_i32(m))
            a[k] = a[k] ^ t
            a[k + j] = a[k + j] ^ lax.shift_left(t, jnp.int32(j))
            k = (k + j + 1) & ~j
        j >>= 1
        m = (m ^ (m << j)) & 0xFFFFFFFF
    return list(reversed(a))


def _dsa_kernel(iq_ref, iw_ref, ik_ref, q_ref, k_ref, v_ref, o_ref,
                key_ref, qm_ref, cand_ref, great_ref, sel_ref, qs_ref, m_ref, l_ref, acc_ref,
                *, tq, tk, topk):
    q0 = pl.program_id(1) * tq
    nkb = (q0 + tq + tk - 1) // tk
    cols_per_blk = tk // LANES
    blks_per_group = GROUP_COLS // cols_per_blk
    n_groups = key_ref.shape[0] // blks_per_group
    row_chunk = (q0 + lax.broadcasted_iota(jnp.int32, (tq, tk), 0)) // CHUNK
    col_chunk0 = lax.broadcasted_iota(jnp.int32, (tq, tk), 1) // CHUNK

    def admissible(kb):
        return (col_chunk0 + kb * (tk // CHUNK)) <= row_chunk

    iq = iq_ref[0]
    head_of_lane = lax.broadcasted_iota(jnp.int32, iq.shape, 1) // IDX_DIM
    for h in range(IDX_HEADS):
        qm_ref[h * tq:(h + 1) * tq, :] = jnp.where(head_of_lane == h, iq, jnp.zeros_like(iq))
    w = iw_ref[0]

    sub = 256

    def score_body(kb, carry):
        adm = admissible(kb)
        for c in range(tk // sub):
            k0 = pl.multiple_of(kb * tk + c * sub, sub)
            d = _dot_t(qm_ref[...], ik_ref[0, pl.ds(k0, sub), :])
            sc = jnp.zeros((tq, sub), F32)
            for h in range(IDX_HEADS):
                sc = sc + jnp.maximum(d[h * tq:(h + 1) * tq], 0.0) * w[:, h:h + 1]
            sc = jnp.where(adm[:, c * sub:(c + 1) * sub], sc, NEG)
            bits = pltpu.bitcast(sc, jnp.int32)
            key_ref[kb, :, c * sub:(c + 1) * sub] = bits ^ ((bits >> 31) | jnp.int32(INT_MIN))
        return carry

    lax.fori_loop(0, nkb, score_body, 0)

    def clear_body(kb, carry):
        key_ref[kb] = jnp.zeros((tq, tk), jnp.int32)
        return carry
    n_active = (nkb + blks_per_group - 1) // blks_per_group
    lax.fori_loop(nkb, n_active * blks_per_group, clear_body, 0)

    def slot(g, j):
        return g * blks_per_group + j // cols_per_blk, slice((j % cols_per_blk) * LANES,
                                                             (j % cols_per_blk + 1) * LANES)

    for g in range(n_groups):
        @pl.when(nkb > g * blks_per_group)
        def _(g=g):
            def transpose_rows(rg, carry):
                rows = pl.ds(pl.multiple_of(rg * 8, 8), 8)
                cols = [slot(g, j) for j in range(GROUP_COLS)]
                planes = _bit_planes([key_ref[blk, rows, ls] for blk, ls in cols])
                for (blk, ls), p in zip(cols, planes):
                    key_ref[blk, rows, ls] = p
                return carry
            lax.fori_loop(0, tq // 8, transpose_rows, 0)

    for g in range(n_groups):
        n_cols = jnp.clip((nkb - g * blks_per_group) * cols_per_blk, 0, GROUP_COLS)
        word = jnp.where(n_cols >= GROUP_COLS, jnp.int32(I32_ALL),
                         lax.shift_left(jnp.int32(1), jnp.minimum(n_cols, GROUP_COLS - 1)) - 1)
        cand_ref[g] = jnp.full((tq, LANES), word, jnp.int32)
        great_ref[g] = jnp.zeros((tq, LANES), jnp.int32)

    def refine(plane_of, need, prefer_set, groups=n_groups):
        picked = []
        for g in range(groups):
            p = plane_of(g)
            picked.append(cand_ref[g] & (p if prefer_set else ~p))
        cnt = lax.population_count(picked[0])
        for x in picked[1:]:
            cnt = cnt + lax.population_count(x)
        c = _row_all_lanes(jnp.sum, cnt.astype(F32))
        keep = c >= need
        for g in range(groups):
            great_ref[g] = jnp.where(keep, great_ref[g], great_ref[g] | picked[g])
            cand_ref[g] = jnp.where(keep, picked[g], cand_ref[g] ^ picked[g])
        return jnp.where(keep, need, need - c)

    def value_bits(groups):
        def body(i, need):
            blk = blks_per_group - 1 - i
            for c in reversed(range(cols_per_blk)):
                ls = slice(c * LANES, (c + 1) * LANES)
                need = refine(lambda g: key_ref[g * blks_per_group + blk, :, ls], need, True, groups)
            return need
        return lambda need: lax.fori_loop(0, blks_per_group, body, need)

    need = lax.switch(n_active - 1, [value_bits(g + 1) for g in range(n_groups)],
                      jnp.full((tq, LANES), float(topk), F32))

    def index_bits(need):
        lane = lax.broadcasted_iota(jnp.int32, (tq, LANES), 1)
        col_bit_words = (0xAAAAAAAA, 0xCCCCCCCC, 0xF0F0F0F0, 0xFF00FF00, 0xFFFF0000)
        n_group_bits = max(n_groups - 1, 0).bit_length()
        for t in reversed(range(n_group_bits)):
            need = refine(lambda g: jnp.full((tq, LANES), I32_ALL if (g >> t) & 1 else 0, jnp.int32),
                          need, False)
        for t in reversed(range(5)):
            need = refine(lambda g: jnp.full((tq, LANES), _i32(col_bit_words[t]), jnp.int32),
                          need, False)
        for t in reversed(range(7)):
            need = refine(lambda g: jnp.where((lane >> t) & 1 == 1, jnp.int32(I32_ALL), 0),
                          need, False)
        return need

    n_cand = cand_ref[0] * 0
    for g in range(n_groups):
        n_cand = n_cand + lax.population_count(cand_ref[g])
    has_tie = jnp.max(jnp.where(_row_all_lanes(jnp.sum, n_cand.astype(F32)) > need, 1, 0)) > 0
    lax.cond(has_tie, index_bits, lambda need: need, need)
    for g in range(n_groups):
        sel_ref[g] = great_ref[g] | cand_ref[g]

    q = q_ref[0]
    qlane = lax.broadcasted_iota(jnp.int32, (tq, LANES), 1)
    n_pairs = (B_HEADS + 1) // 2
    for p in range(n_pairs):
        qp = q[:, p * LANES:(p + 1) * LANES]
        qs_ref[p, 0:tq, :] = jnp.where(qlane < HEAD_DIM, qp, jnp.zeros_like(qp))
        if 2 * p + 1 < B_HEADS:
            qs_ref[p, tq:2 * tq, :] = jnp.where(qlane >= HEAD_DIM, qp, jnp.zeros_like(qp))
    m_ref[...] = jnp.full(m_ref.shape, NEG, F32)
    l_ref[...] = jnp.zeros(l_ref.shape, F32)
    acc_ref[...] = jnp.zeros(acc_ref.shape, F32)

    def attn_body(kb, carry):
        k0 = pl.multiple_of(kb * tk, tk)
        words = sel_ref[kb // blks_per_group]
        first_col = (kb % blks_per_group) * cols_per_blk
        taken = jnp.concatenate(
            [lax.shift_right_logical(words, jnp.full(words.shape, first_col + c, jnp.int32)) & 1
             for c in range(cols_per_blk)], axis=1)
        bias = jnp.where(taken == 1, jnp.where(admissible(kb), 0.0, 2 * NEG), 2 * NEG)
        for p in range(n_pairs):
            ps = slice(p * LANES, (p + 1) * LANES)
            heads = [h for h in (2 * p, 2 * p + 1) if h < B_HEADS]
            s_all = _dot_t(qs_ref[p, 0:len(heads) * tq, :], k_ref[0, pl.ds(k0, tk), ps])
            weights, alphas = [], []
            for e, h in enumerate(heads):
                s = s_all[e * tq:(e + 1) * tq] + bias
                m_old = m_ref[h]
                m_new = jnp.maximum(m_old, _row_all_lanes(
                    jnp.max, _fold(jnp.maximum, s[:, :LANES], s[:, LANES:])))
                alpha = jnp.exp2(m_old - m_new)
                w = jnp.exp2(s - jnp.tile(m_new, (1, tk // LANES)))
                l_ref[h] = alpha * l_ref[h] + _fold(jnp.add, w[:, :LANES], w[:, LANES:])
                m_ref[h] = m_new
                weights.append(w.astype(BF16))
                alphas.append(alpha)
            pv = _dot(jnp.concatenate(weights, axis=0), v_ref[0, pl.ds(k0, tk), ps])
            for e, h in enumerate(heads):
                acc_ref[h] = alphas[e] * acc_ref[h] + pv[e * tq:(e + 1) * tq]
        return carry

    lax.fori_loop(0, nkb, attn_body, 0)
    _write_heads(o_ref, acc_ref, l_ref, B_HEADS)


def _dsa(iq, iw, ik, q, k, v, *, tq=512, tk=512):
    bsz, seq, _ = q.shape
    assert seq % tq == 0 and seq % tk == 0 and GROUP_COLS % (tk // LANES) == 0
    topk = min(TOPK_MAX, seq // 4)
    blks_per_group = GROUP_COLS // (tk // LANES)
    n_groups = -(-(seq // tk) // blks_per_group)
    qmap = lambda b, i: (b, i, 0)
    full = lambda b, i: (b, 0, 0)
    stat = pltpu.VMEM((B_HEADS, tq, LANES), F32)
    words = pltpu.VMEM((n_groups, tq, LANES), jnp.int32)
    return pl.pallas_call(
        functools.partial(_dsa_kernel, tq=tq, tk=tk, topk=topk),
        grid=(bsz, seq // tq),
        in_specs=[pl.BlockSpec((1, tq, IQ_W), qmap),
                  pl.BlockSpec((1, tq, IDX_HEADS), qmap),
                  _resident((1, seq, IQ_W), full),
                  pl.BlockSpec((1, tq, B_WP), qmap),
                  _resident((1, seq, B_WP), full),
                  _resident((1, seq, B_WP), full)],
        out_specs=pl.BlockSpec((1, tq, B_WP), qmap),
        out_shape=jax.ShapeDtypeStruct((bsz, seq, B_WP), BF16),
        scratch_shapes=[pltpu.VMEM((n_groups * blks_per_group, tq, tk), jnp.int32),
                        pltpu.VMEM((IDX_HEADS * tq, IQ_W), BF16),
                        words, words, words,
                        pltpu.VMEM(((B_HEADS + 1) // 2, 2 * tq, LANES), BF16),
                        stat, stat, stat],
        compiler_params=pltpu.CompilerParams(
            dimension_semantics=("arbitrary", "arbitrary"), vmem_limit_bytes=VMEM_LIMIT),
        name="dsa",
    )(iq, iw, ik, q, k, v)


def _pad_cols(w, n):
    return jnp.pad(w, ((0, 0), (0, n - w.shape[1])))


def _prep_proj_weights(w_in, c_w_uq, c_w_ukv):
    offs = np.cumsum([0, A_W, A_W, A_W, B_W, B_W, B_W, IQ_W, IDX_DIM, IDX_HEADS,
                      C_Q_RANK, C_KV_RANK, C_ROPE])
    (aq, ak, av, bq, bk, bv, iq, ik, iw, cq, ckv, ckr) = [
        w_in[:, offs[j]:offs[j + 1]] for j in range(12)]
    ascale = HEAD_DIM ** -0.5
    bscale = HEAD_DIM ** -0.5 * LOG2E
    zeros64 = jnp.zeros((D_MODEL, 64), F32)
    zeros32 = jnp.zeros((D_MODEL, 32), F32)
    kr_group = jnp.concatenate([zeros64, ckr, zeros32], axis=1)
    plain = [aq * ascale, ak, av, _pad_cols(bv, B_WP), cq, ckv,
             _pad_cols(iw * (IDX_HEADS * IDX_DIM) ** -0.5, 128)]
    rope_main = [_pad_cols(bq * bscale, B_WP), _pad_cols(bk, B_WP), iq, jnp.tile(ik, (1, IDX_HEADS)),
                 kr_group]
    wcat = jnp.concatenate(plain + rope_main, axis=1).astype(BF16)

    cscale = (C_NOPE + C_ROPE) ** -0.5 * LOG2E
    uq = c_w_uq.reshape(C_Q_RANK, C_HEADS, C_NOPE + C_ROPE) * cscale
    zq = jnp.zeros((C_Q_RANK, C_HEADS, 32), F32)
    wuq = jnp.concatenate([uq, zq], axis=-1).reshape(C_Q_RANK, C_QW).astype(BF16)
    ukv = c_w_ukv.reshape(C_KV_RANK, C_HEADS, C_NOPE + C_V)
    wukn = jnp.concatenate([ukv[..., :C_NOPE], jnp.zeros((C_KV_RANK, C_HEADS, 64), F32)],
                           axis=-1).reshape(C_KV_RANK, C_QW).astype(BF16)
    wuv = _pad_cols(ukv[..., C_NOPE:].reshape(C_KV_RANK, C_HEADS * C_V), 384).astype(BF16)
    return wcat, wuq, wukn, wuv


def _rope_tables(seq):
    pos = jnp.arange(seq, dtype=F32)[:, None]

    def cs(d):
        inv = ROPE_THETA ** (-jnp.arange(0, d, 2, dtype=F32) / d)
        ang = pos * inv[None, :]
        c, s = jnp.cos(ang), jnp.sin(ang)
        return (jnp.tile(jnp.concatenate([c, c], axis=1), (1, LANES // d)),
                jnp.tile(jnp.concatenate([-s, s], axis=1), (1, LANES // d)))

    c64, s64 = cs(HEAD_DIM)
    c32, s32 = cs(C_ROPE)
    lane = jnp.arange(LANES)[None, :]
    roped = (lane >= C_NOPE) & (lane < C_NOPE + C_ROPE)
    cc = jnp.where(roped, c32, 1.0)
    sc = jnp.where(roped, s32, 0.0)
    return jnp.concatenate([c64, s64, c32, s32, cc, sc], axis=1)


def _chunk_bias(rel_bias):
    rb = rel_bias.astype(F32)
    n_rel = A_QB + A_WIN - 1
    below = jnp.broadcast_to(rb[:, :1], (A_HEADS, A_QB - 1 - (CHUNK - 1)))
    above = jnp.broadcast_to(rb[:, -1:], (A_HEADS, A_PREV + A_QB - 1 - A_REL_MAX))
    by_rel_desc = jnp.concatenate([below, rb, above], axis=1)[:, ::-1]
    assert by_rel_desc.shape[1] == n_rel
    v = jnp.roll(by_rel_desc, -(A_QB - 1), axis=1)
    flat = jnp.broadcast_to(v[:, None, :], (A_HEADS, A_QB, n_rel)).reshape(A_HEADS, A_QB * n_rel)
    toeplitz = flat[:, :A_QB * (n_rel - 1)].reshape(A_HEADS, A_QB, n_rel - 1)[:, :, :A_WIN]
    r = np.arange(A_QB)
    ki = np.arange(A_WIN)[None, :] - CHUNK * (r // CHUNK)[:, None]
    inwin = (ki >= 0) & (ki < (A_LEFT_CHUNKS + 1) * CHUNK)
    b = jnp.where(jnp.asarray(inwin)[None], toeplitz, NEG)
    return b.reshape(A_HEADS // 2, 2 * A_QB, A_WIN)


def _prep_wo(w_out):
    pad64 = jnp.zeros((64, D_MODEL), F32)
    return jnp.concatenate([w_out[:A_W], w_out[A_W:A_W + B_W], pad64,
                            w_out[A_W + B_W:], pad64], axis=0).astype(BF16)


def kernel(x, ffn1_norm, ffn1_w_gate, ffn1_w_up, ffn1_w_down, mix_norm, w_in, a_rel_bias, c_q_norm,
           c_kv_norm, c_w_uq, c_w_ukv, w_out, ffn2_norm, ffn2_w_gate, ffn2_w_up, ffn2_w_down,
           final_norm):
    bsz, seq, _ = x.shape
    depth = w_in.shape[0]
    tab = _rope_tables(seq)
    f1g, f1u, f1d = ffn1_w_gate.astype(BF16), ffn1_w_up.astype(BF16), ffn1_w_down.astype(BF16)
    f2g, f2u, f2d = ffn2_w_gate.astype(BF16), ffn2_w_up.astype(BF16), ffn2_w_down.astype(BF16)
    wcat, wuq, wukn, wuv = jax.vmap(_prep_proj_weights)(w_in, c_w_uq, c_w_ukv)
    abias = jax.vmap(_chunk_bias)(a_rel_bias)
    wo = jax.vmap(_prep_wo)(w_out)
    norms = [n[:, None, :] for n in (ffn1_norm, mix_norm, c_q_norm, c_kv_norm, ffn2_norm)]
    n_ffn1, n_mix, n_cq, n_ckv, n_ffn2 = norms
    x2 = x.reshape(bsz * seq, D_MODEL)
    for l in range(depth):
        x2 = _ffn(x2, None, None, n_ffn1, f1g, f1u, f1d, None, l)
        (aq, ak, av, bq, bk, bv, iq, ik, iw, cq, ck, cv) = _proj(
            x2, bsz, seq, n_mix, wcat, tab, n_cq, n_ckv, wuq, wukn, wuv, l)
        o_a = _chunk_attn(aq, ak, av, abias, l)
        o_b = _dsa(iq, iw, ik, bq, bk, bv)
        o_c = _mla(cq, ck, cv)
        x2 = _ffn(x2, (o_a, o_b, o_c), wo, n_ffn2, f2g, f2u, f2d,
                  final_norm[None] if l == depth - 1 else None, l)
    return x2.reshape(bsz, seq, D_MODEL)
```
